```python
import jax
import jax.numpy as jnp
from jax import lax
import numpy as np


D_MODEL = 1024
BATCH = 8
SEQ = 4096
DEPTH = 4

GRID_W = 64
CTX_LEN = 256
NORM_EPS = 1e-6
ROPE_BASE = 10000.0
NEG_INF = -1e30

A_HEADS = 8
A_KV_HEADS = 2
A_HEAD_DIM = 64
WINDOW = 128
A_BLOCK = 128
B_WIDTH = 512
B_GROUPS = 4
B_CHUNK = 128
C_HEADS = 8
C_Q_LORA = 384
C_KV_LORA = 256
C_NOPE = 64
C_ROPE = 32
C_V = 64
C_BLOCK = 128
D_WIDTH = 512
D_CONV = 3
N_EXPERTS = 16
EXPERT_FF = 1024
EC_FACTOR = 2

A_Q_W = A_HEADS * A_HEAD_DIM
A_KV_W = A_KV_HEADS * A_HEAD_DIM
EVEN_CTX_COLS = 2 * A_KV_W
EVEN_IN = EVEN_CTX_COLS + A_Q_W + 2 * B_WIDTH
EVEN_OUT = A_Q_W + B_WIDTH
ODD_CTX_COLS = C_KV_LORA + C_ROPE
ODD_IN = ODD_CTX_COLS + C_Q_LORA + 3 * D_WIDTH
ODD_OUT = C_HEADS * C_V + D_WIDTH
N_EVEN = (DEPTH + 1) // 2
N_ODD = DEPTH // 2

kernel_name = 'hybrid_diffusion_backbone'


def rms_norm(x, g):
    xf = x.astype(jnp.float32)
    y = xf * lax.rsqrt(jnp.mean(xf * xf, axis=-1, keepdims=True) + NORM_EPS)
    return (y * g.astype(jnp.float32)).astype(x.dtype)


def layer_norm(x, g):
    xf = x.astype(jnp.float32)
    mu = jnp.mean(xf, axis=-1, keepdims=True)
    var = jnp.mean(jnp.square(xf - mu), axis=-1, keepdims=True)
    return ((xf - mu) * lax.rsqrt(var + NORM_EPS) * g.astype(jnp.float32)).astype(x.dtype)


def axial_rope_tables(rows, rot_dim):
    row = jnp.repeat(jnp.arange(rows, dtype=jnp.float32), GRID_W)
    col = jnp.tile(jnp.arange(GRID_W, dtype=jnp.float32), rows)
    n_freq = rot_dim // 4
    inv_freq = ROPE_BASE ** (-jnp.arange(n_freq, dtype=jnp.float32) / n_freq)
    ang = jnp.concatenate([row[:, None] * inv_freq[None, :], col[:, None] * inv_freq[None, :]], axis=-1)
    return jnp.cos(ang), jnp.sin(ang)


def apply_rope(x, cos, sin):
    x1, x2 = jnp.split(x, 2, axis=-1)
    cs = cos[None, :, None, :].astype(x.dtype)
    sn = sin[None, :, None, :].astype(x.dtype)
    return jnp.concatenate([x1 * cs - x2 * sn, x2 * cs + x1 * sn], axis=-1)


def gqa_attend(q, k, v, mask, sink, scale):
    s = jnp.einsum('bqhgd,bkhd->bhgqk', q, k).astype(jnp.float32) * scale
    if mask is not None:
        s = jnp.where(mask, s, NEG_INF)
    if sink is not None:
        sk = jnp.broadcast_to(sink.astype(jnp.float32)[None, :, :, None, None], s.shape[:-1] + (1,))
        s = jnp.concatenate([s, sk], axis=-1)
    p = jax.nn.softmax(s, axis=-1)
    if sink is not None:
        p = p[..., :-1]
    return jnp.einsum('bhgqk,bkhd->bqhgd', p.astype(v.dtype), v)


def window_attention(q, k, v, k_ctx, v_ctx, sink_g, scale):
    bsz, n = q.shape[0], q.shape[1]
    g = A_HEADS // A_KV_HEADS
    nb = n // A_BLOCK
    span = A_BLOCK + 2 * WINDOW
    pad = ((0, 0), (WINDOW, WINDOW), (0, 0), (0, 0))
    kp = jnp.pad(k, pad)
    vp = jnp.pad(v, pad)
    qi = jnp.arange(A_BLOCK)
    kj = jnp.arange(span)
    band = jnp.abs(kj[None, :] - WINDOW - qi[:, None]) <= WINDOW
    ctx_ok = jnp.ones((A_BLOCK, k_ctx.shape[1]), dtype=jnp.bool_)

    def block(nidx):
        start = nidx * A_BLOCK
        qb = lax.dynamic_slice_in_dim(q, start, A_BLOCK, axis=1).reshape(bsz, A_BLOCK, A_KV_HEADS, g, A_HEAD_DIM)
        kb = lax.dynamic_slice_in_dim(kp, start, span, axis=1)
        vb = lax.dynamic_slice_in_dim(vp, start, span, axis=1)
        kpos = start - WINDOW + kj
        mask = band & ((kpos >= 0) & (kpos < n))[None, :]
        mask = jnp.concatenate([mask, ctx_ok], axis=1)
        o = gqa_attend(qb, jnp.concatenate([kb, k_ctx], axis=1), jnp.concatenate([vb, v_ctx], axis=1), mask, sink_g, scale)
        return o.reshape(bsz, A_BLOCK, A_Q_W)

    out = lax.map(block, jnp.arange(nb))
    return jnp.moveaxis(out, 0, 1).reshape(bsz, n, A_Q_W)


def dense_block_attention(q, k_all, v_all, scale):
    bsz, n, h, _ = q.shape
    dv = v_all.shape[-1]
    nb = n // C_BLOCK

    def block(nidx):
        qb = lax.dynamic_slice_in_dim(q, nidx * C_BLOCK, C_BLOCK, axis=1)
        return gqa_attend(qb[:, :, :, None], k_all, v_all, None, None, scale)[:, :, :, 0]

    out = lax.map(block, jnp.arange(nb))
    return jnp.moveaxis(out, 0, 1).reshape(bsz, n, h * dv)


def spatial_gating(z, norm_g, w_s, b_s):
    bsz, n, _ = z.shape
    u, v = jnp.split(jax.nn.gelu(z), 2, axis=-1)
    v = layer_norm(v, norm_g)
    vc = v.reshape(bsz, n // B_CHUNK, B_CHUNK, B_GROUPS, B_WIDTH // B_GROUPS)
    mixed = jnp.einsum('gpq,bcqgd->bcpgd', w_s, vc) + b_s.T[None, None, :, :, None]
    return u * mixed.reshape(bsz, n, B_WIDTH)


def gated_short_conv(z, conv_w):
    gb, gc, hh = jnp.split(z, 3, axis=-1)
    u = gc * hh
    n = u.shape[1]
    half = D_CONV // 2
    up = jnp.pad(u, ((0, 0), (half, half), (0, 0)))
    conv = sum(up[:, j:j + n] * conv_w[j] for j in range(D_CONV))
    return gb * conv


def mla_q(cq, q_norm_g, w_uq, cos, sin):
    bsz, n, _ = cq.shape
    q = (rms_norm(cq, q_norm_g) @ w_uq).reshape(bsz, n, C_HEADS, C_NOPE + C_ROPE)
    q_nope, q_rope = q[..., :C_NOPE], q[..., C_NOPE:]
    if cos is not None:
        q_rope = apply_rope(q_rope, cos, sin)
    return jnp.concatenate([q_nope, q_rope], axis=-1)


def mla_kv(ckv, kr, kv_norm_g, w_ukv, cos, sin):
    bsz, n, _ = ckv.shape
    kv = (rms_norm(ckv, kv_norm_g) @ w_ukv).reshape(bsz, n, C_HEADS, C_NOPE + C_V)
    k_nope, v = kv[..., :C_NOPE], kv[..., C_NOPE:]
    kr = kr[:, :, None, :]
    if cos is not None:
        kr = apply_rope(kr, cos, sin)
    k = jnp.concatenate([k_nope, jnp.broadcast_to(kr, (bsz, n, C_HEADS, C_ROPE))], axis=-1)
    return k, v


def even_mixer(h_lat, h_ctx, w_in, sink, sgu_norm_g, sgu_w, sgu_b, w_out, cos, sin, need_ctx):
    bsz, n_lat, _ = h_lat.shape
    n_ctx = h_ctx.shape[1]
    g = A_HEADS // A_KV_HEADS
    scale = A_HEAD_DIM ** -0.5
    sink_g = sink.reshape(A_KV_HEADS, g)
    z_lat = h_lat @ w_in
    z_ctx = h_ctx @ (w_in if need_ctx else w_in[:, :EVEN_CTX_COLS])
    k_ctx = z_ctx[..., :A_KV_W].reshape(bsz, n_ctx, A_KV_HEADS, A_HEAD_DIM)
    v_ctx = z_ctx[..., A_KV_W:EVEN_CTX_COLS].reshape(bsz, n_ctx, A_KV_HEADS, A_HEAD_DIM)
    k_lat = apply_rope(z_lat[..., :A_KV_W].reshape(bsz, n_lat, A_KV_HEADS, A_HEAD_DIM), cos, sin)
    v_lat = z_lat[..., A_KV_W:EVEN_CTX_COLS].reshape(bsz, n_lat, A_KV_HEADS, A_HEAD_DIM)
    q_lat = apply_rope(z_lat[..., EVEN_CTX_COLS:EVEN_CTX_COLS + A_Q_W].reshape(bsz, n_lat, A_HEADS, A_HEAD_DIM), cos, sin)
    o_lat = window_attention(q_lat, k_lat, v_lat, k_ctx, v_ctx, sink_g, scale)
    s_lat = spatial_gating(z_lat[..., EVEN_CTX_COLS + A_Q_W:], sgu_norm_g, sgu_w, sgu_b)
    y_lat = jnp.concatenate([o_lat, s_lat], axis=-1) @ w_out
    if not need_ctx:
        return y_lat, None
    q_ctx = z_ctx[..., EVEN_CTX_COLS:EVEN_CTX_COLS + A_Q_W].reshape(bsz, n_ctx, A_KV_HEADS, g, A_HEAD_DIM)
    o_ctx = gqa_attend(q_ctx, k_ctx, v_ctx, None, sink_g, scale).reshape(bsz, n_ctx, A_Q_W)
    s_ctx = spatial_gating(z_ctx[..., EVEN_CTX_COLS + A_Q_W:], sgu_norm_g, sgu_w, sgu_b)
    y_ctx = jnp.concatenate([o_ctx, s_ctx], axis=-1) @ w_out
    return y_lat, y_ctx


def odd_mixer(h_lat, h_ctx, w_in, q_norm_g, w_uq, kv_norm_g, w_ukv, conv_w, w_out, cos, sin, need_ctx):
    bsz, n_ctx = h_ctx.shape[0], h_ctx.shape[1]
    scale = (C_NOPE + C_ROPE) ** -0.5
    z_lat = h_lat @ w_in
    z_ctx = h_ctx @ (w_in if need_ctx else w_in[:, :ODD_CTX_COLS])
    k_ctx, v_ctx = mla_kv(z_ctx[..., :C_KV_LORA], z_ctx[..., C_KV_LORA:ODD_CTX_COLS], kv_norm_g, w_ukv, None, None)
    k_lat, v_lat = mla_kv(z_lat[..., :C_KV_LORA], z_lat[..., C_KV_LORA:ODD_CTX_COLS], kv_norm_g, w_ukv, cos, sin)
    q_lat = mla_q(z_lat[..., ODD_CTX_COLS:ODD_CTX_COLS + C_Q_LORA], q_norm_g, w_uq, cos, sin)
    k_all = jnp.concatenate([k_lat, k_ctx], axis=1)
    v_all = jnp.concatenate([v_lat, v_ctx], axis=1)
    o_lat = dense_block_attention(q_lat, k_all, v_all, scale)
    c_lat = gated_short_conv(z_lat[..., ODD_CTX_COLS + C_Q_LORA:], conv_w)
    y_lat = jnp.concatenate([o_lat, c_lat], axis=-1) @ w_out
    if not need_ctx:
        return y_lat, None
    q_ctx = mla_q(z_ctx[..., ODD_CTX_COLS:ODD_CTX_COLS + C_Q_LORA], q_norm_g, w_uq, None, None)
    o_ctx = gqa_attend(q_ctx[:, :, :, None], k_ctx, v_ctx, None, None, scale)[:, :, :, 0].reshape(bsz, n_ctx, C_HEADS * C_V)
    c_ctx_out = gated_short_conv(z_ctx[..., ODD_CTX_COLS + C_Q_LORA:], conv_w)
    y_ctx = jnp.concatenate([o_ctx, c_ctx_out], axis=-1) @ w_out
    return y_lat, y_ctx


def expert_choice_ffn(h, w_router, w1, w3, w2):
    n_tok, d = h.shape[1], h.shape[2]
    cap = EC_FACTOR * n_tok // N_EXPERTS
    aff = jax.nn.softmax(jnp.einsum('bnd,de->bne', h, w_router).astype(jnp.float32), axis=-1)
    gate, idx = lax.top_k(jnp.swapaxes(aff, 1, 2), cap)
    xs = jax.vmap(lambda hb, ib: hb[ib])(h, idx)
    a = jnp.einsum('becd,edf->becf', xs, w1)
    b = jnp.einsum('becd,edf->becf', xs, w3)
    y = jnp.einsum('becf,efd->becd', jax.nn.silu(a) * b, w2) * gate[..., None].astype(h.dtype)

    def combine(yb, ib):
        return jnp.zeros((n_tok, d), h.dtype).at[ib.reshape(-1)].add(yb.reshape(-1, d))

    return jax.vmap(combine)(y, idx)


def setup_inputs(seed: int = 0) -> dict:
    key = jax.random.key(seed)
    ks = iter(jax.random.split(key, 32))
    D = D_MODEL

    def nrm(shape, scale):
        return jax.random.normal(next(ks), shape, jnp.float32) * scale

    return {
        'x': nrm((BATCH, SEQ, D), 1.0),
        'c': nrm((BATCH, D), 1.0),
        'ctx': nrm((BATCH, CTX_LEN, D), 1.0),
        'c_ctx': nrm((D,), 1.0),
        'mod_w': nrm((DEPTH, D, 6 * D), 0.5 * D ** -0.5),
        'mod_b': nrm((DEPTH, 6 * D), 0.01),
        'norm1_g': 1.0 + nrm((DEPTH, D), 0.05),
        'norm2_g': 1.0 + nrm((DEPTH, D), 0.05),
        'ev_w_in': nrm((N_EVEN, D, EVEN_IN), D ** -0.5),
        'ev_sink': nrm((N_EVEN, A_HEADS), 0.5),
        'ev_sgu_norm_g': 1.0 + nrm((N_EVEN, B_WIDTH), 0.05),
        'ev_sgu_w': nrm((N_EVEN, B_GROUPS, B_CHUNK, B_CHUNK), B_CHUNK ** -0.5),
        'ev_sgu_b': nrm((N_EVEN, B_GROUPS, B_CHUNK), 0.1),
        'ev_w_out': nrm((N_EVEN, EVEN_OUT, D), EVEN_OUT ** -0.5),
        'od_w_in': nrm((N_ODD, D, ODD_IN), D ** -0.5),
        'od_q_norm_g': 1.0 + nrm((N_ODD, C_Q_LORA), 0.05),
        'od_w_uq': nrm((N_ODD, C_Q_LORA, C_HEADS * (C_NOPE + C_ROPE)), C_Q_LORA ** -0.5),
        'od_kv_norm_g': 1.0 + nrm((N_ODD, C_KV_LORA), 0.05),
        'od_w_ukv': nrm((N_ODD, C_KV_LORA, C_HEADS * (C_NOPE + C_V)), C_KV_LORA ** -0.5),
        'od_conv_w': nrm((N_ODD, D_CONV, D_WIDTH), D_CONV ** -0.5),
        'od_w_out': nrm((N_ODD, ODD_OUT, D), ODD_OUT ** -0.5),
        'router_w': nrm((DEPTH, D, N_EXPERTS), D ** -0.5),
        'exp_w1': nrm((DEPTH, N_EXPERTS, D, EXPERT_FF), D ** -0.5),
        'exp_w3': nrm((DEPTH, N_EXPERTS, D, EXPERT_FF), D ** -0.5),
        'exp_w2': nrm((DEPTH, N_EXPERTS, EXPERT_FF, D), EXPERT_FF ** -0.5),
        'final_g': 1.0 + nrm((D,), 0.05),
    }


def reference(x, c, ctx, c_ctx, mod_w, mod_b, norm1_g, norm2_g, ev_w_in, ev_sink, ev_sgu_norm_g, ev_sgu_w, ev_sgu_b, ev_w_out, od_w_in, od_q_norm_g, od_w_uq, od_kv_norm_g, od_w_ukv, od_conv_w, od_w_out, router_w, exp_w1, exp_w3, exp_w2, final_g):
    rows = x.shape[1] // GRID_W
    cos_a, sin_a = axial_rope_tables(rows, A_HEAD_DIM)
    cos_c, sin_c = axial_rope_tables(rows, C_ROPE)
    silu_c = jax.nn.silu(c)
    silu_cc = jax.nn.silu(c_ctx)[None, :]
    x_lat, x_ctx = x, ctx
    for layer in range(DEPTH):
        need_ctx = layer < DEPTH - 1
        m_lat = jnp.split((silu_c @ mod_w[layer] + mod_b[layer])[:, None, :], 6, axis=-1)
        m_ctx = jnp.split((silu_cc @ mod_w[layer] + mod_b[layer])[:, None, :], 6, axis=-1)
        h_lat = rms_norm(x_lat, norm1_g[layer]) * (1 + m_lat[1]) + m_lat[0]
        h_ctx = rms_norm(x_ctx, norm1_g[layer]) * (1 + m_ctx[1]) + m_ctx[0]
        i = layer // 2
        if layer % 2 == 0:
            y_lat, y_ctx = even_mixer(h_lat, h_ctx, ev_w_in[i], ev_sink[i], ev_sgu_norm_g[i], ev_sgu_w[i], ev_sgu_b[i], ev_w_out[i], cos_a, sin_a, need_ctx)
        else:
            y_lat, y_ctx = odd_mixer(h_lat, h_ctx, od_w_in[i], od_q_norm_g[i], od_w_uq[i], od_kv_norm_g[i], od_w_ukv[i], od_conv_w[i], od_w_out[i], cos_c, sin_c, need_ctx)
        x_lat = x_lat + m_lat[2] * y_lat
        h_lat = rms_norm(x_lat, norm2_g[layer]) * (1 + m_lat[4]) + m_lat[3]
        x_lat = x_lat + m_lat[5] * expert_choice_ffn(h_lat, router_w[layer], exp_w1[layer], exp_w3[layer], exp_w2[layer])
        if need_ctx:
            x_ctx = x_ctx + m_ctx[2] * y_ctx
            h_ctx = rms_norm(x_ctx, norm2_g[layer]) * (1 + m_ctx[4]) + m_ctx[3]
            x_ctx = x_ctx + m_ctx[5] * expert_choice_ffn(h_ctx, router_w[layer], exp_w1[layer], exp_w3[layer], exp_w2[layer])
    return rms_norm(x_lat, final_g)
```

```python
import functools

import jax
import jax.numpy as jnp
from jax import lax
from jax.experimental import pallas as pl
from jax.experimental.pallas import tpu as pltpu

F32 = jnp.float32
BF16 = jnp.bfloat16
HIGHEST = lax.Precision.HIGHEST

D_MODEL = 1024
DEPTH = 4
GRID_W = 64
NORM_EPS = 1e-6
ROPE_BASE = 10000.0
NEG_INF = -1e30

A_HEADS = 8
A_KV_HEADS = 2
A_HEAD_DIM = 64
WINDOW = 128
A_BLOCK = 128
B_WIDTH = 512
B_GROUPS = 4
B_CHUNK = 128
C_HEADS = 8
C_Q_LORA = 384
C_KV_LORA = 256
C_NOPE = 64
C_ROPE = 32
C_V = 64
D_WIDTH = 512
D_CONV = 3
N_EXPERTS = 16
EXPERT_FF = 1024
EC_FACTOR = 2

A_Q_W = A_HEADS * A_HEAD_DIM
A_KV_W = A_KV_HEADS * A_HEAD_DIM

LANES = 128
SUBLANES = 8
ROW_TILE = 256
MOD_ROWS = 8
HP_CHUNKS = D_MODEL // (2 * LANES)
ACC_CHUNKS = D_MODEL // LANES
VMEM_LIMIT = 56 * 1024 * 1024

EVEN_COLS = 2 * A_KV_W + 2 * A_KV_W + A_Q_W + 2 * B_WIDTH
ODD_COLS = C_KV_LORA + LANES + C_Q_LORA + 3 * D_WIDTH
HEAD_PAD = LANES


def _cparams(sem):
    return pltpu.CompilerParams(dimension_semantics=sem, vmem_limit_bytes=VMEM_LIMIT)


def _nt_dot(a, b, precision=None):
    return lax.dot_general(a, b, (((1,), (1,)), ((), ())), precision=precision,
                           preferred_element_type=F32)


def _tn_dot(a, b, precision=None):
    return lax.dot_general(a, b, (((0,), (0,)), ((), ())), precision=precision,
                           preferred_element_type=F32)


def _mod_kernel(cc_ref, w_ref, b_ref, o_ref):
    cc = cc_ref[...]
    s = cc * jax.nn.sigmoid(cc)
    o_ref[0] = jnp.dot(s, w_ref[0], precision=HIGHEST, preferred_element_type=F32) + b_ref[0]


def _modulation(cc, mod_w, mod_b):
    rows = cc.shape[0]
    ncol = mod_w.shape[2] // D_MODEL
    return pl.pallas_call(
        _mod_kernel,
        grid=(DEPTH, ncol),
        in_specs=[
            pl.BlockSpec((rows, D_MODEL), lambda l, j: (0, 0)),
            pl.BlockSpec((1, D_MODEL, D_MODEL), lambda l, j: (l, 0, j)),
            pl.BlockSpec((1, 1, D_MODEL), lambda l, j: (l, 0, j)),
        ],
        out_specs=pl.BlockSpec((1, rows, D_MODEL), lambda l, j: (l, 0, j)),
        out_shape=jax.ShapeDtypeStruct((DEPTH, rows, mod_w.shape[2]), F32),
        compiler_params=_cparams(("arbitrary", "arbitrary")),
        name="adaln_modulation",
    )(cc, mod_w, mod_b.reshape(DEPTH, 1, -1))


def _residual_from_moe(x, moe_ref, gate_row, tm):
    cols = []
    for j in range(ACC_CHUNKS):
        chunk = moe_ref[0, pl.ds(j, tm, stride=ACC_CHUNKS), :]
        sl = slice(j * LANES, (j + 1) * LANES)
        cols.append(x[:, sl] + gate_row[:, sl] * chunk)
    return jnp.concatenate(cols, axis=1)


def _rms_mod(x, g, shift, scale):
    y = x * lax.rsqrt(jnp.mean(x * x, axis=-1, keepdims=True) + NORM_EPS) * g
    return y * (1.0 + scale) + shift


def _rms(x, g):
    return x * lax.rsqrt(jnp.mean(x * x, axis=-1, keepdims=True) + NORM_EPS) * g


def _rope_pairs(a, cos, sin, half, first):
    rot = jnp.where(first, pltpu.roll(a, LANES - half, 1), pltpu.roll(a, half, 1))
    return a * cos + rot * sin


def _in_even_kernel(has_moe, tm, *refs):
    if has_moe:
        (x_ref, moe_ref, modp_ref, mod_ref, g_ref, w_ref, cos_ref, sin_ref, lng_ref,
         xo_ref, q_ref, kd_ref, vd_ref, u_ref, vn_ref) = refs
    else:
        (x_ref, mod_ref, g_ref, w_ref, cos_ref, sin_ref, lng_ref,
         q_ref, kd_ref, vd_ref, u_ref, vn_ref) = refs
    x = x_ref[0]
    if has_moe:
        x = _residual_from_moe(x, moe_ref, modp_ref[0, 0, 5:6, :], tm)
        xo_ref[0] = x
    h = _rms_mod(x, g_ref[...], mod_ref[0, 0, 0:1, :], mod_ref[0, 0, 1:2, :])
    z = jnp.dot(h.astype(BF16), w_ref[...], preferred_element_type=F32)
    cos = cos_ref[...]
    sin = sin_ref[...]
    lane = lax.broadcasted_iota(jnp.int32, (tm, LANES), 1)
    first = (lane % A_HEAD_DIM) < (A_HEAD_DIM // 2)
    half = A_HEAD_DIM // 2
    for j in range(2):
        sl = slice(j * LANES, (j + 1) * LANES)
        kd_ref[0, :, sl] = _rope_pairs(z[:, sl], cos, sin, half, first).astype(BF16)
    vd_ref[0] = z[:, 2 * LANES:4 * LANES].astype(BF16)
    qoff = 4 * LANES
    scale = A_HEAD_DIM ** -0.5
    for j in range(A_Q_W // LANES):
        sl = slice(qoff + j * LANES, qoff + (j + 1) * LANES)
        q_ref[0, :, j * LANES:(j + 1) * LANES] = (
            _rope_pairs(z[:, sl], cos, sin, half, first) * scale).astype(BF16)
    uoff = qoff + A_Q_W
    u_ref[0] = jax.nn.gelu(z[:, uoff:uoff + B_WIDTH])
    gv = jax.nn.gelu(z[:, uoff + B_WIDTH:uoff + 2 * B_WIDTH])
    mu = jnp.mean(gv, axis=-1, keepdims=True)
    var = jnp.mean(jnp.square(gv - mu), axis=-1, keepdims=True)
    vn_ref[0] = ((gv - mu) * lax.rsqrt(var + NORM_EPS) * lng_ref[...]).astype(BF16)


def _tile_is_ctx(n_lat_tiles):
    return lambda i: jnp.where(i >= n_lat_tiles, 1, 0)


def _in_even(x, moe, modp, mod, g1, w_in, cos, sin, lng, n_lat):
    bsz, t, _ = x.shape
    tm = ROW_TILE
    nt = t // tm
    is_ctx = _tile_is_ctx(n_lat // tm)
    has_moe = moe is not None
    row = lambda w: pl.BlockSpec((1, tm, w), lambda b, i: (b, i, 0))
    mod_spec = pl.BlockSpec((1, 1, MOD_ROWS, D_MODEL), lambda b, i: (b, is_ctx(i), 0, 0))
    full2 = lambda a: pl.BlockSpec(a.shape, lambda b, i: (0, 0))
    in_specs = [row(D_MODEL)]
    args = [x]
    if has_moe:
        in_specs += [pl.BlockSpec((1, tm * ACC_CHUNKS, LANES), lambda b, i: (b, i, 0)), mod_spec]
        args += [moe, modp]
    in_specs += [mod_spec, full2(g1), full2(w_in),
                 pl.BlockSpec((tm, LANES), lambda b, i: (i, 0)),
                 pl.BlockSpec((tm, LANES), lambda b, i: (i, 0)), full2(lng)]
    args += [mod, g1, w_in, cos, sin, lng]
    out_specs = []
    out_shape = []
    if has_moe:
        out_specs.append(row(D_MODEL))
        out_shape.append(jax.ShapeDtypeStruct((bsz, t, D_MODEL), F32))
    out_specs += [row(A_Q_W), row(2 * LANES), row(2 * LANES), row(B_WIDTH), row(B_WIDTH)]
    out_shape += [jax.ShapeDtypeStruct((bsz, t, A_Q_W), BF16),
                  jax.ShapeDtypeStruct((bsz, t, 2 * LANES), BF16),
                  jax.ShapeDtypeStruct((bsz, t, 2 * LANES), BF16),
                  jax.ShapeDtypeStruct((bsz, t, B_WIDTH), F32),
                  jax.ShapeDtypeStruct((bsz, t, B_WIDTH), BF16)]
    outs = pl.pallas_call(
        functools.partial(_in_even_kernel, has_moe, tm),
        grid=(bsz, nt), in_specs=in_specs, out_specs=out_specs, out_shape=out_shape,
        compiler_params=_cparams(("parallel", "arbitrary")),
        name="in_proj_even",
    )(*args)
    if not has_moe:
        outs = [x] + list(outs)
    return outs


def _win_attn_kernel(n_lat_blocks, sink_ref, q_ref, kp_ref, kc_ref, kn_ref, kx_ref,
                     vp_ref, vc_ref, vn_ref, vx_ref, o_ref):
    i = pl.program_id(1)
    blk = A_BLOCK
    n_ctx = kx_ref.shape[1]
    is_lat = i < n_lat_blocks
    prev_ok = jnp.logical_and(is_lat, i >= 1)
    next_ok = jnp.logical_and(is_lat, i + 1 < n_lat_blocks)
    g = A_HEADS // A_KV_HEADS
    rows = g * blk
    r = lax.broadcasted_iota(jnp.int32, (rows, blk), 0) % blk
    c = lax.broadcasted_iota(jnp.int32, (rows, blk), 1)
    m_prev = jnp.logical_and(c >= r, prev_ok)
    m_cur = jnp.logical_and(c >= 0, is_lat)
    m_next = jnp.logical_and(c <= r, next_ok)
    lane = lax.broadcasted_iota(jnp.int32, (blk, LANES), 1)
    low = lane < A_HEAD_DIM
    zero = jnp.zeros((), BF16)
    for h in range(A_KV_HEADS):
        hs = slice(h * LANES, (h + 1) * LANES)
        qs = []
        for cidx in range(g // 2):
            qc = q_ref[0, :, (h * (g // 2) + cidx) * LANES:(h * (g // 2) + cidx + 1) * LANES]
            qs.append(jnp.where(low, qc, zero))
            qs.append(jnp.where(low, zero, qc))
        qh = jnp.concatenate(qs, axis=0)
        s_p = jnp.where(m_prev, _nt_dot(qh, kp_ref[0, :, hs]), NEG_INF)
        s_c = jnp.where(m_cur, _nt_dot(qh, kc_ref[0, :, hs]), NEG_INF)
        s_n = jnp.where(m_next, _nt_dot(qh, kn_ref[0, :, hs]), NEG_INF)
        s_x = _nt_dot(qh, kx_ref[0, :, hs])
        sink = jnp.concatenate(
            [jnp.full((blk, 1), sink_ref[h * g + j], F32) for j in range(g)], axis=0)
        m = jnp.maximum(
            jnp.maximum(jnp.max(s_p, axis=-1, keepdims=True), jnp.max(s_c, axis=-1, keepdims=True)),
            jnp.maximum(jnp.max(s_n, axis=-1, keepdims=True), jnp.max(s_x, axis=-1, keepdims=True)))
        m = jnp.maximum(m, sink)
        p_p = jnp.exp(s_p - m)
        p_c = jnp.exp(s_c - m)
        p_n = jnp.exp(s_n - m)
        p_x = jnp.exp(s_x - m)
        den = (jnp.sum(p_p, axis=-1, keepdims=True) + jnp.sum(p_c, axis=-1, keepdims=True)
               + jnp.sum(p_n, axis=-1, keepdims=True) + jnp.sum(p_x, axis=-1, keepdims=True)
               + jnp.exp(sink - m))
        inv = 1.0 / den
        dot = functools.partial(jnp.dot, preferred_element_type=F32)
        o = (dot((p_p * inv).astype(BF16), vp_ref[0, :, hs])
             + dot((p_c * inv).astype(BF16), vc_ref[0, :, hs])
             + dot((p_n * inv).astype(BF16), vn_ref[0, :, hs])
             + dot((p_x * inv).astype(BF16), vx_ref[0, :, hs]))
        for cidx in range(g // 2):
            o_even = o[(2 * cidx) * blk:(2 * cidx + 1) * blk]
            o_odd = o[(2 * cidx + 1) * blk:(2 * cidx + 2) * blk]
            col = (h * (g // 2) + cidx) * LANES
            o_ref[0, :, col:col + LANES] = jnp.where(low, o_even, o_odd).astype(BF16)


def _win_attn(sink, q, kd, vd, n_lat, n_ctx):
    bsz, t, _ = q.shape
    blk = A_BLOCK
    nb = t // blk
    nlb = n_lat // blk
    kvw = 2 * LANES
    prev = pl.BlockSpec((1, blk, kvw), lambda b, i: (b, jnp.maximum(i - 1, 0), 0))
    cur = pl.BlockSpec((1, blk, kvw), lambda b, i: (b, i, 0))
    nxt = pl.BlockSpec((1, blk, kvw), lambda b, i: (b, jnp.minimum(i + 1, nb - 1), 0))
    ctx = pl.BlockSpec((1, n_ctx, kvw), lambda b, i: (b, t // n_ctx - 1, 0))
    return pl.pallas_call(
        functools.partial(_win_attn_kernel, nlb),
        grid=(bsz, nb),
        in_specs=[pl.BlockSpec(memory_space=pltpu.SMEM),
                  pl.BlockSpec((1, blk, A_Q_W), lambda b, i: (b, i, 0)),
                  prev, cur, nxt, ctx, prev, cur, nxt, ctx],
        out_specs=pl.BlockSpec((1, blk, A_Q_W), lambda b, i: (b, i, 0)),
        out_shape=jax.ShapeDtypeStruct((bsz, t, A_Q_W), BF16),
        compiler_params=_cparams(("parallel", "arbitrary")),
        name="window_attention",
    )(sink, q, kd, kd, kd, kd, vd, vd, vd, vd)


def _out_epilogue(tm, y, x_ref, mod_ref, g2_ref, rw_ref, xo_ref, hp_ref, aff_ref):
    x = x_ref[0] + mod_ref[0, 0, 2:3, :] * y
    xo_ref[0] = x
    h2 = _rms_mod(x, g2_ref[...], mod_ref[0, 0, 3:4, :], mod_ref[0, 0, 4:5, :])
    bits = lax.bitcast_convert_type(h2.astype(BF16).astype(F32), jnp.uint32)
    half = D_MODEL // 2
    packed = (bits[:, :half] >> 16) | (bits[:, half:] & jnp.uint32(0xFFFF0000))
    for j in range(HP_CHUNKS):
        hp_ref[0, pl.ds(j, tm, stride=HP_CHUNKS), :] = packed[:, j * LANES:(j + 1) * LANES]
    logits = _nt_dot(rw_ref[...], h2, precision=HIGHEST)
    mx = jnp.max(logits, axis=0, keepdims=True)
    ex = jnp.exp(logits - mx)
    aff_ref[0] = ex / jnp.sum(ex, axis=0, keepdims=True)


def _out_even_kernel(tm, o_ref, u_ref, vn_ref, ws_ref, bs_ref, wo_ref, x_ref, mod_ref, g2_ref,
                     rw_ref, xo_ref, hp_ref, aff_ref):
    dot = functools.partial(jnp.dot, preferred_element_type=F32)
    gw = B_WIDTH // B_GROUPS
    rows = []
    for cidx in range(tm // B_CHUNK):
        rs = slice(cidx * B_CHUNK, (cidx + 1) * B_CHUNK)
        cols = []
        for g in range(B_GROUPS):
            cs = slice(g * gw, (g + 1) * gw)
            mixed = dot(ws_ref[g], vn_ref[0, rs, cs]) + bs_ref[:, cs]
            cols.append(u_ref[0, rs, cs] * mixed)
        rows.append(jnp.concatenate(cols, axis=1))
    s = jnp.concatenate(rows, axis=0)
    y = dot(o_ref[0], wo_ref[:A_Q_W, :]) + dot(s.astype(BF16), wo_ref[A_Q_W:, :])
    _out_epilogue(tm, y, x_ref, mod_ref, g2_ref, rw_ref, xo_ref, hp_ref, aff_ref)


def _out_specs_common(bsz, t, tm):
    row = lambda w: pl.BlockSpec((1, tm, w), lambda b, i: (b, i, 0))
    out_specs = [row(D_MODEL),
                 pl.BlockSpec((1, tm * HP_CHUNKS, LANES), lambda b, i: (b, i, 0)),
                 pl.BlockSpec((1, N_EXPERTS, tm), lambda b, i: (b, 0, i))]
    out_shape = [jax.ShapeDtypeStruct((bsz, t, D_MODEL), F32),
                 jax.ShapeDtypeStruct((bsz, t * HP_CHUNKS, LANES), jnp.uint32),
                 jax.ShapeDtypeStruct((bsz, N_EXPERTS, t), F32)]
    return out_specs, out_shape


def _out_even(o, u, vn, ws, bs, wo, x, mod, g2, rw_t, n_lat):
    bsz, t, _ = x.shape
    tm = ROW_TILE
    is_ctx = _tile_is_ctx(n_lat // tm)
    row = lambda w: pl.BlockSpec((1, tm, w), lambda b, i: (b, i, 0))
    full = lambda a: pl.BlockSpec(a.shape, lambda b, i: (0,) * a.ndim)
    mod_spec = pl.BlockSpec((1, 1, MOD_ROWS, D_MODEL), lambda b, i: (b, is_ctx(i), 0, 0))
    out_specs, out_shape = _out_specs_common(bsz, t, tm)
    return pl.pallas_call(
        functools.partial(_out_even_kernel, tm),
        grid=(bsz, t // tm),
        in_specs=[row(A_Q_W), row(B_WIDTH), row(B_WIDTH), full(ws), full(bs), full(wo),
                  row(D_MODEL), mod_spec, full(g2), full(rw_t)],
        out_specs=out_specs, out_shape=out_shape,
        compiler_params=_cparams(("parallel", "arbitrary")),
        name="out_proj_even",
    )(o, u, vn, ws, bs, wo, x, mod, g2, rw_t)


def _in_odd_kernel(has_moe, tm, *refs):
    (x_ref, moe_ref, modp_ref, mod_ref, g_ref, w_ref, cos_ref, sin_ref, qg_ref, kvg_ref,
     wuq_ref, wukv_ref, xo_ref, q_ref, k_ref, v_ref, gb_ref, u_ref) = refs
    x = _residual_from_moe(x_ref[0], moe_ref, modp_ref[0, 0, 5:6, :], tm)
    xo_ref[0] = x
    h = _rms_mod(x, g_ref[...], mod_ref[0, 0, 0:1, :], mod_ref[0, 0, 1:2, :])
    z = jnp.dot(h.astype(BF16), w_ref[...], preferred_element_type=F32)
    cos = cos_ref[...]
    sin = sin_ref[...]
    lane = lax.broadcasted_iota(jnp.int32, (tm, LANES), 1)
    half = C_ROPE // 2
    first = lane < C_NOPE + half
    ckv = _rms(z[:, :C_KV_LORA], kvg_ref[...])
    kr = _rope_pairs(z[:, C_KV_LORA:C_KV_LORA + LANES], cos, sin, half, first)
    cqo = C_KV_LORA + LANES
    cq = _rms(z[:, cqo:cqo + C_Q_LORA], qg_ref[...])
    kv = jnp.dot(ckv.astype(BF16), wukv_ref[...], preferred_element_type=F32)
    qq = jnp.dot(cq.astype(BF16), wuq_ref[...], preferred_element_type=F32)
    scale = (C_NOPE + C_ROPE) ** -0.5
    ones_col = jnp.where(lane == C_V, 1.0, 0.0)
    voff = C_HEADS * HEAD_PAD
    for hd in range(C_HEADS):
        sl = slice(hd * HEAD_PAD, (hd + 1) * HEAD_PAD)
        q_ref[0, :, sl] = (_rope_pairs(qq[:, sl], cos, sin, half, first) * scale).astype(BF16)
        k_ref[0, :, sl] = (kv[:, sl] + kr).astype(BF16)
        v_ref[0, :, sl] = (kv[:, voff + hd * HEAD_PAD:voff + (hd + 1) * HEAD_PAD]
                           + ones_col).astype(BF16)
    co = cqo + C_Q_LORA
    gb_ref[0] = z[:, co:co + D_WIDTH]
    u_ref[0] = z[:, co + D_WIDTH:co + 2 * D_WIDTH] * z[:, co + 2 * D_WIDTH:co + 3 * D_WIDTH]


def _in_odd(x, moe, modp, mod, g1, w_in, cos, sin, qg, kvg, wuq, wukv, n_lat):
    bsz, t, _ = x.shape
    tm = ROW_TILE
    is_ctx = _tile_is_ctx(n_lat // tm)
    row = lambda w: pl.BlockSpec((1, tm, w), lambda b, i: (b, i, 0))
    mod_spec = pl.BlockSpec((1, 1, MOD_ROWS, D_MODEL), lambda b, i: (b, is_ctx(i), 0, 0))
    full2 = lambda a: pl.BlockSpec(a.shape, lambda b, i: (0, 0))
    hw = C_HEADS * HEAD_PAD
    return pl.pallas_call(
        functools.partial(_in_odd_kernel, True, tm),
        grid=(bsz, t // tm),
        in_specs=[row(D_MODEL),
                  pl.BlockSpec((1, tm * ACC_CHUNKS, LANES), lambda b, i: (b, i, 0)),
                  mod_spec, mod_spec, full2(g1), full2(w_in),
                  pl.BlockSpec((tm, LANES), lambda b, i: (i, 0)),
                  pl.BlockSpec((tm, LANES), lambda b, i: (i, 0)),
                  full2(qg), full2(kvg), full2(wuq), full2(wukv)],
        out_specs=[row(D_MODEL), row(hw), row(hw), row(hw), row(D_WIDTH), row(D_WIDTH)],
        out_shape=[jax.ShapeDtypeStruct((bsz, t, D_MODEL), F32),
                   jax.ShapeDtypeStruct((bsz, t, hw), BF16),
                   jax.ShapeDtypeStruct((bsz, t, hw), BF16),
                   jax.ShapeDtypeStruct((bsz, t, hw), BF16),
                   jax.ShapeDtypeStruct((bsz, t, D_WIDTH), F32),
                   jax.ShapeDtypeStruct((bsz, t, D_WIDTH), F32)],
        compiler_params=_cparams(("parallel", "arbitrary")),
        name="in_proj_odd",
    )(x, moe, modp, mod, g1, w_in, cos, sin, qg, kvg, wuq, wukv)


def _dense_attn_kernel(tk, q_ref, k_ref, v_ref, o_ref):
    q = q_ref[0]
    tq = q.shape[0]
    nk = k_ref.shape[1] // tk

    def body(j, carry):
        m, acc = carry
        ks = pl.multiple_of(j * tk, tk)
        s = _nt_dot(q, k_ref[0, pl.ds(ks, tk), :])
        m_new = jnp.maximum(m, jnp.max(s, axis=-1, keepdims=True))
        alpha = jnp.exp(m - m_new)
        p = jnp.exp(s - m_new)
        acc = alpha * acc + jnp.dot(p.astype(BF16), v_ref[0, pl.ds(ks, tk), :],
                                    preferred_element_type=F32)
        return m_new, acc

    m0 = jnp.full((tq, 1), NEG_INF, F32)
    acc0 = jnp.zeros((tq, HEAD_PAD), F32)
    _, acc = lax.fori_loop(0, nk, body, (m0, acc0))
    o_ref[0] = (acc / acc[:, C_V:C_V + 1]).astype(BF16)


def _dense_attn(q, k, v, q_start, q_len, k_start, k_len, tq):
    bsz, _, hw = q.shape
    tk = ROW_TILE
    q0 = q_start // tq
    kb = k_start // k_len
    return pl.pallas_call(
        functools.partial(_dense_attn_kernel, tk),
        grid=(bsz, C_HEADS, q_len // tq),
        in_specs=[pl.BlockSpec((1, tq, HEAD_PAD), lambda b, h, i: (b, q0 + i, h)),
                  pl.BlockSpec((1, k_len, HEAD_PAD), lambda b, h, i: (b, kb, h)),
                  pl.BlockSpec((1, k_len, HEAD_PAD), lambda b, h, i: (b, kb, h))],
        out_specs=pl.BlockSpec((1, tq, HEAD_PAD), lambda b, h, i: (b, i, h)),
        out_shape=jax.ShapeDtypeStruct((bsz, q_len, hw), BF16),
        compiler_params=_cparams(("parallel", "parallel", "arbitrary")),
        name="dense_attention",
    )(q, k, v)


def _out_odd_kernel(tm, n_lat_tiles, n_tiles, ol_ref, oc_ref, gb_ref, u_ref, up_ref, un_ref, cw_ref,
                    wo_ref, x_ref, mod_ref, g2_ref, rw_ref, xo_ref, hp_ref, aff_ref):
    i = pl.program_id(1)
    o_att = jnp.where(i < n_lat_tiles, ol_ref[0], oc_ref[0])
    dot = functools.partial(jnp.dot, preferred_element_type=F32)
    u = u_ref[0]
    has_prev = jnp.logical_and(i != 0, i != n_lat_tiles)
    has_next = jnp.logical_and(i != n_lat_tiles - 1, i != n_tiles - 1)
    prev_row = jnp.where(has_prev, up_ref[0, SUBLANES - 1:SUBLANES, :], 0.0)
    next_row = jnp.where(has_next, un_ref[0, 0:1, :], 0.0)
    ridx = lax.broadcasted_iota(jnp.int32, u.shape, 0)
    u_m1 = jnp.where(ridx == 0, prev_row, pltpu.roll(u, 1, 0))
    u_p1 = jnp.where(ridx == tm - 1, next_row, pltpu.roll(u, tm - 1, 0))
    conv = u_m1 * cw_ref[0:1, :] + u * cw_ref[1:2, :] + u_p1 * cw_ref[2:3, :]
    c = gb_ref[0] * conv
    hw = C_HEADS * HEAD_PAD
    y = dot(o_att, wo_ref[:hw, :]) + dot(c.astype(BF16), wo_ref[hw:, :])
    _out_epilogue(tm, y, x_ref, mod_ref, g2_ref, rw_ref, xo_ref, hp_ref, aff_ref)


def _out_odd(o_lat, o_ctx, gb, u, cw, wo, x, mod, g2, rw_t, n_lat):
    bsz, t, _ = x.shape
    tm = ROW_TILE
    nt = t // tm
    nlt = n_lat // tm
    is_ctx = _tile_is_ctx(nlt)
    row = lambda w: pl.BlockSpec((1, tm, w), lambda b, i: (b, i, 0))
    full = lambda a: pl.BlockSpec(a.shape, lambda b, i: (0,) * a.ndim)
    mod_spec = pl.BlockSpec((1, 1, MOD_ROWS, D_MODEL), lambda b, i: (b, is_ctx(i), 0, 0))
    hw = C_HEADS * HEAD_PAD
    per = tm // SUBLANES
    last = t // SUBLANES - 1
    halo_prev = pl.BlockSpec((1, SUBLANES, D_WIDTH), lambda b, i: (b, jnp.maximum(i * per - 1, 0), 0))
    halo_next = pl.BlockSpec((1, SUBLANES, D_WIDTH), lambda b, i: (b, jnp.minimum((i + 1) * per, last), 0))
    out_specs, out_shape = _out_specs_common(bsz, t, tm)
    return pl.pallas_call(
        functools.partial(_out_odd_kernel, tm, nlt, nt),
        grid=(bsz, nt),
        in_specs=[pl.BlockSpec((1, tm, hw), lambda b, i: (b, jnp.minimum(i, nlt - 1), 0)),
                  pl.BlockSpec((1, tm, hw), lambda b, i: (b, jnp.maximum(i - nlt, 0), 0)),
                  row(D_WIDTH), row(D_WIDTH), halo_prev, halo_next,
                  full(cw), full(wo), row(D_MODEL), mod_spec, full(g2), full(rw_t)],
        out_specs=out_specs, out_shape=out_shape,
        compiler_params=_cparams(("parallel", "arbitrary")),
        name="out_proj_odd",
    )(o_lat, o_ctx, gb, u, u, u, cw, wo, x, mod, g2, rw_t)


def _route_kernel(cap, nblk, width, a_ref, idx_ref, gate_ref):
    n_e = N_EXPERTS
    a = a_ref[0]
    bits = lax.bitcast_convert_type(a, jnp.int32)

    def count(mask):
        c = jnp.sum(jnp.where(mask, 1.0, 0.0), axis=2, keepdims=True)
        return jnp.sum(c, axis=1, keepdims=True)

    def search(it, thr):
        cand = thr | jnp.left_shift(jnp.int32(1), 30 - it)
        return jnp.where(count(bits >= cand) >= cap, cand, thr)

    thr = lax.fori_loop(0, 31, search, jnp.zeros((n_e, 1, 1), jnp.int32))
    gt = bits > thr
    eq = bits == thr
    need = cap - count(gt)

    rows = n_e * nblk
    li = lax.broadcasted_iota(jnp.int32, (width, width), 0)
    lj = lax.broadcasted_iota(jnp.int32, (width, width), 1)
    upper = jnp.where(li <= lj, 1.0, 0.0).astype(BF16)
    ones_sq = jnp.ones((width, width), BF16)
    if nblk > 1:
        ri = lax.broadcasted_iota(jnp.int32, (rows, rows), 0)
        rj = lax.broadcasted_iota(jnp.int32, (rows, rows), 1)
        lower = jnp.where(jnp.logical_and(ri // nblk == rj // nblk, rj < ri), 1.0, 0.0).astype(BF16)

    def prefix(mask3):
        m2 = jnp.where(mask3, 1.0, 0.0).astype(BF16).reshape(rows, width)
        local = jnp.dot(m2, upper, preferred_element_type=F32)
        total = jnp.dot(m2, ones_sq, preferred_element_type=F32)
        if nblk > 1:
            excl = jnp.dot(lower, total.astype(BF16), preferred_element_type=F32)
        else:
            excl = jnp.zeros_like(total)
        return local, total, excl

    l_eq, _, x_eq = prefix(eq)
    tie_rank = (l_eq + x_eq).reshape(n_e, nblk, width)
    sel = jnp.logical_or(gt, jnp.logical_and(eq, tie_rank <= need))
    local, total, excl = prefix(sel)
    pin = local + excl

    slot_l = lax.broadcasted_iota(jnp.int32, (nblk, cap), 1).astype(F32)
    blk_s = lax.broadcasted_iota(jnp.int32, (nblk, cap), 0).astype(F32)
    slot_s = lax.broadcasted_iota(jnp.int32, (cap, width), 0).astype(F32)
    lane_w = lax.broadcasted_iota(jnp.int32, (cap, width), 1).astype(F32)
    ones_r = jnp.ones((SUBLANES, width), BF16)
    ones_rf = jnp.ones((SUBLANES, width), F32)
    reps = cap // width if cap >= width else 1
    for e in range(n_e):
        rs = slice(e * nblk, (e + 1) * nblk)
        pin_e = pin[rs]
        aff_e = a[e]
        if nblk > 1:
            lo = jnp.concatenate([excl[rs]] * reps, axis=1)[:, :cap]
            hi = lo + jnp.concatenate([total[rs]] * reps, axis=1)[:, :cap]
            oh_t = jnp.where(jnp.logical_and(lo <= slot_l, slot_l < hi), 1.0, 0.0)
            pin_g = _tn_dot(oh_t, pin_e, precision=HIGHEST)
            aff_g = _tn_dot(oh_t, aff_e, precision=HIGHEST)
            blk_row = jnp.sum(oh_t * blk_s, axis=0, keepdims=True)
        else:
            pin_g = jnp.broadcast_to(pin_e, (cap, width))
            aff_g = jnp.broadcast_to(aff_e, (cap, width))
            blk_row = jnp.zeros((1, cap), F32)
        ind = jnp.where(pin_g <= slot_s, 1.0, 0.0).astype(BF16)
        cnt_row = _nt_dot(ones_r, ind)[0:1]
        cnt_b = jnp.dot(ind, ones_sq, preferred_element_type=F32)
        picked = jnp.where(lane_w == cnt_b, aff_g, 0.0)
        gate_row = _nt_dot(ones_rf, picked, precision=HIGHEST)[0:1]
        idx_ref[0, e] = (blk_row * width + cnt_row).astype(jnp.int32)
        gate_ref[0, e] = gate_row


def _route(aff, cap):
    bsz, n_e, n = aff.shape
    width = LANES if n % (LANES * SUBLANES) == 0 else n
    nblk = n // width
    a4 = aff.reshape(bsz, n_e, nblk, width)
    idx, gate = pl.pallas_call(
        functools.partial(_route_kernel, cap, nblk, width),
        grid=(bsz,),
        in_specs=[pl.BlockSpec((1, n_e, nblk, width), lambda b: (b, 0, 0, 0))],
        out_specs=[pl.BlockSpec((1, n_e, 1, cap), lambda b: (b, 0, 0, 0)),
                   pl.BlockSpec((1, n_e, 1, cap), lambda b: (b, 0, 0, 0))],
        out_shape=[jax.ShapeDtypeStruct((bsz, n_e, 1, cap), jnp.int32),
                   jax.ShapeDtypeStruct((bsz, n_e, 1, cap), F32)],
        compiler_params=_cparams(("parallel",)),
        name="expert_choice_routing",
    )(a4)
    return idx.reshape(bsz, n_e, cap), gate.reshape(bsz, n_e, cap)


def _ffn_copies(hp_hbm, out_hbm, hsrc, acc, sem_in, sem_out, grp, nsub, n_per, row0):
    ins = []
    outs = []
    for sb in range(nsub):
        b = grp * nsub + sb
        ins.append(pltpu.make_async_copy(
            hp_hbm.at[b, pl.ds(row0 * HP_CHUNKS, n_per * HP_CHUNKS), :],
            hsrc.at[pl.ds(sb * n_per * HP_CHUNKS, n_per * HP_CHUNKS), :], sem_in.at[sb]))
        outs.append(pltpu.make_async_copy(
            acc.at[pl.ds(sb * n_per * ACC_CHUNKS, n_per * ACC_CHUNKS), :],
            out_hbm.at[b, pl.ds(row0 * ACC_CHUNKS, n_per * ACC_CHUNKS), :], sem_out.at[sb]))
    return ins, outs


def _ffn_kernel(m_slots, nsub, n_per, row0, unroll, idx_ref, gate_ref, hp_hbm, w1_ref, w3_ref,
                w2_ref, _moe_in, out_hbm, hsrc, acc, xt, yt, sem_in, sem_out):
    grp = pl.program_id(0)
    e = pl.program_id(1)
    stride = m_slots + SUBLANES
    ins, outs = _ffn_copies(hp_hbm, out_hbm, hsrc, acc, sem_in, sem_out, grp, nsub, n_per, row0)

    @pl.when(e == 0)
    def _():
        for cp in ins:
            cp.start()
        acc[...] = jnp.zeros(acc.shape, F32)
        for cp in ins:
            cp.wait()

    def gather(c, carry):
        for k in range(unroll):
            mi = c * unroll + k
            src = pl.multiple_of(idx_ref[0, 0, 0, mi] * HP_CHUNKS, HP_CHUNKS)
            xt[pl.ds(mi, HP_CHUNKS, stride=stride), :] = hsrc[pl.ds(src, HP_CHUNKS), :]
        return carry

    lax.fori_loop(0, m_slots // unroll, gather, 0)

    lo = []
    hi = []
    for j in range(HP_CHUNKS):
        w = xt[pl.ds(j * stride, m_slots), :]
        lo.append(lax.bitcast_convert_type(w << 16, F32))
        hi.append(lax.bitcast_convert_type(w & jnp.uint32(0xFFFF0000), F32))
    xs = jnp.concatenate(lo + hi, axis=1).astype(BF16)
    dot = functools.partial(jnp.dot, preferred_element_type=F32)
    a = dot(xs, w1_ref[0])
    b = dot(xs, w3_ref[0])
    act = (a * jax.nn.sigmoid(a) * b).astype(BF16)
    y = dot(act, w2_ref[0])
    for j in range(ACC_CHUNKS):
        yt[pl.ds(j * stride, m_slots), :] = y[:, j * LANES:(j + 1) * LANES]

    def scatter(c, carry):
        dst = []
        val = []
        for k in range(unroll):
            mi = c * unroll + k
            d = pl.multiple_of(idx_ref[0, 0, 0, mi] * ACC_CHUNKS, ACC_CHUNKS)
            slab = yt[pl.ds(mi, ACC_CHUNKS, stride=stride), :] * gate_ref[0, 0, 0, mi]
            dst.append(d)
            val.append(acc[pl.ds(d, ACC_CHUNKS), :] + slab)
        for d, v in zip(dst, val):
            acc[pl.ds(d, ACC_CHUNKS), :] = v
        return carry

    lax.fori_loop(0, m_slots // unroll, scatter, 0)

    @pl.when(e == N_EXPERTS - 1)
    def _():
        for cp in outs:
            cp.start()
        for cp in outs:
            cp.wait()


def _expert_ffn(idx, gate, hp, w1, w3, w2, moe_buf, nsub, n_per, row0):
    groups, n_e, m_slots = idx.shape
    n_tok = nsub * n_per
    stride = m_slots + SUBLANES
    unroll = 4
    smem = lambda: pl.BlockSpec((1, 1, 1, m_slots), lambda g, e: (g, e, 0, 0), memory_space=pltpu.SMEM)
    wspec = lambda: pl.BlockSpec((1, D_MODEL, EXPERT_FF), lambda g, e: (e, 0, 0))
    return pl.pallas_call(
        functools.partial(_ffn_kernel, m_slots, nsub, n_per, row0, unroll),
        grid=(groups, n_e),
        in_specs=[smem(), smem(), pl.BlockSpec(memory_space=pl.ANY), wspec(), wspec(),
                  pl.BlockSpec((1, EXPERT_FF, D_MODEL), lambda g, e: (e, 0, 0)),
                  pl.BlockSpec(memory_space=pl.ANY)],
        out_specs=pl.BlockSpec(memory_space=pl.ANY),
        out_shape=jax.ShapeDtypeStruct(moe_buf.shape, F32),
        scratch_shapes=[pltpu.VMEM((n_tok * HP_CHUNKS, LANES), jnp.uint32),
                        pltpu.VMEM((n_tok * ACC_CHUNKS, LANES), F32),
                        pltpu.VMEM((HP_CHUNKS * stride, LANES), jnp.uint32),
                        pltpu.VMEM((ACC_CHUNKS * stride, LANES), F32),
                        pltpu.SemaphoreType.DMA((nsub,)),
                        pltpu.SemaphoreType.DMA((nsub,))],
        input_output_aliases={6: 0},
        compiler_params=_cparams(("arbitrary", "arbitrary")),
        name="expert_ffn",
    )(idx.reshape(groups, n_e, 1, m_slots), gate.reshape(groups, n_e, 1, m_slots), hp, w1, w3, w2,
      moe_buf)


def _final_kernel(tm, x_ref, moe_ref, modp_ref, g_ref, o_ref):
    x = _residual_from_moe(x_ref[0], moe_ref, modp_ref[0, 0, 5:6, :], tm)
    o_ref[0] = _rms(x, g_ref[...])


def _final(x, moe, modp, g, n_lat):
    bsz = x.shape[0]
    tm = ROW_TILE
    return pl.pallas_call(
        functools.partial(_final_kernel, tm),
        grid=(bsz, n_lat // tm),
        in_specs=[pl.BlockSpec((1, tm, D_MODEL), lambda b, i: (b, i, 0)),
                  pl.BlockSpec((1, tm * ACC_CHUNKS, LANES), lambda b, i: (b, i, 0)),
                  pl.BlockSpec((1, 1, MOD_ROWS, D_MODEL), lambda b, i: (b, 0, 0, 0)),
                  pl.BlockSpec(g.shape, lambda b, i: (0, 0))],
        out_specs=pl.BlockSpec((1, tm, D_MODEL), lambda b, i: (b, i, 0)),
        out_shape=jax.ShapeDtypeStruct((bsz, n_lat, D_MODEL), F32),
        compiler_params=_cparams(("parallel", "arbitrary")),
        name="final_norm",
    )(x, moe, modp, g)


def _rope_angles(rows, rot_dim):
    row = jnp.repeat(jnp.arange(rows, dtype=F32), GRID_W)
    col = jnp.tile(jnp.arange(GRID_W, dtype=F32), rows)
    n_freq = rot_dim // 4
    inv_freq = ROPE_BASE ** (-jnp.arange(n_freq, dtype=F32) / n_freq)
    ang = jnp.concatenate([row[:, None] * inv_freq[None, :], col[:, None] * inv_freq[None, :]], axis=-1)
    return jnp.cos(ang), jnp.sin(ang)


def _rope_tables_even(n_lat, n_ctx):
    cos, sin = _rope_angles(n_lat // GRID_W, A_HEAD_DIM)
    cos_h = jnp.concatenate([cos, cos], axis=1)
    sin_h = jnp.concatenate([-sin, sin], axis=1)
    reps = LANES // A_HEAD_DIM
    cos_l = jnp.tile(cos_h, (1, reps))
    sin_l = jnp.tile(sin_h, (1, reps))
    return (jnp.concatenate([cos_l, jnp.ones((n_ctx, LANES), F32)], axis=0),
            jnp.concatenate([sin_l, jnp.zeros((n_ctx, LANES), F32)], axis=0))


def _rope_tables_odd(n_lat, n_ctx):
    cos, sin = _rope_angles(n_lat // GRID_W, C_ROPE)
    n = cos.shape[0]
    pad = LANES - C_NOPE - C_ROPE
    cos_l = jnp.concatenate([jnp.ones((n, C_NOPE), F32), cos, cos, jnp.ones((n, pad), F32)], axis=1)
    sin_l = jnp.concatenate([jnp.zeros((n, C_NOPE), F32), -sin, sin, jnp.zeros((n, pad), F32)], axis=1)
    return (jnp.concatenate([cos_l, jnp.ones((n_ctx, LANES), F32)], axis=0),
            jnp.concatenate([sin_l, jnp.zeros((n_ctx, LANES), F32)], axis=0))


def _even_w_in(w):
    k = w[:, :A_KV_W]
    v = w[:, A_KV_W:2 * A_KV_W]
    dup = lambda a: jnp.concatenate(
        [a[:, h * A_HEAD_DIM:(h + 1) * A_HEAD_DIM] for h in range(A_KV_HEADS) for _ in range(2)], axis=1)
    return jnp.concatenate([dup(k), dup(v), w[:, 2 * A_KV_W:]], axis=1).astype(BF16)


def _odd_w_in(w):
    d = w.shape[0]
    kr = w[:, C_KV_LORA:C_KV_LORA + C_ROPE]
    kr_group = jnp.concatenate(
        [jnp.zeros((d, C_NOPE), F32), kr, jnp.zeros((d, LANES - C_NOPE - C_ROPE), F32)], axis=1)
    return jnp.concatenate([w[:, :C_KV_LORA], kr_group, w[:, C_KV_LORA + C_ROPE:]], axis=1).astype(BF16)


def _odd_w_uq(w):
    r = w.shape[0]
    w3 = w.reshape(r, C_HEADS, C_NOPE + C_ROPE)
    pad = jnp.zeros((r, C_HEADS, HEAD_PAD - C_NOPE - C_ROPE), F32)
    return jnp.concatenate([w3, pad], axis=2).reshape(r, C_HEADS * HEAD_PAD).astype(BF16)


def _odd_w_ukv(w):
    r = w.shape[0]
    w3 = w.reshape(r, C_HEADS, C_NOPE + C_V)
    kpad = jnp.zeros((r, C_HEADS, HEAD_PAD - C_NOPE), F32)
    vpad = jnp.zeros((r, C_HEADS, HEAD_PAD - C_V), F32)
    kpart = jnp.concatenate([w3[:, :, :C_NOPE], kpad], axis=2).reshape(r, C_HEADS * HEAD_PAD)
    vpart = jnp.concatenate([w3[:, :, C_NOPE:], vpad], axis=2).reshape(r, C_HEADS * HEAD_PAD)
    return jnp.concatenate([kpart, vpart], axis=1).astype(BF16)


def _odd_w_out(w):
    d = w.shape[1]
    att = w[:C_HEADS * C_V].reshape(C_HEADS, C_V, d)
    att = jnp.concatenate([att, jnp.zeros((C_HEADS, HEAD_PAD - C_V, d), F32)], axis=1)
    return jnp.concatenate([att.reshape(C_HEADS * HEAD_PAD, d), w[C_HEADS * C_V:]], axis=0).astype(BF16)


def _mod_table(mods_l, bsz):
    lat = mods_l[:bsz].reshape(bsz, 1, 6, D_MODEL)
    ctx = jnp.broadcast_to(mods_l[bsz].reshape(1, 1, 6, D_MODEL), (bsz, 1, 6, D_MODEL))
    tab = jnp.concatenate([lat, ctx], axis=1)
    return jnp.pad(tab, ((0, 0), (0, 0), (0, MOD_ROWS - 6), (0, 0)))


def kernel(x, c, ctx, c_ctx, mod_w, mod_b, norm1_g, norm2_g, ev_w_in, ev_sink, ev_sgu_norm_g, ev_sgu_w, ev_sgu_b, ev_w_out, od_w_in, od_q_norm_g, od_w_uq, od_kv_norm_g, od_w_ukv, od_conv_w, od_w_out, router_w, exp_w1, exp_w3, exp_w2, final_g):
    bsz, n_lat, _ = x.shape
    n_ctx = ctx.shape[1]
    t = n_lat + n_ctx
    cap_lat = EC_FACTOR * n_lat // N_EXPERTS
    cap_ctx = EC_FACTOR * n_ctx // N_EXPERTS

    mod_rows = -(-(bsz + 1) // SUBLANES) * SUBLANES
    cc = jnp.concatenate([c, c_ctx[None, :], jnp.zeros((mod_rows - bsz - 1, D_MODEL), F32)], axis=0)
    mods = _modulation(cc, mod_w, mod_b)
    tabs = [_mod_table(mods[l], bsz) for l in range(DEPTH)]

    cos_e, sin_e = _rope_tables_even(n_lat, n_ctx)
    cos_o, sin_o = _rope_tables_odd(n_lat, n_ctx)

    xs = jnp.concatenate([x, ctx], axis=1)
    moe = None
    for layer in range(DEPTH):
        i = layer // 2
        need_ctx = layer < DEPTH - 1
        g1 = norm1_g[layer][None, :]
        g2 = norm2_g[layer][None, :]
        rw_t = router_w[layer].T
        modp = tabs[layer - 1] if layer > 0 else None
        if layer % 2 == 0:
            xs, q, kd, vd, u, vn = _in_even(xs, moe, modp, tabs[layer], g1, _even_w_in(ev_w_in[i]),
                                            cos_e, sin_e, ev_sgu_norm_g[i][None, :], n_lat)
            o = _win_attn(ev_sink[i], q, kd, vd, n_lat, n_ctx)
            bs = jnp.repeat(ev_sgu_b[i].T, B_WIDTH // B_GROUPS, axis=1)
            xs, hp, aff = _out_even(o, u, vn, ev_sgu_w[i].astype(BF16), bs, ev_w_out[i].astype(BF16),
                                    xs, tabs[layer], g2, rw_t, n_lat)
        else:
            xs, q, k, v, gb, u = _in_odd(xs, moe, modp, tabs[layer], g1, _odd_w_in(od_w_in[i]),
                                         cos_o, sin_o, od_q_norm_g[i][None, :], od_kv_norm_g[i][None, :],
                                         _odd_w_uq(od_w_uq[i]), _odd_w_ukv(od_w_ukv[i]), n_lat)
            o_lat = _dense_attn(q, k, v, 0, n_lat, 0, t, 2 * ROW_TILE)
            o_ctx = _dense_attn(q, k, v, n_lat, n_ctx, n_lat, n_ctx, n_ctx)
            xs, hp, aff = _out_odd(o_lat, o_ctx, gb, u, od_conv_w[i], _odd_w_out(od_w_out[i]), xs, tabs[layer],
                                   g2, rw_t, n_lat)
        w1 = exp_w1[layer].astype(BF16)
        w3 = exp_w3[layer].astype(BF16)
        w2 = exp_w2[layer].astype(BF16)
        if moe is None:
            moe = jnp.zeros((bsz, t * ACC_CHUNKS, LANES), F32)
        idx, gate = _route(aff[:, :, :n_lat], cap_lat)
        moe = _expert_ffn(idx, gate, hp, w1, w3, w2, moe, 1, n_lat, 0)
        if need_ctx:
            idx_c, gate_c = _route(aff[:, :, n_lat:], cap_ctx)
            idx_c = idx_c + (jnp.arange(bsz, dtype=jnp.int32) * n_ctx)[:, None, None]
            idx_c = jnp.transpose(idx_c, (1, 0, 2)).reshape(1, N_EXPERTS, bsz * cap_ctx)
            gate_c = jnp.transpose(gate_c, (1, 0, 2)).reshape(1, N_EXPERTS, bsz * cap_ctx)
            moe = _expert_ffn(idx_c, gate_c, hp, w1, w3, w2, moe, bsz, n_ctx, n_lat)
    return _final(xs, moe, tabs[DEPTH - 1], final_g[None, :], n_lat)
```

```python
import functools

import jax
import jax.numpy as jnp
from jax import lax
from jax.experimental import pallas as pl
from jax.experimental.pallas import tpu as pltpu

F32 = jnp.float32
BF16 = jnp.bfloat16
HIGHEST = lax.Precision.HIGHEST

D_MODEL = 1024
DEPTH = 4
GRID_W = 64
NORM_EPS = 1e-6
ROPE_BASE = 10000.0
NEG_INF = -1e30
LOG2_E = 1.4426950408889634

A_HEADS = 8
A_KV_HEADS = 2
A_HEAD_DIM = 64
WINDOW = 128
A_BLOCK = 128
B_WIDTH = 512
B_GROUPS = 4
B_CHUNK = 128
C_HEADS = 8
C_Q_LORA = 384
C_KV_LORA = 256
C_NOPE = 64
C_ROPE = 32
C_V = 64
D_WIDTH = 512
D_CONV = 3
N_EXPERTS = 16
EXPERT_FF = 1024
EC_FACTOR = 2

A_Q_W = A_HEADS * A_HEAD_DIM
A_KV_W = A_KV_HEADS * A_HEAD_DIM

LANES = 128
SUBLANES = 8
ROW_TILE = 256
SAMPLES_PER_STEP = 2
MOD_ROWS = 8
HP_CHUNKS = D_MODEL // (2 * LANES)
ACC_CHUNKS = D_MODEL // LANES
VMEM_LIMIT = 56 * 1024 * 1024

EVEN_COLS = 2 * A_KV_W + 2 * A_KV_W + A_Q_W + 2 * B_WIDTH
ODD_COLS = C_KV_LORA + LANES + C_Q_LORA + 3 * D_WIDTH
HEAD_PAD = LANES


def _cparams(sem):
    return pltpu.CompilerParams(dimension_semantics=sem, vmem_limit_bytes=VMEM_LIMIT)


def _nt_dot(a, b, precision=None):
    return lax.dot_general(a, b, (((1,), (1,)), ((), ())), precision=precision,
                           preferred_element_type=F32)


def _tn_dot(a, b, precision=None):
    return lax.dot_general(a, b, (((0,), (0,)), ((), ())), precision=precision,
                           preferred_element_type=F32)


def _mod_kernel(cc_ref, w_ref, b_ref, o_ref):
    cc = cc_ref[...]
    s = cc * jax.nn.sigmoid(cc)
    o_ref[0] = jnp.dot(s, w_ref[0], precision=HIGHEST, preferred_element_type=F32) + b_ref[0]


def _modulation(cc, mod_w, mod_b):
    rows = cc.shape[0]
    ncol = mod_w.shape[2] // D_MODEL
    return pl.pallas_call(
        _mod_kernel,
        grid=(DEPTH, ncol),
        in_specs=[
            pl.BlockSpec((rows, D_MODEL), lambda l, j: (0, 0)),
            pl.BlockSpec((1, D_MODEL, D_MODEL), lambda l, j: (l, 0, j)),
            pl.BlockSpec((1, 1, D_MODEL), lambda l, j: (l, 0, j)),
        ],
        out_specs=pl.BlockSpec((1, rows, D_MODEL), lambda l, j: (l, 0, j)),
        out_shape=jax.ShapeDtypeStruct((DEPTH, rows, mod_w.shape[2]), F32),
        compiler_params=_cparams(("arbitrary", "arbitrary")),
        name="adaln_modulation",
    )(cc, mod_w, mod_b.reshape(DEPTH, 1, -1))


def _residual_from_moe(x, moe_ref, s, gate_row, tm):
    cols = []
    for j in range(ACC_CHUNKS):
        chunk = moe_ref[s, pl.ds(j, tm, stride=ACC_CHUNKS), :]
        sl = slice(j * LANES, (j + 1) * LANES)
        cols.append(x[:, sl] + gate_row[:, sl] * chunk)
    return jnp.concatenate(cols, axis=1)


def _rms_mod(x, g, shift, scale):
    gain = g * (1.0 + scale)
    return x * lax.rsqrt(jnp.mean(x * x, axis=-1, keepdims=True) + NORM_EPS) * gain + shift


def _rms(x, g):
    return x * lax.rsqrt(jnp.mean(x * x, axis=-1, keepdims=True) + NORM_EPS) * g


def _rope_pairs(a, cos, sin, half, first):
    rot = jnp.where(first, pltpu.roll(a, LANES - half, 1), pltpu.roll(a, half, 1))
    return a * cos + rot * sin


def _in_even_kernel(has_moe, tm, *refs):
    if has_moe:
        (x_ref, moe_ref, modp_ref, mod_ref, g_ref, w_ref, cos_ref, sin_ref, lng_ref,
         xo_ref, q_ref, kd_ref, vd_ref, u_ref, vn_ref) = refs
    else:
        (x_ref, mod_ref, g_ref, w_ref, cos_ref, sin_ref, lng_ref,
         q_ref, kd_ref, vd_ref, u_ref, vn_ref) = refs
    cos = cos_ref[...]
    sin = sin_ref[...]
    lane = lax.broadcasted_iota(jnp.int32, (tm, LANES), 1)
    first = (lane % A_HEAD_DIM) < (A_HEAD_DIM // 2)
    half = A_HEAD_DIM // 2
    qoff = 4 * LANES
    uoff = qoff + A_Q_W
    scale = A_HEAD_DIM ** -0.5 * LOG2_E
    for s in range(SAMPLES_PER_STEP):
        x = x_ref[s]
        if has_moe:
            x = _residual_from_moe(x, moe_ref, s, modp_ref[s, 0, 5:6, :], tm)
            xo_ref[s] = x
        h = _rms_mod(x, g_ref[...], mod_ref[s, 0, 0:1, :], mod_ref[s, 0, 1:2, :])
        z = jnp.dot(h.astype(BF16), w_ref[...], preferred_element_type=F32)
        for j in range(2):
            sl = slice(j * LANES, (j + 1) * LANES)
            kd_ref[s, :, sl] = _rope_pairs(z[:, sl], cos, sin, half, first).astype(BF16)
        vd_ref[s] = z[:, 2 * LANES:4 * LANES].astype(BF16)
        for j in range(A_Q_W // LANES):
            sl = slice(qoff + j * LANES, qoff + (j + 1) * LANES)
            q_ref[s, :, j * LANES:(j + 1) * LANES] = (
                _rope_pairs(z[:, sl], cos, sin, half, first) * scale).astype(BF16)
        u_ref[s] = jax.nn.gelu(z[:, uoff:uoff + B_WIDTH])
        gv = jax.nn.gelu(z[:, uoff + B_WIDTH:uoff + 2 * B_WIDTH])
        mu = jnp.mean(gv, axis=-1, keepdims=True)
        var = jnp.mean(jnp.square(gv - mu), axis=-1, keepdims=True)
        vn_ref[s] = ((gv - mu) * lax.rsqrt(var + NORM_EPS) * lng_ref[...]).astype(BF16)


def _tile_is_ctx(n_lat_tiles):
    return lambda i: jnp.where(i >= n_lat_tiles, 1, 0)


def _in_even(x, moe, modp, mod, g1, w_in, cos, sin, lng, n_lat):
    bsz, t, _ = x.shape
    tm = ROW_TILE
    nt = t // tm
    is_ctx = _tile_is_ctx(n_lat // tm)
    has_moe = moe is not None
    spb = SAMPLES_PER_STEP
    row = lambda w: pl.BlockSpec((spb, tm, w), lambda b, i: (b, i, 0))
    mod_spec = pl.BlockSpec((spb, 1, MOD_ROWS, D_MODEL), lambda b, i: (b, is_ctx(i), 0, 0))
    full2 = lambda a: pl.BlockSpec(a.shape, lambda b, i: (0, 0))
    in_specs = [row(D_MODEL)]
    args = [x]
    if has_moe:
        in_specs += [pl.BlockSpec((spb, tm * ACC_CHUNKS, LANES), lambda b, i: (b, i, 0)), mod_spec]
        args += [moe, modp]
    in_specs += [mod_spec, full2(g1), full2(w_in),
                 pl.BlockSpec((tm, LANES), lambda b, i: (i, 0)),
                 pl.BlockSpec((tm, LANES), lambda b, i: (i, 0)), full2(lng)]
    args += [mod, g1, w_in, cos, sin, lng]
    out_specs = []
    out_shape = []
    if has_moe:
        out_specs.append(row(D_MODEL))
        out_shape.append(jax.ShapeDtypeStruct((bsz, t, D_MODEL), F32))
    out_specs += [row(A_Q_W), row(2 * LANES), row(2 * LANES), row(B_WIDTH), row(B_WIDTH)]
    out_shape += [jax.ShapeDtypeStruct((bsz, t, A_Q_W), BF16),
                  jax.ShapeDtypeStruct((bsz, t, 2 * LANES), BF16),
                  jax.ShapeDtypeStruct((bsz, t, 2 * LANES), BF16),
                  jax.ShapeDtypeStruct((bsz, t, B_WIDTH), F32),
                  jax.ShapeDtypeStruct((bsz, t, B_WIDTH), BF16)]
    outs = pl.pallas_call(
        functools.partial(_in_even_kernel, has_moe, tm),
        grid=(bsz // spb, nt), in_specs=in_specs, out_specs=out_specs, out_shape=out_shape,
        compiler_params=_cparams(("parallel", "arbitrary")),
        name="in_proj_even",
    )(*args)
    if not has_moe:
        outs = [x] + list(outs)
    return outs


def _win_attn_kernel(sink_ref, bias_ref, q_ref, kp_ref, kc_ref, kn_ref, kx_ref,
                     vp_ref, vc_ref, vn_ref, vx_ref, o_ref):
    blk = A_BLOCK
    g = A_HEADS // A_KV_HEADS
    bias = jnp.concatenate([bias_ref[0]] * g, axis=0)
    lane = lax.broadcasted_iota(jnp.int32, (blk, LANES), 1)
    low = lane < A_HEAD_DIM
    zero = jnp.zeros((), BF16)
    for h in range(A_KV_HEADS):
        hs = slice(h * LANES, (h + 1) * LANES)
        qs = []
        for cidx in range(g // 2):
            qc = q_ref[0, :, (h * (g // 2) + cidx) * LANES:(h * (g // 2) + cidx + 1) * LANES]
            qs.append(jnp.where(low, qc, zero))
            qs.append(jnp.where(low, zero, qc))
        qh = jnp.concatenate(qs, axis=0)
        kcat = jnp.concatenate([kp_ref[0, :, hs], kc_ref[0, :, hs], kn_ref[0, :, hs],
                                kx_ref[0, :, hs]], axis=0)
        vcat = jnp.concatenate([vp_ref[0, :, hs], vc_ref[0, :, hs], vn_ref[0, :, hs],
                                vx_ref[0, :, hs]], axis=0)
        s = _nt_dot(qh, kcat) + bias
        sink = jnp.concatenate(
            [jnp.full((blk, 1), sink_ref[h * g + j] * LOG2_E, F32) for j in range(g)], axis=0)
        m = jnp.maximum(jnp.max(s, axis=-1, keepdims=True), sink)
        p = jnp.exp2(s - m)
        den = jnp.sum(p, axis=-1, keepdims=True) + jnp.exp2(sink - m)
        o = jnp.dot((p * (1.0 / den)).astype(BF16), vcat,
                    preferred_element_type=F32)
        for cidx in range(g // 2):
            o_even = o[(2 * cidx) * blk:(2 * cidx + 1) * blk]
            o_odd = o[(2 * cidx + 1) * blk:(2 * cidx + 2) * blk]
            col = (h * (g // 2) + cidx) * LANES
            o_ref[0, :, col:col + LANES] = jnp.where(low, o_even, o_odd).astype(BF16)


def _win_attn(sink, q, kd, vd, n_lat, n_ctx):
    bsz, t, _ = q.shape
    blk = A_BLOCK
    nb = t // blk
    nlb = n_lat // blk
    kvw = 2 * LANES
    prev = pl.BlockSpec((1, blk, kvw), lambda b, i: (b, jnp.maximum(i - 1, 0), 0))
    cur = pl.BlockSpec((1, blk, kvw), lambda b, i: (b, i, 0))
    nxt = pl.BlockSpec((1, blk, kvw), lambda b, i: (b, jnp.minimum(i + 1, nb - 1), 0))
    ctx = pl.BlockSpec((1, n_ctx, kvw), lambda b, i: (b, t // n_ctx - 1, 0))
    r = jnp.arange(blk)[:, None]
    c = jnp.arange(blk)[None, :]
    yes = jnp.ones((blk, blk), jnp.bool_)
    no = jnp.zeros((blk, blk), jnp.bool_)
    ctx_ok = jnp.ones((blk, n_ctx), jnp.bool_)
    kinds = [(c >= r, yes, c <= r), (no, yes, c <= r), (c >= r, yes, no), (no, no, no)]
    bias = jnp.stack([jnp.where(jnp.concatenate([p, m, n, ctx_ok], axis=1), 0.0, NEG_INF)
                      for p, m, n in kinds]).astype(F32)
    kind = lambda i: jnp.where(i >= nlb, 3, jnp.where(i == 0, 1, jnp.where(i == nlb - 1, 2, 0)))
    return pl.pallas_call(
        _win_attn_kernel,
        grid=(bsz, nb),
        in_specs=[pl.BlockSpec(memory_space=pltpu.SMEM),
                  pl.BlockSpec((1, blk, 3 * blk + n_ctx), lambda b, i: (kind(i), 0, 0)),
                  pl.BlockSpec((1, blk, A_Q_W), lambda b, i: (b, i, 0)),
                  prev, cur, nxt, ctx, prev, cur, nxt, ctx],
        out_specs=pl.BlockSpec((1, blk, A_Q_W), lambda b, i: (b, i, 0)),
        out_shape=jax.ShapeDtypeStruct((bsz, t, A_Q_W), BF16),
        compiler_params=_cparams(("parallel", "arbitrary")),
        name="window_attention",
    )(sink, bias, q, kd, kd, kd, kd, vd, vd, vd, vd)


def _out_epilogue(tm, s, y, x_ref, mod_ref, g2_ref, rw_ref, xo_ref, hp_ref, aff_ref):
    x = x_ref[s] + mod_ref[s, 0, 2:3, :] * y
    xo_ref[s] = x
    h2 = _rms_mod(x, g2_ref[...], mod_ref[s, 0, 3:4, :], mod_ref[s, 0, 4:5, :])
    h_hi = h2.astype(BF16)
    h_hi32 = h_hi.astype(F32)
    bits = lax.bitcast_convert_type(h_hi32, jnp.uint32)
    half = D_MODEL // 2
    packed = (bits[:, :half] >> 16) | (bits[:, half:] & jnp.uint32(0xFFFF0000))
    for j in range(HP_CHUNKS):
        hp_ref[s, pl.ds(j, tm, stride=HP_CHUNKS), :] = packed[:, j * LANES:(j + 1) * LANES]
    h_lo = (h2 - h_hi32).astype(BF16)
    logits = (_nt_dot(rw_ref[0], h_hi) + _nt_dot(rw_ref[0], h_lo)) + _nt_dot(rw_ref[1], h_hi)
    mx = jnp.max(logits, axis=0, keepdims=True)
    ex = jnp.exp(logits - mx)
    aff_ref[s] = ex / jnp.sum(ex, axis=0, keepdims=True)


def _out_even_kernel(tm, o_ref, u_ref, vn_ref, ws_ref, bs_ref, wo_ref, x_ref, mod_ref, g2_ref,
                     rw_ref, xo_ref, hp_ref, aff_ref):
    dot = functools.partial(jnp.dot, preferred_element_type=F32)
    gw = B_WIDTH // B_GROUPS
    for s in range(SAMPLES_PER_STEP):
        rows = []
        for cidx in range(tm // B_CHUNK):
            rs = slice(cidx * B_CHUNK, (cidx + 1) * B_CHUNK)
            cols = []
            for g in range(B_GROUPS):
                cs = slice(g * gw, (g + 1) * gw)
                mixed = dot(ws_ref[g], vn_ref[s, rs, cs]) + bs_ref[:, cs]
                cols.append(u_ref[s, rs, cs] * mixed)
            rows.append(jnp.concatenate(cols, axis=1))
        sg = jnp.concatenate(rows, axis=0)
        y = dot(o_ref[s], wo_ref[:A_Q_W, :]) + dot(sg.astype(BF16), wo_ref[A_Q_W:, :])
        _out_epilogue(tm, s, y, x_ref, mod_ref, g2_ref, rw_ref, xo_ref, hp_ref, aff_ref)


def _out_specs_common(bsz, t, tm):
    spb = SAMPLES_PER_STEP
    row = lambda w: pl.BlockSpec((spb, tm, w), lambda b, i: (b, i, 0))
    out_specs = [row(D_MODEL),
                 pl.BlockSpec((spb, tm * HP_CHUNKS, LANES), lambda b, i: (b, i, 0)),
                 pl.BlockSpec((spb, N_EXPERTS, tm), lambda b, i: (b, 0, i))]
    out_shape = [jax.ShapeDtypeStruct((bsz, t, D_MODEL), F32),
                 jax.ShapeDtypeStruct((bsz, t * HP_CHUNKS, LANES), jnp.uint32),
                 jax.ShapeDtypeStruct((bsz, N_EXPERTS, t), F32)]
    return out_specs, out_shape


def _out_even(o, u, vn, ws, bs, wo, x, mod, g2, rw_t, n_lat):
    bsz, t, _ = x.shape
    tm = ROW_TILE
    is_ctx = _tile_is_ctx(n_lat // tm)
    spb = SAMPLES_PER_STEP
    row = lambda w: pl.BlockSpec((spb, tm, w), lambda b, i: (b, i, 0))
    full = lambda a: pl.BlockSpec(a.shape, lambda b, i: (0,) * a.ndim)
    mod_spec = pl.BlockSpec((spb, 1, MOD_ROWS, D_MODEL), lambda b, i: (b, is_ctx(i), 0, 0))
    out_specs, out_shape = _out_specs_common(bsz, t, tm)
    return pl.pallas_call(
        functools.partial(_out_even_kernel, tm),
        grid=(bsz // spb, t // tm),
        in_specs=[row(A_Q_W), row(B_WIDTH), row(B_WIDTH), full(ws), full(bs), full(wo),
                  row(D_MODEL), mod_spec, full(g2), full(rw_t)],
        out_specs=out_specs, out_shape=out_shape,
        compiler_params=_cparams(("parallel", "arbitrary")),
        name="out_proj_even",
    )(o, u, vn, ws, bs, wo, x, mod, g2, rw_t)


def _in_odd_kernel(has_moe, tm, *refs):
    (x_ref, moe_ref, modp_ref, mod_ref, g_ref, w_ref, cos_ref, sin_ref, qg_ref, kvg_ref,
     wuq_ref, wukv_ref, xo_ref, q_ref, k_ref, v_ref, gb_ref, u_ref) = refs
    cos = cos_ref[...]
    sin = sin_ref[...]
    lane = lax.broadcasted_iota(jnp.int32, (tm, LANES), 1)
    half = C_ROPE // 2
    first = lane < C_NOPE + half
    cqo = C_KV_LORA + LANES
    co = cqo + C_Q_LORA
    scale = (C_NOPE + C_ROPE) ** -0.5 * LOG2_E
    ones_col = jnp.where(lane == C_V, 1.0, 0.0)
    voff = C_HEADS * HEAD_PAD
    for s in range(SAMPLES_PER_STEP):
        x = _residual_from_moe(x_ref[s], moe_ref, s, modp_ref[s, 0, 5:6, :], tm)
        xo_ref[s] = x
        h = _rms_mod(x, g_ref[...], mod_ref[s, 0, 0:1, :], mod_ref[s, 0, 1:2, :])
        z = jnp.dot(h.astype(BF16), w_ref[...], preferred_element_type=F32)
        ckv = _rms(z[:, :C_KV_LORA], kvg_ref[...])
        kr = _rope_pairs(z[:, C_KV_LORA:C_KV_LORA + LANES], cos, sin, half, first)
        cq = _rms(z[:, cqo:cqo + C_Q_LORA], qg_ref[...])
        kv = jnp.dot(ckv.astype(BF16), wukv_ref[...], preferred_element_type=F32)
        qq = jnp.dot(cq.astype(BF16), wuq_ref[...], preferred_element_type=F32)
        for hd in range(C_HEADS):
            sl = slice(hd * HEAD_PAD, (hd + 1) * HEAD_PAD)
            q_ref[s, :, sl] = (_rope_pairs(qq[:, sl], cos, sin, half, first) * scale).astype(BF16)
            k_ref[s, :, sl] = (kv[:, sl] + kr).astype(BF16)
            v_ref[s, :, sl] = (kv[:, voff + hd * HEAD_PAD:voff + (hd + 1) * HEAD_PAD]
                               + ones_col).astype(BF16)
        gb_ref[s] = z[:, co:co + D_WIDTH]
        u_ref[s] = z[:, co + D_WIDTH:co + 2 * D_WIDTH] * z[:, co + 2 * D_WIDTH:co + 3 * D_WIDTH]


def _in_odd(x, moe, modp, mod, g1, w_in, cos, sin, qg, kvg, wuq, wukv, n_lat):
    bsz, t, _ = x.shape
    tm = ROW_TILE
    is_ctx = _tile_is_ctx(n_lat // tm)
    spb = SAMPLES_PER_STEP
    row = lambda w: pl.BlockSpec((spb, tm, w), lambda b, i: (b, i, 0))
    mod_spec = pl.BlockSpec((spb, 1, MOD_ROWS, D_MODEL), lambda b, i: (b, is_ctx(i), 0, 0))
    full2 = lambda a: pl.BlockSpec(a.shape, lambda b, i: (0, 0))
    hw = C_HEADS * HEAD_PAD
    return pl.pallas_call(
        functools.partial(_in_odd_kernel, True, tm),
        grid=(bsz // spb, t // tm),
        in_specs=[row(D_MODEL),
                  pl.BlockSpec((spb, tm * ACC_CHUNKS, LANES), lambda b, i: (b, i, 0)),
                  mod_spec, mod_spec, full2(g1), full2(w_in),
                  pl.BlockSpec((tm, LANES), lambda b, i: (i, 0)),
                  pl.BlockSpec((tm, LANES), lambda b, i: (i, 0)),
                  full2(qg), full2(kvg), full2(wuq), full2(wukv)],
        out_specs=[row(D_MODEL), row(hw), row(hw), row(hw), row(D_WIDTH), row(D_WIDTH)],
        out_shape=[jax.ShapeDtypeStruct((bsz, t, D_MODEL), F32),
                   jax.ShapeDtypeStruct((bsz, t, hw), BF16),
                   jax.ShapeDtypeStruct((bsz, t, hw), BF16),
                   jax.ShapeDtypeStruct((bsz, t, hw), BF16),
                   jax.ShapeDtypeStruct((bsz, t, D_WIDTH), F32),
                   jax.ShapeDtypeStruct((bsz, t, D_WIDTH), F32)],
        compiler_params=_cparams(("parallel", "arbitrary")),
        name="in_proj_odd",
    )(x, moe, modp, mod, g1, w_in, cos, sin, qg, kvg, wuq, wukv)


def _dense_attn_kernel(tk, q_ref, k_ref, v_ref, o_ref):
    q = q_ref[0]
    tq = q.shape[0]
    nk = k_ref.shape[1] // tk

    m = jnp.full((tq, 1), NEG_INF, F32)
    acc = jnp.zeros((tq, HEAD_PAD), F32)
    for j in range(nk):
        s = _nt_dot(q, k_ref[0, j * tk:(j + 1) * tk, :])
        m_new = jnp.maximum(m, jnp.max(s, axis=-1, keepdims=True))
        alpha = jnp.exp2(m - m_new)
        p = jnp.exp2(s - m_new)
        acc = alpha * acc + jnp.dot(p.astype(BF16), v_ref[0, j * tk:(j + 1) * tk, :],
                                    preferred_element_type=F32)
        m = m_new
    o_ref[0] = (acc / acc[:, C_V:C_V + 1]).astype(BF16)


def _dense_attn(q, k, v, q_start, q_len, k_start, k_len, tq):
    bsz, _, hw = q.shape
    tk = ROW_TILE
    q0 = q_start // tq
    kb = k_start // k_len
    return pl.pallas_call(
        functools.partial(_dense_attn_kernel, tk),
        grid=(bsz, C_HEADS, q_len // tq),
        in_specs=[pl.BlockSpec((1, tq, HEAD_PAD), lambda b, h, i: (b, q0 + i, h)),
                  pl.BlockSpec((1, k_len, HEAD_PAD), lambda b, h, i: (b, kb, h)),
                  pl.BlockSpec((1, k_len, HEAD_PAD), lambda b, h, i: (b, kb, h))],
        out_specs=pl.BlockSpec((1, tq, HEAD_PAD), lambda b, h, i: (b, i, h)),
        out_shape=jax.ShapeDtypeStruct((bsz, q_len, hw), BF16),
        compiler_params=_cparams(("parallel", "parallel", "arbitrary")),
        name="dense_attention",
    )(q, k, v)


def _out_odd_kernel(tm, n_lat_tiles, n_tiles, ol_ref, oc_ref, gb_ref, u_ref, up_ref, un_ref, cw_ref,
                    wo_ref, x_ref, mod_ref, g2_ref, rw_ref, xo_ref, hp_ref, aff_ref):
    i = pl.program_id(1)
    dot = functools.partial(jnp.dot, preferred_element_type=F32)
    has_prev = jnp.logical_and(i != 0, i != n_lat_tiles)
    has_next = jnp.logical_and(i != n_lat_tiles - 1, i != n_tiles - 1)
    hw = C_HEADS * HEAD_PAD
    for s in range(SAMPLES_PER_STEP):
        o_att = jnp.where(i < n_lat_tiles, ol_ref[s], oc_ref[s])
        u = u_ref[s]
        prev_row = jnp.where(has_prev, up_ref[s, SUBLANES - 1:SUBLANES, :], 0.0)
        next_row = jnp.where(has_next, un_ref[s, 0:1, :], 0.0)
        ridx = lax.broadcasted_iota(jnp.int32, u.shape, 0)
        u_m1 = jnp.where(ridx == 0, prev_row, pltpu.roll(u, 1, 0))
        u_p1 = jnp.where(ridx == tm - 1, next_row, pltpu.roll(u, tm - 1, 0))
        conv = u_m1 * cw_ref[0:1, :] + u * cw_ref[1:2, :] + u_p1 * cw_ref[2:3, :]
        c = gb_ref[s] * conv
        y = dot(o_att, wo_ref[:hw, :]) + dot(c.astype(BF16), wo_ref[hw:, :])
        _out_epilogue(tm, s, y, x_ref, mod_ref, g2_ref, rw_ref, xo_ref, hp_ref, aff_ref)


def _out_odd(o_lat, o_ctx, gb, u, cw, wo, x, mod, g2, rw_t, n_lat):
    bsz, t, _ = x.shape
    tm = ROW_TILE
    nt = t // tm
    nlt = n_lat // tm
    is_ctx = _tile_is_ctx(nlt)
    spb = SAMPLES_PER_STEP
    row = lambda w: pl.BlockSpec((spb, tm, w), lambda b, i: (b, i, 0))
    full = lambda a: pl.BlockSpec(a.shape, lambda b, i: (0,) * a.ndim)
    mod_spec = pl.BlockSpec((spb, 1, MOD_ROWS, D_MODEL), lambda b, i: (b, is_ctx(i), 0, 0))
    hw = C_HEADS * HEAD_PAD
    per = tm // SUBLANES
    last = t // SUBLANES - 1
    halo_prev = pl.BlockSpec((spb, SUBLANES, D_WIDTH), lambda b, i: (b, jnp.maximum(i * per - 1, 0), 0))
    halo_next = pl.BlockSpec((spb, SUBLANES, D_WIDTH), lambda b, i: (b, jnp.minimum((i + 1) * per, last), 0))
    out_specs, out_shape = _out_specs_common(bsz, t, tm)
    return pl.pallas_call(
        functools.partial(_out_odd_kernel, tm, nlt, nt),
        grid=(bsz // spb, nt),
        in_specs=[pl.BlockSpec((spb, tm, hw), lambda b, i: (b, jnp.minimum(i, nlt - 1), 0)),
                  pl.BlockSpec((spb, tm, hw), lambda b, i: (b, jnp.maximum(i - nlt, 0), 0)),
                  row(D_WIDTH), row(D_WIDTH), halo_prev, halo_next,
                  full(cw), full(wo), row(D_MODEL), mod_spec, full(g2), full(rw_t)],
        out_specs=out_specs, out_shape=out_shape,
        compiler_params=_cparams(("parallel", "arbitrary")),
        name="out_proj_odd",
    )(o_lat, o_ctx, gb, u, u, u, cw, wo, x, mod, g2, rw_t)


def _route_kernel(cap, nblk, width, a_ref, idx_ref, gate_ref):
    n_e = N_EXPERTS
    a = a_ref[0]
    bits = lax.bitcast_convert_type(a, jnp.int32)

    def count(mask):
        c = jnp.sum(jnp.where(mask, 1.0, 0.0), axis=2, keepdims=True)
        return jnp.sum(c, axis=1, keepdims=True)

    def search(it, thr):
        cand = thr | jnp.left_shift(jnp.int32(1), 30 - it)
        return jnp.where(count(bits >= cand) >= cap, cand, thr)

    thr = lax.fori_loop(0, 31, search, jnp.zeros((n_e, 1, 1), jnp.int32))
    gt = bits > thr
    eq = bits == thr
    need = cap - count(gt)

    rows = n_e * nblk
    li = lax.broadcasted_iota(jnp.int32, (width, width), 0)
    lj = lax.broadcasted_iota(jnp.int32, (width, width), 1)
    upper = jnp.where(li <= lj, 1.0, 0.0).astype(BF16)
    ones_sq = jnp.ones((width, width), BF16)
    if nblk > 1:
        ri = lax.broadcasted_iota(jnp.int32, (rows, rows), 0)
        rj = lax.broadcasted_iota(jnp.int32, (rows, rows), 1)
        lower = jnp.where(jnp.logical_and(ri // nblk == rj // nblk, rj < ri), 1.0, 0.0).astype(BF16)

    def prefix(mask3):
        m2 = jnp.where(mask3, 1.0, 0.0).astype(BF16).reshape(rows, width)
        local = jnp.dot(m2, upper, preferred_element_type=F32)
        total = jnp.dot(m2, ones_sq, preferred_element_type=F32)
        if nblk > 1:
            excl = jnp.dot(lower, total.astype(BF16), preferred_element_type=F32)
        else:
            excl = jnp.zeros_like(total)
        return local, total, excl

    l_eq, _, x_eq = prefix(eq)
    tie_rank = (l_eq + x_eq).reshape(n_e, nblk, width)
    sel = jnp.logical_or(gt, jnp.logical_and(eq, tie_rank <= need))
    local, total, excl = prefix(sel)
    pin = local + excl

    slot_l = lax.broadcasted_iota(jnp.int32, (nblk, cap), 1).astype(F32)
    blk_s = lax.broadcasted_iota(jnp.int32, (nblk, cap), 0).astype(F32)
    slot_s = lax.broadcasted_iota(jnp.int32, (cap, width), 0).astype(F32)
    lane_w = lax.broadcasted_iota(jnp.int32, (cap, width), 1).astype(F32)
    ones_r = jnp.ones((SUBLANES, width), BF16)
    ones_rf = jnp.ones((SUBLANES, width), F32)
    reps = cap // width if cap >= width else 1
    for e in range(n_e):
        rs = slice(e * nblk, (e + 1) * nblk)
        pin_e = pin[rs]
        aff_e = a[e]
        if nblk > 1:
            lo = jnp.concatenate([excl[rs]] * reps, axis=1)[:, :cap]
            hi = lo + jnp.concatenate([total[rs]] * reps, axis=1)[:, :cap]
            oh_t = jnp.where(jnp.logical_and(lo <= slot_l, slot_l < hi), 1.0, 0.0)
            pin_g = _tn_dot(oh_t, pin_e, precision=HIGHEST)
            aff_g = _tn_dot(oh_t, aff_e, precision=HIGHEST)
            blk_row = jnp.sum(oh_t * blk_s, axis=0, keepdims=True)
        else:
            pin_g = jnp.broadcast_to(pin_e, (cap, width))
            aff_g = jnp.broadcast_to(aff_e, (cap, width))
            blk_row = jnp.zeros((1, cap), F32)
        ind = jnp.where(pin_g <= slot_s, 1.0, 0.0).astype(BF16)
        cnt_row = _nt_dot(ones_r, ind)[0:1]
        cnt_b = jnp.dot(ind, ones_sq, preferred_element_type=F32)
        picked = jnp.where(lane_w == cnt_b, aff_g, 0.0)
        gate_row = _nt_dot(ones_rf, picked, precision=HIGHEST)[0:1]
        idx_ref[0, e] = (blk_row * width + cnt_row).astype(jnp.int32)
        gate_ref[0, e] = gate_row


def _route(aff, cap):
    bsz, n_e, n = aff.shape
    width = LANES if n % (LANES * SUBLANES) == 0 else n
    nblk = n // width
    a4 = aff.reshape(bsz, n_e, nblk, width)
    idx, gate = pl.pallas_call(
        functools.partial(_route_kernel, cap, nblk, width),
        grid=(bsz,),
        in_specs=[pl.BlockSpec((1, n_e, nblk, width), lambda b: (b, 0, 0, 0))],
        out_specs=[pl.BlockSpec((1, n_e, 1, cap), lambda b: (b, 0, 0, 0)),
                   pl.BlockSpec((1, n_e, 1, cap), lambda b: (b, 0, 0, 0))],
        out_shape=[jax.ShapeDtypeStruct((bsz, n_e, 1, cap), jnp.int32),
                   jax.ShapeDtypeStruct((bsz, n_e, 1, cap), F32)],
        compiler_params=_cparams(("parallel",)),
        name="expert_choice_routing",
    )(a4)
    return idx.reshape(bsz, n_e, cap), gate.reshape(bsz, n_e, cap)


def _ffn_copies(hp_hbm, out_hbm, hsrc, acc, sem_in, sem_out, grp, nsub, n_per, row0):
    ins = []
    outs = []
    for sb in range(nsub):
        b = grp * nsub + sb
        ins.append(pltpu.make_async_copy(
            hp_hbm.at[b, pl.ds(row0 * HP_CHUNKS, n_per * HP_CHUNKS), :],
            hsrc.at[pl.ds(sb * n_per * HP_CHUNKS, n_per * HP_CHUNKS), :], sem_in.at[sb]))
        outs.append(pltpu.make_async_copy(
            acc.at[pl.ds(sb * n_per * ACC_CHUNKS, n_per * ACC_CHUNKS), :],
            out_hbm.at[b, pl.ds(row0 * ACC_CHUNKS, n_per * ACC_CHUNKS), :], sem_out.at[sb]))
    return ins, outs


def _ffn_kernel(m_slots, nsub, n_per, row0, unroll, idx_ref, gate_ref, hp_hbm, w1_ref, w3_ref,
                w2_ref, _moe_in, out_hbm, hsrc, acc, xt, yt, sem_in, sem_out):
    grp = pl.program_id(0)
    e = pl.program_id(1)
    stride = m_slots + SUBLANES
    ins, outs = _ffn_copies(hp_hbm, out_hbm, hsrc, acc, sem_in, sem_out, grp, nsub, n_per, row0)

    @pl.when(e == 0)
    def _():
        for cp in ins:
            cp.start()
        acc[...] = jnp.zeros(acc.shape, F32)
        for cp in ins:
            cp.wait()

    def gather(c, carry):
        for k in range(unroll):
            mi = c * unroll + k
            src = pl.multiple_of(idx_ref[0, 0, 0, mi] * HP_CHUNKS, HP_CHUNKS)
            xt[pl.ds(mi, HP_CHUNKS, stride=stride), :] = hsrc[pl.ds(src, HP_CHUNKS), :]
        return carry

    lax.fori_loop(0, m_slots // unroll, gather, 0)

    lo = []
    hi = []
    for j in range(HP_CHUNKS):
        w = xt[pl.ds(j * stride, m_slots), :]
        lo.append(lax.bitcast_convert_type(w << 16, F32))
        hi.append(lax.bitcast_convert_type(w & jnp.uint32(0xFFFF0000), F32))
    xs = jnp.concatenate(lo + hi, axis=1).astype(BF16)
    dot = functools.partial(jnp.dot, preferred_element_type=F32)
    a = dot(xs, w1_ref[0])
    b = dot(xs, w3_ref[0])
    act = (a * jax.nn.sigmoid(a) * b).astype(BF16)
    y = dot(act, w2_ref[0])
    for j in range(ACC_CHUNKS):
        yt[pl.ds(j * stride, m_slots), :] = y[:, j * LANES:(j + 1) * LANES]

    def scatter(c, carry):
        dst = []
        val = []
        for k in range(unroll):
            mi = c * unroll + k
            d = pl.multiple_of(idx_ref[0, 0, 0, mi] * ACC_CHUNKS, ACC_CHUNKS)
            slab = yt[pl.ds(mi, ACC_CHUNKS, stride=stride), :] * gate_ref[0, 0, 0, mi]
            dst.append(d)
            val.append(acc[pl.ds(d, ACC_CHUNKS), :] + slab)
        for d, v in zip(dst, val):
            acc[pl.ds(d, ACC_CHUNKS), :] = v
        return carry

    lax.fori_loop(0, m_slots // unroll, scatter, 0)

    @pl.when(e == N_EXPERTS - 1)
    def _():
        for cp in outs:
            cp.start()
        for cp in outs:
            cp.wait()


def _expert_ffn(idx, gate, hp, w1, w3, w2, moe_buf, nsub, n_per, row0):
    groups, n_e, m_slots = idx.shape
    n_tok = nsub * n_per
    stride = m_slots + SUBLANES
    unroll = 4
    smem = lambda: pl.BlockSpec((1, 1, 1, m_slots), lambda g, e: (g, e, 0, 0), memory_space=pltpu.SMEM)
    wspec = lambda: pl.BlockSpec((1, D_MODEL, EXPERT_FF), lambda g, e: (e, 0, 0))
    return pl.pallas_call(
        functools.partial(_ffn_kernel, m_slots, nsub, n_per, row0, unroll),
        grid=(groups, n_e),
        in_specs=[smem(), smem(), pl.BlockSpec(memory_space=pl.ANY), wspec(), wspec(),
                  pl.BlockSpec((1, EXPERT_FF, D_MODEL), lambda g, e: (e, 0, 0)),
                  pl.BlockSpec(memory_space=pl.ANY)],
        out_specs=pl.BlockSpec(memory_space=pl.ANY),
        out_shape=jax.ShapeDtypeStruct(moe_buf.shape, F32),
        scratch_shapes=[pltpu.VMEM((n_tok * HP_CHUNKS, LANES), jnp.uint32),
                        pltpu.VMEM((n_tok * ACC_CHUNKS, LANES), F32),
                        pltpu.VMEM((HP_CHUNKS * stride, LANES), jnp.uint32),
                        pltpu.VMEM((ACC_CHUNKS * stride, LANES), F32),
                        pltpu.SemaphoreType.DMA((nsub,)),
                        pltpu.SemaphoreType.DMA((nsub,))],
        input_output_aliases={6: 0},
        compiler_params=_cparams(("arbitrary", "arbitrary")),
        name="expert_ffn",
    )(idx.reshape(groups, n_e, 1, m_slots), gate.reshape(groups, n_e, 1, m_slots), hp, w1, w3, w2,
      moe_buf)


def _final_kernel(tm, x_ref, moe_ref, modp_ref, g_ref, o_ref):
    x = _residual_from_moe(x_ref[0], moe_ref, 0, modp_ref[0, 0, 5:6, :], tm)
    o_ref[0] = _rms(x, g_ref[...])


def _final(x, moe, modp, g, n_lat):
    bsz = x.shape[0]
    tm = ROW_TILE
    return pl.pallas_call(
        functools.partial(_final_kernel, tm),
        grid=(bsz, n_lat // tm),
        in_specs=[pl.BlockSpec((1, tm, D_MODEL), lambda b, i: (b, i, 0)),
                  pl.BlockSpec((1, tm * ACC_CHUNKS, LANES), lambda b, i: (b, i, 0)),
                  pl.BlockSpec((1, 1, MOD_ROWS, D_MODEL), lambda b, i: (b, 0, 0, 0)),
                  pl.BlockSpec(g.shape, lambda b, i: (0, 0))],
        out_specs=pl.BlockSpec((1, tm, D_MODEL), lambda b, i: (b, i, 0)),
        out_shape=jax.ShapeDtypeStruct((bsz, n_lat, D_MODEL), F32),
        compiler_params=_cparams(("parallel", "arbitrary")),
        name="final_norm",
    )(x, moe, modp, g)


def _rope_angles(rows, rot_dim):
    row = jnp.repeat(jnp.arange(rows, dtype=F32), GRID_W)
    col = jnp.tile(jnp.arange(GRID_W, dtype=F32), rows)
    n_freq = rot_dim // 4
    inv_freq = ROPE_BASE ** (-jnp.arange(n_freq, dtype=F32) / n_freq)
    ang = jnp.concatenate([row[:, None] * inv_freq[None, :], col[:, None] * inv_freq[None, :]], axis=-1)
    return jnp.cos(ang), jnp.sin(ang)


def _rope_tables_even(n_lat, n_ctx):
    cos, sin = _rope_angles(n_lat // GRID_W, A_HEAD_DIM)
    cos_h = jnp.concatenate([cos, cos], axis=1)
    sin_h = jnp.concatenate([-sin, sin], axis=1)
    reps = LANES // A_HEAD_DIM
    cos_l = jnp.tile(cos_h, (1, reps))
    sin_l = jnp.tile(sin_h, (1, reps))
    return (jnp.concatenate([cos_l, jnp.ones((n_ctx, LANES), F32)], axis=0),
            jnp.concatenate([sin_l, jnp.zeros((n_ctx, LANES), F32)], axis=0))


def _rope_tables_odd(n_lat, n_ctx):
    cos, sin = _rope_angles(n_lat // GRID_W, C_ROPE)
    n = cos.shape[0]
    pad = LANES - C_NOPE - C_ROPE
    cos_l = jnp.concatenate([jnp.ones((n, C_NOPE), F32), cos, cos, jnp.ones((n, pad), F32)], axis=1)
    sin_l = jnp.concatenate([jnp.zeros((n, C_NOPE), F32), -sin, sin, jnp.zeros((n, pad), F32)], axis=1)
    return (jnp.concatenate([cos_l, jnp.ones((n_ctx, LANES), F32)], axis=0),
            jnp.concatenate([sin_l, jnp.zeros((n_ctx, LANES), F32)], axis=0))


def _even_w_in(w):
    k = w[:, :A_KV_W]
    v = w[:, A_KV_W:2 * A_KV_W]
    dup = lambda a: jnp.concatenate(
        [a[:, h * A_HEAD_DIM:(h + 1) * A_HEAD_DIM] for h in range(A_KV_HEADS) for _ in range(2)], axis=1)
    return jnp.concatenate([dup(k), dup(v), w[:, 2 * A_KV_W:]], axis=1).astype(BF16)


def _odd_w_in(w):
    d = w.shape[0]
    kr = w[:, C_KV_LORA:C_KV_LORA + C_ROPE]
    kr_group = jnp.concatenate(
        [jnp.zeros((d, C_NOPE), F32), kr, jnp.zeros((d, LANES - C_NOPE - C_ROPE), F32)], axis=1)
    return jnp.concatenate([w[:, :C_KV_LORA], kr_group, w[:, C_KV_LORA + C_ROPE:]], axis=1).astype(BF16)


def _odd_w_uq(w):
    r = w.shape[0]
    w3 = w.reshape(r, C_HEADS, C_NOPE + C_ROPE)
    pad = jnp.zeros((r, C_HEADS, HEAD_PAD - C_NOPE - C_ROPE), F32)
    return jnp.concatenate([w3, pad], axis=2).reshape(r, C_HEADS * HEAD_PAD).astype(BF16)


def _odd_w_ukv(w):
    r = w.shape[0]
    w3 = w.reshape(r, C_HEADS, C_NOPE + C_V)
    kpad = jnp.zeros((r, C_HEADS, HEAD_PAD - C_NOPE), F32)
    vpad = jnp.zeros((r, C_HEADS, HEAD_PAD - C_V), F32)
    kpart = jnp.concatenate([w3[:, :, :C_NOPE], kpad], axis=2).reshape(r, C_HEADS * HEAD_PAD)
    vpart = jnp.concatenate([w3[:, :, C_NOPE:], vpad], axis=2).reshape(r, C_HEADS * HEAD_PAD)
    return jnp.concatenate([kpart, vpart], axis=1).astype(BF16)


def _odd_w_out(w):
    d = w.shape[1]
    att = w[:C_HEADS * C_V].reshape(C_HEADS, C_V, d)
    att = jnp.concatenate([att, jnp.zeros((C_HEADS, HEAD_PAD - C_V, d), F32)], axis=1)
    return jnp.concatenate([att.reshape(C_HEADS * HEAD_PAD, d), w[C_HEADS * C_V:]], axis=0).astype(BF16)


def _mod_table(mods_l, bsz):
    lat = mods_l[:bsz].reshape(bsz, 1, 6, D_MODEL)
    ctx = jnp.broadcast_to(mods_l[bsz].reshape(1, 1, 6, D_MODEL), (bsz, 1, 6, D_MODEL))
    tab = jnp.concatenate([lat, ctx], axis=1)
    return jnp.pad(tab, ((0, 0), (0, 0), (0, MOD_ROWS - 6), (0, 0)))


def kernel(x, c, ctx, c_ctx, mod_w, mod_b, norm1_g, norm2_g, ev_w_in, ev_sink, ev_sgu_norm_g, ev_sgu_w, ev_sgu_b, ev_w_out, od_w_in, od_q_norm_g, od_w_uq, od_kv_norm_g, od_w_ukv, od_conv_w, od_w_out, router_w, exp_w1, exp_w3, exp_w2, final_g):
    bsz, n_lat, _ = x.shape
    n_ctx = ctx.shape[1]
    t = n_lat + n_ctx
    cap_lat = EC_FACTOR * n_lat // N_EXPERTS
    cap_ctx = EC_FACTOR * n_ctx // N_EXPERTS

    mod_rows = -(-(bsz + 1) // SUBLANES) * SUBLANES
    cc = jnp.concatenate([c, c_ctx[None, :], jnp.zeros((mod_rows - bsz - 1, D_MODEL), F32)], axis=0)
    mods = _modulation(cc, mod_w, mod_b)
    tabs = [_mod_table(mods[l], bsz) for l in range(DEPTH)]

    cos_e, sin_e = _rope_tables_even(n_lat, n_ctx)
    cos_o, sin_o = _rope_tables_odd(n_lat, n_ctx)

    xs = jnp.concatenate([x, ctx], axis=1)
    moe = None
    for layer in range(DEPTH):
        i = layer // 2
        need_ctx = layer < DEPTH - 1
        g1 = norm1_g[layer][None, :]
        g2 = norm2_g[layer][None, :]
        rw_f = router_w[layer].T
        rw_hi = rw_f.astype(BF16)
        rw_t = jnp.stack([rw_hi, (rw_f - rw_hi.astype(F32)).astype(BF16)])
        modp = tabs[layer - 1] if layer > 0 else None
        if layer % 2 == 0:
            xs, q, kd, vd, u, vn = _in_even(xs, moe, modp, tabs[layer], g1, _even_w_in(ev_w_in[i]),
                                            cos_e, sin_e, ev_sgu_norm_g[i][None, :], n_lat)
            o = _win_attn(ev_sink[i], q, kd, vd, n_lat, n_ctx)
            bs = jnp.repeat(ev_sgu_b[i].T, B_WIDTH // B_GROUPS, axis=1)
            xs, hp, aff = _out_even(o, u, vn, ev_sgu_w[i].astype(BF16), bs, ev_w_out[i].astype(BF16),
                                    xs, tabs[layer], g2, rw_t, n_lat)
        else:
            xs, q, k, v, gb, u = _in_odd(xs, moe, modp, tabs[layer], g1, _odd_w_in(od_w_in[i]),
                                         cos_o, sin_o, od_q_norm_g[i][None, :], od_kv_norm_g[i][None, :],
                                         _odd_w_uq(od_w_uq[i]), _odd_w_ukv(od_w_ukv[i]), n_lat)
            o_lat = _dense_attn(q, k, v, 0, n_lat, 0, t, 2 * ROW_TILE)
            o_ctx = _dense_attn(q, k, v, n_lat, n_ctx, n_lat, n_ctx, n_ctx)
            xs, hp, aff = _out_odd(o_lat, o_ctx, gb, u, od_conv_w[i], _odd_w_out(od_w_out[i]), xs, tabs[layer],
                                   g2, rw_t, n_lat)
        w1 = exp_w1[layer].astype(BF16)
        w3 = exp_w3[layer].astype(BF16)
        w2 = exp_w2[layer].astype(BF16)
        if moe is None:
            moe = jnp.zeros((bsz, t * ACC_CHUNKS, LANES), F32)
        idx, gate = _route(aff[:, :, :n_lat], cap_lat)
        moe = _expert_ffn(idx, gate, hp, w1, w3, w2, moe, 1, n_lat, 0)
        if need_ctx:
            idx_c, gate_c = _route(aff[:, :, n_lat:], cap_ctx)
            idx_c = idx_c + (jnp.arange(bsz, dtype=jnp.int32) * n_ctx)[:, None, None]
            idx_c = jnp.transpose(idx_c, (1, 0, 2)).reshape(1, N_EXPERTS, bsz * cap_ctx)
            gate_c = jnp.transpose(gate_c, (1, 0, 2)).reshape(1, N_EXPERTS, bsz * cap_ctx)
            moe = _expert_ffn(idx_c, gate_c, hp, w1, w3, w2, moe, bsz, n_ctx, n_lat)
    return _final(xs, moe, tabs[DEPTH - 1], final_g[None, :], n_lat)
```

```python
import functools

import jax
import jax.numpy as jnp
from jax import lax
from jax.experimental import pallas as pl
from jax.experimental.pallas import tpu as pltpu

F32 = jnp.float32
BF16 = jnp.bfloat16
HIGHEST = lax.Precision.HIGHEST

D_MODEL = 1024
DEPTH = 4
GRID_W = 64
NORM_EPS = 1e-6
ROPE_BASE = 10000.0
NEG_INF = -1e30
LOG2_E = 1.4426950408889634
PIN_SPLIT = 64.0

A_HEADS = 8
A_KV_HEADS = 2
A_HEAD_DIM = 64
WINDOW = 128
A_BLOCK = 128
B_WIDTH = 512
B_GROUPS = 4
B_CHUNK = 128
C_HEADS = 8
C_Q_LORA = 384
C_KV_LORA = 256
C_NOPE = 64
C_ROPE = 32
C_V = 64
D_WIDTH = 512
D_CONV = 3
N_EXPERTS = 16
EXPERT_FF = 1024
EC_FACTOR = 2

A_Q_W = A_HEADS * A_HEAD_DIM
A_KV_W = A_KV_HEADS * A_HEAD_DIM

LANES = 128
SUBLANES = 8
ROW_TILE = 256
SAMPLES_PER_STEP = 2
MOD_ROWS = 8
HP_CHUNKS = D_MODEL // (2 * LANES)
ACC_CHUNKS = D_MODEL // LANES
VMEM_LIMIT = 56 * 1024 * 1024

EVEN_COLS = 2 * A_KV_W + 2 * A_KV_W + A_Q_W + 2 * B_WIDTH
ODD_COLS = C_KV_LORA + LANES + C_Q_LORA + 3 * D_WIDTH
HEAD_PAD = LANES


def _cparams(sem):
    return pltpu.CompilerParams(dimension_semantics=sem, vmem_limit_bytes=VMEM_LIMIT)


def _nt_dot(a, b, precision=None):
    return lax.dot_general(a, b, (((1,), (1,)), ((), ())), precision=precision,
                           preferred_element_type=F32)


def _tn_dot(a, b, precision=None):
    return lax.dot_general(a, b, (((0,), (0,)), ((), ())), precision=precision,
                           preferred_element_type=F32)


def _mod_kernel(cc_ref, w_ref, b_ref, o_ref):
    cc = cc_ref[...]
    s = cc * jax.nn.sigmoid(cc)
    o_ref[0] = jnp.dot(s, w_ref[0], precision=HIGHEST, preferred_element_type=F32) + b_ref[0]


def _modulation(cc, mod_w, mod_b):
    rows = cc.shape[0]
    ncol = mod_w.shape[2] // D_MODEL
    return pl.pallas_call(
        _mod_kernel,
        grid=(DEPTH, ncol),
        in_specs=[
            pl.BlockSpec((rows, D_MODEL), lambda l, j: (0, 0)),
            pl.BlockSpec((1, D_MODEL, D_MODEL), lambda l, j: (l, 0, j)),
            pl.BlockSpec((1, 1, D_MODEL), lambda l, j: (l, 0, j)),
        ],
        out_specs=pl.BlockSpec((1, rows, D_MODEL), lambda l, j: (l, 0, j)),
        out_shape=jax.ShapeDtypeStruct((DEPTH, rows, mod_w.shape[2]), F32),
        compiler_params=_cparams(("arbitrary", "arbitrary")),
        name="adaln_modulation",
    )(cc, mod_w, mod_b.reshape(DEPTH, 1, -1))


def _residual_from_moe(x, moe_ref, s, gate_row, tm):
    cols = []
    for j in range(ACC_CHUNKS):
        chunk = moe_ref[s, pl.ds(j, tm, stride=ACC_CHUNKS), :]
        sl = slice(j * LANES, (j + 1) * LANES)
        cols.append(x[:, sl] + gate_row[:, sl] * chunk)
    return jnp.concatenate(cols, axis=1)


def _rms_mod(x, g, shift, scale):
    gain = g * (1.0 + scale)
    return x * lax.rsqrt(jnp.mean(x * x, axis=-1, keepdims=True) + NORM_EPS) * gain + shift


def _rms(x, g):
    return x * lax.rsqrt(jnp.mean(x * x, axis=-1, keepdims=True) + NORM_EPS) * g


def _rope_pairs(a, cos, sin, half, first):
    rot = jnp.where(first, pltpu.roll(a, LANES - half, 1), pltpu.roll(a, half, 1))
    return a * cos + rot * sin


def _in_even_kernel(has_moe, tm, *refs):
    if has_moe:
        (x_ref, moe_ref, modp_ref, mod_ref, g_ref, w_ref, cos_ref, sin_ref, lng_ref,
         xo_ref, q_ref, kd_ref, vd_ref, u_ref, vn_ref) = refs
    else:
        (x_ref, mod_ref, g_ref, w_ref, cos_ref, sin_ref, lng_ref,
         q_ref, kd_ref, vd_ref, u_ref, vn_ref) = refs
    cos = cos_ref[...]
    sin = sin_ref[...]
    lane = lax.broadcasted_iota(jnp.int32, (tm, LANES), 1)
    first = (lane % A_HEAD_DIM) < (A_HEAD_DIM // 2)
    half = A_HEAD_DIM // 2
    qoff = 4 * LANES
    uoff = qoff + A_Q_W
    scale = A_HEAD_DIM ** -0.5 * LOG2_E
    for s in range(SAMPLES_PER_STEP):
        x = x_ref[s]
        if has_moe:
            x = _residual_from_moe(x, moe_ref, s, modp_ref[s, 0, 5:6, :], tm)
            xo_ref[s] = x
        h = _rms_mod(x, g_ref[...], mod_ref[s, 0, 0:1, :], mod_ref[s, 0, 1:2, :])
        z = jnp.dot(h.astype(BF16), w_ref[...], preferred_element_type=F32)
        for j in range(2):
            sl = slice(j * LANES, (j + 1) * LANES)
            kd_ref[s, :, sl] = _rope_pairs(z[:, sl], cos, sin, half, first).astype(BF16)
        vd_ref[s] = z[:, 2 * LANES:4 * LANES].astype(BF16)
        for j in range(A_Q_W // LANES):
            sl = slice(qoff + j * LANES, qoff + (j + 1) * LANES)
            q_ref[s, :, j * LANES:(j + 1) * LANES] = (
                _rope_pairs(z[:, sl], cos, sin, half, first) * scale).astype(BF16)
        u_ref[s] = jax.nn.gelu(z[:, uoff:uoff + B_WIDTH])
        gv = jax.nn.gelu(z[:, uoff + B_WIDTH:uoff + 2 * B_WIDTH])
        mu = jnp.mean(gv, axis=-1, keepdims=True)
        var = jnp.mean(jnp.square(gv - mu), axis=-1, keepdims=True)
        vn_ref[s] = ((gv - mu) * lax.rsqrt(var + NORM_EPS) * lng_ref[...]).astype(BF16)


def _tile_is_ctx(n_lat_tiles):
    return lambda i: jnp.where(i >= n_lat_tiles, 1, 0)


def _in_even(x, moe, modp, mod, g1, w_in, cos, sin, lng, n_lat):
    bsz, t, _ = x.shape
    tm = ROW_TILE
    nt = t // tm
    is_ctx = _tile_is_ctx(n_lat // tm)
    has_moe = moe is not None
    spb = SAMPLES_PER_STEP
    row = lambda w: pl.BlockSpec((spb, tm, w), lambda b, i: (b, i, 0))
    mod_spec = pl.BlockSpec((spb, 1, MOD_ROWS, D_MODEL), lambda b, i: (b, is_ctx(i), 0, 0))
    full2 = lambda a: pl.BlockSpec(a.shape, lambda b, i: (0, 0))
    in_specs = [row(D_MODEL)]
    args = [x]
    if has_moe:
        in_specs += [pl.BlockSpec((spb, tm * ACC_CHUNKS, LANES), lambda b, i: (b, i, 0)), mod_spec]
        args += [moe, modp]
    in_specs += [mod_spec, full2(g1), full2(w_in),
                 pl.BlockSpec((tm, LANES), lambda b, i: (i, 0)),
                 pl.BlockSpec((tm, LANES), lambda b, i: (i, 0)), full2(lng)]
    args += [mod, g1, w_in, cos, sin, lng]
    out_specs = []
    out_shape = []
    if has_moe:
        out_specs.append(row(D_MODEL))
        out_shape.append(jax.ShapeDtypeStruct((bsz, t, D_MODEL), F32))
    out_specs += [row(A_Q_W), row(2 * LANES), row(2 * LANES), row(B_WIDTH), row(B_WIDTH)]
    out_shape += [jax.ShapeDtypeStruct((bsz, t, A_Q_W), BF16),
                  jax.ShapeDtypeStruct((bsz, t, 2 * LANES), BF16),
                  jax.ShapeDtypeStruct((bsz, t, 2 * LANES), BF16),
                  jax.ShapeDtypeStruct((bsz, t, B_WIDTH), F32),
                  jax.ShapeDtypeStruct((bsz, t, B_WIDTH), BF16)]
    outs = pl.pallas_call(
        functools.partial(_in_even_kernel, has_moe, tm),
        grid=(bsz // spb, nt), in_specs=in_specs, out_specs=out_specs, out_shape=out_shape,
        compiler_params=_cparams(("parallel", "arbitrary")),
        name="in_proj_even",
    )(*args)
    if not has_moe:
        outs = [x] + list(outs)
    return outs


def _win_attn_kernel(sink_ref, bias_ref, q_ref, kp_ref, kc_ref, kn_ref, kx_ref,
                     vp_ref, vc_ref, vn_ref, vx_ref, o_ref):
    blk = A_BLOCK
    g = A_HEADS // A_KV_HEADS
    bias = jnp.concatenate([bias_ref[0]] * g, axis=0)
    lane = lax.broadcasted_iota(jnp.int32, (blk, LANES), 1)
    low = lane < A_HEAD_DIM
    zero = jnp.zeros((), BF16)
    for smp, h in [(a, b) for a in range(SAMPLES_PER_STEP) for b in range(A_KV_HEADS)]:
        hs = slice(h * LANES, (h + 1) * LANES)
        qs = []
        for cidx in range(g // 2):
            qc = q_ref[smp, :, (h * (g // 2) + cidx) * LANES:(h * (g // 2) + cidx + 1) * LANES]
            qs.append(jnp.where(low, qc, zero))
            qs.append(jnp.where(low, zero, qc))
        qh = jnp.concatenate(qs, axis=0)
        kcat = jnp.concatenate([kp_ref[smp, :, hs], kc_ref[smp, :, hs], kn_ref[smp, :, hs],
                                kx_ref[smp, :, hs]], axis=0)
        vcat = jnp.concatenate([vp_ref[smp, :, hs], vc_ref[smp, :, hs], vn_ref[smp, :, hs],
                                vx_ref[smp, :, hs]], axis=0)
        s = _nt_dot(qh, kcat) + bias
        sink = jnp.concatenate(
            [jnp.full((blk, 1), sink_ref[h * g + j] * LOG2_E, F32) for j in range(g)], axis=0)
        m = jnp.maximum(jnp.max(s, axis=-1, keepdims=True), sink)
        p = jnp.exp2(s - m)
        den = jnp.sum(p, axis=-1, keepdims=True) + jnp.exp2(sink - m)
        o = jnp.dot(p.astype(BF16), vcat, preferred_element_type=F32) * (1.0 / den)
        for cidx in range(g // 2):
            o_even = o[(2 * cidx) * blk:(2 * cidx + 1) * blk]
            o_odd = o[(2 * cidx + 1) * blk:(2 * cidx + 2) * blk]
            col = (h * (g // 2) + cidx) * LANES
            o_ref[smp, :, col:col + LANES] = jnp.where(low, o_even, o_odd).astype(BF16)


def _win_attn(sink, q, kd, vd, n_lat, n_ctx):
    bsz, t, _ = q.shape
    blk = A_BLOCK
    nb = t // blk
    nlb = n_lat // blk
    kvw = 2 * LANES
    spb = SAMPLES_PER_STEP
    prev = pl.BlockSpec((spb, blk, kvw), lambda b, i: (b, jnp.maximum(i - 1, 0), 0))
    cur = pl.BlockSpec((spb, blk, kvw), lambda b, i: (b, i, 0))
    nxt = pl.BlockSpec((spb, blk, kvw), lambda b, i: (b, jnp.minimum(i + 1, nb - 1), 0))
    ctx = pl.BlockSpec((spb, n_ctx, kvw), lambda b, i: (b, t // n_ctx - 1, 0))
    r = jnp.arange(blk)[:, None]
    c = jnp.arange(blk)[None, :]
    yes = jnp.ones((blk, blk), jnp.bool_)
    no = jnp.zeros((blk, blk), jnp.bool_)
    ctx_ok = jnp.ones((blk, n_ctx), jnp.bool_)
    kinds = [(c >= r, yes, c <= r), (no, yes, c <= r), (c >= r, yes, no), (no, no, no)]
    bias = jnp.stack([jnp.where(jnp.concatenate([p, m, n, ctx_ok], axis=1), 0.0, NEG_INF)
                      for p, m, n in kinds]).astype(F32)
    kind = lambda i: jnp.where(i >= nlb, 3, jnp.where(i == 0, 1, jnp.where(i == nlb - 1, 2, 0)))
    return pl.pallas_call(
        _win_attn_kernel,
        grid=(bsz // spb, nb),
        in_specs=[pl.BlockSpec(memory_space=pltpu.SMEM),
                  pl.BlockSpec((1, blk, 3 * blk + n_ctx), lambda b, i: (kind(i), 0, 0)),
                  pl.BlockSpec((spb, blk, A_Q_W), lambda b, i: (b, i, 0)),
                  prev, cur, nxt, ctx, prev, cur, nxt, ctx],
        out_specs=pl.BlockSpec((spb, blk, A_Q_W), lambda b, i: (b, i, 0)),
        out_shape=jax.ShapeDtypeStruct((bsz, t, A_Q_W), BF16),
        compiler_params=_cparams(("parallel", "arbitrary")),
        name="window_attention",
    )(sink, bias, q, kd, kd, kd, kd, vd, vd, vd, vd)


def _out_epilogue(tm, s, y, x_ref, mod_ref, g2_ref, rw_ref, xo_ref, hp_ref, aff_ref):
    x = x_ref[s] + mod_ref[s, 0, 2:3, :] * y
    xo_ref[s] = x
    h2 = _rms_mod(x, g2_ref[...], mod_ref[s, 0, 3:4, :], mod_ref[s, 0, 4:5, :])
    h_hi = h2.astype(BF16)
    h_hi32 = h_hi.astype(F32)
    bits = lax.bitcast_convert_type(h_hi32, jnp.uint32)
    half = D_MODEL // 2
    packed = (bits[:, :half] >> 16) | (bits[:, half:] & jnp.uint32(0xFFFF0000))
    for j in range(HP_CHUNKS):
        hp_ref[s, pl.ds(j, tm, stride=HP_CHUNKS), :] = packed[:, j * LANES:(j + 1) * LANES]
    h_lo = (h2 - h_hi32).astype(BF16)
    logits = (_nt_dot(rw_ref[0], h_hi) + _nt_dot(rw_ref[0], h_lo)) + _nt_dot(rw_ref[1], h_hi)
    mx = jnp.max(logits, axis=0, keepdims=True)
    ex = jnp.exp(logits - mx)
    aff_ref[s] = ex / jnp.sum(ex, axis=0, keepdims=True)


def _out_even_kernel(tm, o_ref, u_ref, vn_ref, ws_ref, bs_ref, wo_ref, x_ref, mod_ref, g2_ref,
                     rw_ref, xo_ref, hp_ref, aff_ref):
    dot = functools.partial(jnp.dot, preferred_element_type=F32)
    gw = B_WIDTH // B_GROUPS
    for s in range(SAMPLES_PER_STEP):
        rows = []
        for cidx in range(tm // B_CHUNK):
            rs = slice(cidx * B_CHUNK, (cidx + 1) * B_CHUNK)
            cols = []
            for g in range(B_GROUPS):
                cs = slice(g * gw, (g + 1) * gw)
                mixed = dot(ws_ref[g], vn_ref[s, rs, cs]) + bs_ref[:, cs]
                cols.append(u_ref[s, rs, cs] * mixed)
            rows.append(jnp.concatenate(cols, axis=1))
        sg = jnp.concatenate(rows, axis=0)
        y = dot(o_ref[s], wo_ref[:A_Q_W, :]) + dot(sg.astype(BF16), wo_ref[A_Q_W:, :])
        _out_epilogue(tm, s, y, x_ref, mod_ref, g2_ref, rw_ref, xo_ref, hp_ref, aff_ref)


def _out_specs_common(bsz, t, tm):
    spb = SAMPLES_PER_STEP
    row = lambda w: pl.BlockSpec((spb, tm, w), lambda b, i: (b, i, 0))
    out_specs = [row(D_MODEL),
                 pl.BlockSpec((spb, tm * HP_CHUNKS, LANES), lambda b, i: (b, i, 0)),
                 pl.BlockSpec((spb, N_EXPERTS, tm), lambda b, i: (b, 0, i))]
    out_shape = [jax.ShapeDtypeStruct((bsz, t, D_MODEL), F32),
                 jax.ShapeDtypeStruct((bsz, t * HP_CHUNKS, LANES), jnp.uint32),
                 jax.ShapeDtypeStruct((bsz, N_EXPERTS, t), F32)]
    return out_specs, out_shape


def _out_even(o, u, vn, ws, bs, wo, x, mod, g2, rw_t, n_lat):
    bsz, t, _ = x.shape
    tm = ROW_TILE
    is_ctx = _tile_is_ctx(n_lat // tm)
    spb = SAMPLES_PER_STEP
    row = lambda w: pl.BlockSpec((spb, tm, w), lambda b, i: (b, i, 0))
    full = lambda a: pl.BlockSpec(a.shape, lambda b, i: (0,) * a.ndim)
    mod_spec = pl.BlockSpec((spb, 1, MOD_ROWS, D_MODEL), lambda b, i: (b, is_ctx(i), 0, 0))
    out_specs, out_shape = _out_specs_common(bsz, t, tm)
    return pl.pallas_call(
        functools.partial(_out_even_kernel, tm),
        grid=(bsz // spb, t // tm),
        in_specs=[row(A_Q_W), row(B_WIDTH), row(B_WIDTH), full(ws), full(bs), full(wo),
                  row(D_MODEL), mod_spec, full(g2), full(rw_t)],
        out_specs=out_specs, out_shape=out_shape,
        compiler_params=_cparams(("parallel", "arbitrary")),
        name="out_proj_even",
    )(o, u, vn, ws, bs, wo, x, mod, g2, rw_t)


def _in_odd_kernel(has_moe, tm, *refs):
    (x_ref, moe_ref, modp_ref, mod_ref, g_ref, w_ref, cos_ref, sin_ref, qg_ref, kvg_ref,
     wuq_ref, wukv_ref, xo_ref, q_ref, k_ref, v_ref, gb_ref, u_ref) = refs
    cos = cos_ref[...]
    sin = sin_ref[...]
    lane = lax.broadcasted_iota(jnp.int32, (tm, LANES), 1)
    half = C_ROPE // 2
    first = lane < C_NOPE + half
    cqo = C_KV_LORA + LANES
    co = cqo + C_Q_LORA
    scale = (C_NOPE + C_ROPE) ** -0.5 * LOG2_E
    ones_col = jnp.where(lane == C_V, 1.0, 0.0)
    voff = C_HEADS * HEAD_PAD
    for s in range(SAMPLES_PER_STEP):
        x = _residual_from_moe(x_ref[s], moe_ref, s, modp_ref[s, 0, 5:6, :], tm)
        xo_ref[s] = x
        h = _rms_mod(x, g_ref[...], mod_ref[s, 0, 0:1, :], mod_ref[s, 0, 1:2, :])
        z = jnp.dot(h.astype(BF16), w_ref[...], preferred_element_type=F32)
        ckv = _rms(z[:, :C_KV_LORA], kvg_ref[...])
        kr = _rope_pairs(z[:, C_KV_LORA:C_KV_LORA + LANES], cos, sin, half, first)
        cq = _rms(z[:, cqo:cqo + C_Q_LORA], qg_ref[...])
        kv = jnp.dot(ckv.astype(BF16), wukv_ref[...], preferred_element_type=F32)
        qq = jnp.dot(cq.astype(BF16), wuq_ref[...], preferred_element_type=F32)
        for hd in range(C_HEADS):
            sl = slice(hd * HEAD_PAD, (hd + 1) * HEAD_PAD)
            q_ref[s, :, sl] = (_rope_pairs(qq[:, sl], cos, sin, half, first) * scale).astype(BF16)
            k_ref[s, :, sl] = (kv[:, sl] + kr).astype(BF16)
            v_ref[s, :, sl] = (kv[:, voff + hd * HEAD_PAD:voff + (hd + 1) * HEAD_PAD]
                               + ones_col).astype(BF16)
        gb_ref[s] = z[:, co:co + D_WIDTH]
        u_ref[s] = z[:, co + D_WIDTH:co + 2 * D_WIDTH] * z[:, co + 2 * D_WIDTH:co + 3 * D_WIDTH]


def _in_odd(x, moe, modp, mod, g1, w_in, cos, sin, qg, kvg, wuq, wukv, n_lat):
    bsz, t, _ = x.shape
    tm = ROW_TILE
    is_ctx = _tile_is_ctx(n_lat // tm)
    spb = SAMPLES_PER_STEP
    row = lambda w: pl.BlockSpec((spb, tm, w), lambda b, i: (b, i, 0))
    mod_spec = pl.BlockSpec((spb, 1, MOD_ROWS, D_MODEL), lambda b, i: (b, is_ctx(i), 0, 0))
    full2 = lambda a: pl.BlockSpec(a.shape, lambda b, i: (0, 0))
    hw = C_HEADS * HEAD_PAD
    return pl.pallas_call(
        functools.partial(_in_odd_kernel, True, tm),
        grid=(bsz // spb, t // tm),
        in_specs=[row(D_MODEL),
                  pl.BlockSpec((spb, tm * ACC_CHUNKS, LANES), lambda b, i: (b, i, 0)),
                  mod_spec, mod_spec, full2(g1), full2(w_in),
                  pl.BlockSpec((tm, LANES), lambda b, i: (i, 0)),
                  pl.BlockSpec((tm, LANES), lambda b, i: (i, 0)),
                  full2(qg), full2(kvg), full2(wuq), full2(wukv)],
        out_specs=[row(D_MODEL), row(hw), row(hw), row(hw), row(D_WIDTH), row(D_WIDTH)],
        out_shape=[jax.ShapeDtypeStruct((bsz, t, D_MODEL), F32),
                   jax.ShapeDtypeStruct((bsz, t, hw), BF16),
                   jax.ShapeDtypeStruct((bsz, t, hw), BF16),
                   jax.ShapeDtypeStruct((bsz, t, hw), BF16),
                   jax.ShapeDtypeStruct((bsz, t, D_WIDTH), F32),
                   jax.ShapeDtypeStruct((bsz, t, D_WIDTH), F32)],
        compiler_params=_cparams(("parallel", "arbitrary")),
        name="in_proj_odd",
    )(x, moe, modp, mod, g1, w_in, cos, sin, qg, kvg, wuq, wukv)


def _dense_attn_kernel(tk, q_ref, k_ref, v_ref, o_ref):
    q = q_ref[0]
    tq = q.shape[0]
    nk = k_ref.shape[1] // tk

    m = jnp.full((tq, 1), NEG_INF, F32)
    acc = jnp.zeros((tq, HEAD_PAD), F32)
    for j in range(nk):
        s = _nt_dot(q, k_ref[0, j * tk:(j + 1) * tk, :])
        m_new = jnp.maximum(m, jnp.max(s, axis=-1, keepdims=True))
        alpha = jnp.exp2(m - m_new)
        p = jnp.exp2(s - m_new)
        acc = alpha * acc + jnp.dot(p.astype(BF16), v_ref[0, j * tk:(j + 1) * tk, :],
                                    preferred_element_type=F32)
        m = m_new
    o_ref[0] = (acc / acc[:, C_V:C_V + 1]).astype(BF16)


def _dense_attn(q, k, v, q_start, q_len, k_start, k_len, tq):
    bsz, _, hw = q.shape
    tk = ROW_TILE
    q0 = q_start // tq
    kb = k_start // k_len
    return pl.pallas_call(
        functools.partial(_dense_attn_kernel, tk),
        grid=(bsz, C_HEADS, q_len // tq),
        in_specs=[pl.BlockSpec((1, tq, HEAD_PAD), lambda b, h, i: (b, q0 + i, h)),
                  pl.BlockSpec((1, k_len, HEAD_PAD), lambda b, h, i: (b, kb, h)),
                  pl.BlockSpec((1, k_len, HEAD_PAD), lambda b, h, i: (b, kb, h))],
        out_specs=pl.BlockSpec((1, tq, HEAD_PAD), lambda b, h, i: (b, i, h)),
        out_shape=jax.ShapeDtypeStruct((bsz, q_len, hw), BF16),
        compiler_params=_cparams(("parallel", "parallel", "arbitrary")),
        name="dense_attention",
    )(q, k, v)


def _out_odd_kernel(tm, n_lat_tiles, n_tiles, ol_ref, oc_ref, gb_ref, u_ref, up_ref, un_ref, cw_ref,
                    wo_ref, x_ref, mod_ref, g2_ref, rw_ref, xo_ref, hp_ref, aff_ref):
    i = pl.program_id(1)
    dot = functools.partial(jnp.dot, preferred_element_type=F32)
    has_prev = jnp.logical_and(i != 0, i != n_lat_tiles)
    has_next = jnp.logical_and(i != n_lat_tiles - 1, i != n_tiles - 1)
    hw = C_HEADS * HEAD_PAD
    for s in range(SAMPLES_PER_STEP):
        o_att = jnp.where(i < n_lat_tiles, ol_ref[s], oc_ref[s])
        u = u_ref[s]
        prev_row = jnp.where(has_prev, up_ref[s, SUBLANES - 1:SUBLANES, :], 0.0)
        next_row = jnp.where(has_next, un_ref[s, 0:1, :], 0.0)
        ridx = lax.broadcasted_iota(jnp.int32, u.shape, 0)
        u_m1 = jnp.where(ridx == 0, prev_row, pltpu.roll(u, 1, 0))
        u_p1 = jnp.where(ridx == tm - 1, next_row, pltpu.roll(u, tm - 1, 0))
        conv = u_m1 * cw_ref[0:1, :] + u * cw_ref[1:2, :] + u_p1 * cw_ref[2:3, :]
        c = gb_ref[s] * conv
        y = dot(o_att, wo_ref[:hw, :]) + dot(c.astype(BF16), wo_ref[hw:, :])
        _out_epilogue(tm, s, y, x_ref, mod_ref, g2_ref, rw_ref, xo_ref, hp_ref, aff_ref)


def _out_odd(o_lat, o_ctx, gb, u, cw, wo, x, mod, g2, rw_t, n_lat):
    bsz, t, _ = x.shape
    tm = ROW_TILE
    nt = t // tm
    nlt = n_lat // tm
    is_ctx = _tile_is_ctx(nlt)
    spb = SAMPLES_PER_STEP
    row = lambda w: pl.BlockSpec((spb, tm, w), lambda b, i: (b, i, 0))
    full = lambda a: pl.BlockSpec(a.shape, lambda b, i: (0,) * a.ndim)
    mod_spec = pl.BlockSpec((spb, 1, MOD_ROWS, D_MODEL), lambda b, i: (b, is_ctx(i), 0, 0))
    hw = C_HEADS * HEAD_PAD
    per = tm // SUBLANES
    last = t // SUBLANES - 1
    halo_prev = pl.BlockSpec((spb, SUBLANES, D_WIDTH), lambda b, i: (b, jnp.maximum(i * per - 1, 0), 0))
    halo_next = pl.BlockSpec((spb, SUBLANES, D_WIDTH), lambda b, i: (b, jnp.minimum((i + 1) * per, last), 0))
    out_specs, out_shape = _out_specs_common(bsz, t, tm)
    return pl.pallas_call(
        functools.partial(_out_odd_kernel, tm, nlt, nt),
        grid=(bsz // spb, nt),
        in_specs=[pl.BlockSpec((spb, tm, hw), lambda b, i: (b, jnp.minimum(i, nlt - 1), 0)),
                  pl.BlockSpec((spb, tm, hw), lambda b, i: (b, jnp.maximum(i - nlt, 0), 0)),
                  row(D_WIDTH), row(D_WIDTH), halo_prev, halo_next,
                  full(cw), full(wo), row(D_MODEL), mod_spec, full(g2), full(rw_t)],
        out_specs=out_specs, out_shape=out_shape,
        compiler_params=_cparams(("parallel", "arbitrary")),
        name="out_proj_odd",
    )(o_lat, o_ctx, gb, u, u, u, cw, wo, x, mod, g2, rw_t)


def _route_kernel(cap, nblk, width, a_ref, idx_ref, gate_ref):
    n_e = N_EXPERTS
    a = a_ref[0]
    bits = lax.bitcast_convert_type(a, jnp.int32)

    def count(mask):
        c = jnp.sum(jnp.where(mask, 1.0, 0.0), axis=2, keepdims=True)
        return jnp.sum(c, axis=1, keepdims=True)

    def search(it, thr):
        cand = thr | jnp.left_shift(jnp.int32(1), 30 - it)
        return jnp.where(count(bits >= cand) >= cap, cand, thr)

    thr = lax.fori_loop(0, 31, search, jnp.zeros((n_e, 1, 1), jnp.int32))
    gt = bits > thr
    eq = bits == thr
    need = cap - count(gt)

    rows = n_e * nblk
    li = lax.broadcasted_iota(jnp.int32, (width, width), 0)
    lj = lax.broadcasted_iota(jnp.int32, (width, width), 1)
    upper = jnp.where(li <= lj, 1.0, 0.0).astype(BF16)
    ones_sq = jnp.ones((width, width), BF16)
    if nblk > 1:
        ri = lax.broadcasted_iota(jnp.int32, (rows, rows), 0)
        rj = lax.broadcasted_iota(jnp.int32, (rows, rows), 1)
        lower = jnp.where(jnp.logical_and(ri // nblk == rj // nblk, rj < ri), 1.0, 0.0).astype(BF16)

    def prefix(mask3):
        m2 = jnp.where(mask3, 1.0, 0.0).astype(BF16).reshape(rows, width)
        local = jnp.dot(m2, upper, preferred_element_type=F32)
        total = jnp.dot(m2, ones_sq, preferred_element_type=F32)
        if nblk > 1:
            excl = jnp.dot(lower, total.astype(BF16), preferred_element_type=F32)
        else:
            excl = jnp.zeros_like(total)
        return local, total, excl

    l_eq, _, x_eq = prefix(eq)
    tie_rank = (l_eq + x_eq).reshape(n_e, nblk, width)
    sel = jnp.logical_or(gt, jnp.logical_and(eq, tie_rank <= need))
    local, total, excl = prefix(sel)
    pin = local + excl

    slot_l = lax.broadcasted_iota(jnp.int32, (nblk, cap), 1).astype(F32)
    blk_s = lax.broadcasted_iota(jnp.int32, (nblk, cap), 0).astype(F32)
    slot_s = lax.broadcasted_iota(jnp.int32, (cap, width), 0).astype(F32)
    lane_w = lax.broadcasted_iota(jnp.int32, (cap, width), 1).astype(F32)
    ones_r = jnp.ones((SUBLANES, width), BF16)
    reps = cap // width if cap >= width else 1
    a2 = a.reshape(rows, width)
    if nblk > 1:
        pin_hi = jnp.floor(pin * (1.0 / PIN_SPLIT))
        a_1 = a2.astype(BF16)
        rem = a2 - a_1.astype(F32)
        a_2 = rem.astype(BF16)
        a_3 = (rem - a_2.astype(F32)).astype(BF16)
        table = jnp.concatenate([pin_hi.astype(BF16), (pin - PIN_SPLIT * pin_hi).astype(BF16),
                                 a_1, a_2, a_3], axis=1)
    for e in range(n_e):
        rs = slice(e * nblk, (e + 1) * nblk)
        if nblk > 1:
            lo = jnp.concatenate([excl[rs]] * reps, axis=1)[:, :cap]
            hi = lo + jnp.concatenate([total[rs]] * reps, axis=1)[:, :cap]
            oh_t = jnp.where(jnp.logical_and(lo <= slot_l, slot_l < hi), 1.0, 0.0)
            got = _tn_dot(oh_t.astype(BF16), table[rs])
            pin_g = PIN_SPLIT * got[:, :width] + got[:, width:2 * width]
            aff_g = (got[:, 2 * width:3 * width] + got[:, 3 * width:4 * width]) + got[:, 4 * width:]
            blk_row = jnp.sum(oh_t * blk_s, axis=0, keepdims=True)
        else:
            pin_g = jnp.broadcast_to(pin[rs], (cap, width))
            aff_g = jnp.broadcast_to(a2[rs], (cap, width))
            blk_row = jnp.zeros((1, cap), F32)
        ind = jnp.where(pin_g <= slot_s, 1.0, 0.0).astype(BF16)
        cnt_row = _nt_dot(ones_r, ind)[0:1]
        cnt_b = jnp.dot(ind, ones_sq, preferred_element_type=F32)
        picked = jnp.where(lane_w == cnt_b, aff_g, 0.0)
        idx_ref[0, e] = (blk_row * width + cnt_row).astype(jnp.int32)
        gate_ref[0, e] = jnp.sum(picked, axis=1, keepdims=True)


def _route(aff, cap):
    bsz, n_e, n = aff.shape
    width = LANES if n % (LANES * SUBLANES) == 0 else n
    nblk = n // width
    a4 = aff.reshape(bsz, n_e, nblk, width)
    idx, gate = pl.pallas_call(
        functools.partial(_route_kernel, cap, nblk, width),
        grid=(bsz,),
        in_specs=[pl.BlockSpec((1, n_e, nblk, width), lambda b: (b, 0, 0, 0))],
        out_specs=[pl.BlockSpec((1, n_e, 1, cap), lambda b: (b, 0, 0, 0)),
                   pl.BlockSpec((1, n_e, cap, 1), lambda b: (b, 0, 0, 0))],
        out_shape=[jax.ShapeDtypeStruct((bsz, n_e, 1, cap), jnp.int32),
                   jax.ShapeDtypeStruct((bsz, n_e, cap, 1), F32)],
        compiler_params=_cparams(("parallel",)),
        name="expert_choice_routing",
    )(a4)
    return idx.reshape(bsz, n_e, cap), gate.reshape(bsz, n_e, cap)


def _ffn_copies(hp_hbm, out_hbm, hsrc, acc, sem_in, sem_out, grp, nsub, n_per, row0):
    ins = []
    outs = []
    for sb in range(nsub):
        b = grp * nsub + sb
        ins.append(pltpu.make_async_copy(
            hp_hbm.at[b, pl.ds(row0 * HP_CHUNKS, n_per * HP_CHUNKS), :],
            hsrc.at[pl.ds(sb * n_per * HP_CHUNKS, n_per * HP_CHUNKS), :], sem_in.at[sb]))
        outs.append(pltpu.make_async_copy(
            acc.at[pl.ds(sb * n_per * ACC_CHUNKS, n_per * ACC_CHUNKS), :],
            out_hbm.at[b, pl.ds(row0 * ACC_CHUNKS, n_per * ACC_CHUNKS), :], sem_out.at[sb]))
    return ins, outs


def _ffn_gather(idx_ref, hsrc, xt, stride, mi):
    src = pl.multiple_of(idx_ref[0, 0, 0, mi] * HP_CHUNKS, HP_CHUNKS)
    xt[pl.ds(mi, HP_CHUNKS, stride=stride), :] = hsrc[pl.ds(src, HP_CHUNKS), :]


def _ffn_scatter(idx_ref, gate_ref, gate_on, acc, yt, stride, mis):
    dst = []
    val = []
    for mi in mis:
        d = pl.multiple_of(idx_ref[0, 0, 0, mi] * ACC_CHUNKS, ACC_CHUNKS)
        gate = gate_ref[0, 0, 0, mi]
        if gate_on is not None:
            gate = jnp.where(gate_on, gate, 0.0)
        slab = yt[pl.ds(mi, ACC_CHUNKS, stride=stride), :] * gate
        dst.append(d)
        val.append(acc[pl.ds(d, ACC_CHUNKS), :] + slab)
    for d, v in zip(dst, val):
        acc[pl.ds(d, ACC_CHUNKS), :] = v


def _ffn_kernel(m_slots, nsub, n_per, row0, unroll, idx_ref, idxn_ref, idxp_ref, gate_ref, gatep_ref,
                hp_hbm, w1_ref, w3_ref, w2_ref, _moe_in, out_hbm, hsrc, acc, xt, xs_buf, act_buf,
                yt, sem_in, sem_out):
    grp = pl.program_id(0)
    e = pl.program_id(1)
    stride = m_slots + SUBLANES
    ins, outs = _ffn_copies(hp_hbm, out_hbm, hsrc, acc, sem_in, sem_out, grp, nsub, n_per, row0)

    @pl.when(jnp.logical_and(grp == 0, e == 0))
    def _():
        yt[...] = jnp.zeros(yt.shape, F32)

    @pl.when(e == 0)
    def _():
        for cp in ins:
            cp.start()
        acc[...] = jnp.zeros(acc.shape, F32)
        for cp in ins:
            cp.wait()

        def first_gather(c, carry):
            for k in range(unroll):
                _ffn_gather(idx_ref, hsrc, xt, stride, c * unroll + k)
            return carry

        lax.fori_loop(0, m_slots // unroll, first_gather, 0)

    half = D_MODEL // 2
    for j in range(HP_CHUNKS):
        w = xt[pl.ds(j * stride, m_slots), :]
        xs_buf[:, j * LANES:(j + 1) * LANES] = lax.bitcast_convert_type(w << 16, F32).astype(BF16)
        xs_buf[:, half + j * LANES:half + (j + 1) * LANES] = lax.bitcast_convert_type(
            w & jnp.uint32(0xFFFF0000), F32).astype(BF16)

    for c in range(m_slots // unroll):
        _ffn_scatter(idxp_ref, gatep_ref, e > 0, acc, yt, stride, range(c * unroll, (c + 1) * unroll))
    for mi in range(m_slots):
        _ffn_gather(idxn_ref, hsrc, xt, stride, mi)

    dot = functools.partial(jnp.dot, preferred_element_type=F32)
    xs = xs_buf[...]
    a = dot(xs, w1_ref[0, 0])
    b = dot(xs, w3_ref[0, 0])
    act_buf[...] = (a * jax.nn.sigmoid(a) * b).astype(BF16)
    seen = xt[pl.ds(0, SUBLANES), :]
    for r in range(SUBLANES, m_slots, SUBLANES):
        seen = seen | xt[pl.ds(r, SUBLANES), :]
    seen = seen | lax.bitcast_convert_type(acc[pl.ds(0, SUBLANES), :], jnp.uint32)
    zero = lax.bitcast_convert_type(seen >> 32, F32)
    tile = (2 * SUBLANES, LANES)
    act_buf[:tile[0], :tile[1]] = (act_buf[:tile[0], :tile[1]].astype(F32)
                                   + jnp.concatenate([zero, zero], axis=0)).astype(BF16)
    y = dot(act_buf[...], w2_ref[0, 0])
    for j in range(ACC_CHUNKS):
        yt[pl.ds(j * stride, m_slots), :] = y[:, j * LANES:(j + 1) * LANES]

    @pl.when(e == N_EXPERTS - 1)
    def _():
        def last_scatter(c, carry):
            _ffn_scatter(idx_ref, gate_ref, None, acc, yt, stride,
                         [c * unroll + k for k in range(unroll)])
            return carry

        lax.fori_loop(0, m_slots // unroll, last_scatter, 0)
        for cp in outs:
            cp.start()
        for cp in outs:
            cp.wait()


def _expert_ffn(idx, gate, hp, w1, w3, w2, layer, moe_buf, nsub, n_per, row0):
    groups, n_e, m_slots = idx.shape
    n_tok = nsub * n_per
    stride = m_slots + SUBLANES
    unroll = 8
    cur = lambda g, e: (g, e, 0, 0)
    nxt = lambda g, e: (g, jnp.minimum(e + 1, n_e - 1), 0, 0)
    prv = lambda g, e: (g, jnp.maximum(e - 1, 0), 0, 0)
    smem = lambda imap: pl.BlockSpec((1, 1, 1, m_slots), imap, memory_space=pltpu.SMEM)
    wspec = lambda: pl.BlockSpec((1, 1, D_MODEL, EXPERT_FF), lambda g, e: (layer, e, 0, 0))
    idx4 = idx.reshape(groups, n_e, 1, m_slots)
    gate4 = gate.reshape(groups, n_e, 1, m_slots)
    return pl.pallas_call(
        functools.partial(_ffn_kernel, m_slots, nsub, n_per, row0, unroll),
        grid=(groups, n_e),
        in_specs=[smem(cur), smem(nxt), smem(prv), smem(cur), smem(prv),
                  pl.BlockSpec(memory_space=pl.ANY), wspec(), wspec(),
                  pl.BlockSpec((1, 1, EXPERT_FF, D_MODEL), lambda g, e: (layer, e, 0, 0)),
                  pl.BlockSpec(memory_space=pl.ANY)],
        out_specs=pl.BlockSpec(memory_space=pl.ANY),
        out_shape=jax.ShapeDtypeStruct(moe_buf.shape, F32),
        scratch_shapes=[pltpu.VMEM((n_tok * HP_CHUNKS, LANES), jnp.uint32),
                        pltpu.VMEM((n_tok * ACC_CHUNKS, LANES), F32),
                        pltpu.VMEM((HP_CHUNKS * stride, LANES), jnp.uint32),
                        pltpu.VMEM((m_slots, D_MODEL), BF16),
                        pltpu.VMEM((m_slots, EXPERT_FF), BF16),
                        pltpu.VMEM((ACC_CHUNKS * stride, LANES), F32),
                        pltpu.SemaphoreType.DMA((nsub,)),
                        pltpu.SemaphoreType.DMA((nsub,))],
        input_output_aliases={9: 0},
        compiler_params=_cparams(("arbitrary", "arbitrary")),
        name="expert_ffn",
    )(idx4, idx4, idx4, gate4, gate4, hp, w1, w3, w2, moe_buf)


def _final_kernel(tm, x_ref, moe_ref, modp_ref, g_ref, o_ref):
    x = _residual_from_moe(x_ref[0], moe_ref, 0, modp_ref[0, 0, 5:6, :], tm)
    o_ref[0] = _rms(x, g_ref[...])


def _final(x, moe, modp, g, n_lat):
    bsz = x.shape[0]
    tm = ROW_TILE
    return pl.pallas_call(
        functools.partial(_final_kernel, tm),
        grid=(bsz, n_lat // tm),
        in_specs=[pl.BlockSpec((1, tm, D_MODEL), lambda b, i: (b, i, 0)),
                  pl.BlockSpec((1, tm * ACC_CHUNKS, LANES), lambda b, i: (b, i, 0)),
                  pl.BlockSpec((1, 1, MOD_ROWS, D_MODEL), lambda b, i: (b, 0, 0, 0)),
                  pl.BlockSpec(g.shape, lambda b, i: (0, 0))],
        out_specs=pl.BlockSpec((1, tm, D_MODEL), lambda b, i: (b, i, 0)),
        out_shape=jax.ShapeDtypeStruct((bsz, n_lat, D_MODEL), F32),
        compiler_params=_cparams(("parallel", "arbitrary")),
        name="final_norm",
    )(x, moe, modp, g)


def _rope_angles(rows, rot_dim):
    row = jnp.repeat(jnp.arange(rows, dtype=F32), GRID_W)
    col = jnp.tile(jnp.arange(GRID_W, dtype=F32), rows)
    n_freq = rot_dim // 4
    inv_freq = ROPE_BASE ** (-jnp.arange(n_freq, dtype=F32) / n_freq)
    ang = jnp.concatenate([row[:, None] * inv_freq[None, :], col[:, None] * inv_freq[None, :]], axis=-1)
    return jnp.cos(ang), jnp.sin(ang)


def _rope_tables_even(n_lat, n_ctx):
    cos, sin = _rope_angles(n_lat // GRID_W, A_HEAD_DIM)
    cos_h = jnp.concatenate([cos, cos], axis=1)
    sin_h = jnp.concatenate([-sin, sin], axis=1)
    reps = LANES // A_HEAD_DIM
    cos_l = jnp.tile(cos_h, (1, reps))
    sin_l = jnp.tile(sin_h, (1, reps))
    return (jnp.concatenate([cos_l, jnp.ones((n_ctx, LANES), F32)], axis=0),
            jnp.concatenate([sin_l, jnp.zeros((n_ctx, LANES), F32)], axis=0))


def _rope_tables_odd(n_lat, n_ctx):
    cos, sin = _rope_angles(n_lat // GRID_W, C_ROPE)
    n = cos.shape[0]
    pad = LANES - C_NOPE - C_ROPE
    cos_l = jnp.concatenate([jnp.ones((n, C_NOPE), F32), cos, cos, jnp.ones((n, pad), F32)], axis=1)
    sin_l = jnp.concatenate([jnp.zeros((n, C_NOPE), F32), -sin, sin, jnp.zeros((n, pad), F32)], axis=1)
    return (jnp.concatenate([cos_l, jnp.ones((n_ctx, LANES), F32)], axis=0),
            jnp.concatenate([sin_l, jnp.zeros((n_ctx, LANES), F32)], axis=0))


def _even_w_in(w):
    k = w[:, :A_KV_W]
    v = w[:, A_KV_W:2 * A_KV_W]
    dup = lambda a: jnp.concatenate(
        [a[:, h * A_HEAD_DIM:(h + 1) * A_HEAD_DIM] for h in range(A_KV_HEADS) for _ in range(2)], axis=1)
    return jnp.concatenate([dup(k), dup(v), w[:, 2 * A_KV_W:]], axis=1).astype(BF16)


def _odd_w_in(w):
    d = w.shape[0]
    kr = w[:, C_KV_LORA:C_KV_LORA + C_ROPE]
    kr_group = jnp.concatenate(
        [jnp.zeros((d, C_NOPE), F32), kr, jnp.zeros((d, LANES - C_NOPE - C_ROPE), F32)], axis=1)
    return jnp.concatenate([w[:, :C_KV_LORA], kr_group, w[:, C_KV_LORA + C_ROPE:]], axis=1).astype(BF16)


def _odd_w_uq(w):
    r = w.shape[0]
    w3 = w.reshape(r, C_HEADS, C_NOPE + C_ROPE)
    pad = jnp.zeros((r, C_HEADS, HEAD_PAD - C_NOPE - C_ROPE), F32)
    return jnp.concatenate([w3, pad], axis=2).reshape(r, C_HEADS * HEAD_PAD).astype(BF16)


def _odd_w_ukv(w):
    r = w.shape[0]
    w3 = w.reshape(r, C_HEADS, C_NOPE + C_V)
    kpad = jnp.zeros((r, C_HEADS, HEAD_PAD - C_NOPE), F32)
    vpad = jnp.zeros((r, C_HEADS, HEAD_PAD - C_V), F32)
    kpart = jnp.concatenate([w3[:, :, :C_NOPE], kpad], axis=2).reshape(r, C_HEADS * HEAD_PAD)
    vpart = jnp.concatenate([w3[:, :, C_NOPE:], vpad], axis=2).reshape(r, C_HEADS * HEAD_PAD)
    return jnp.concatenate([kpart, vpart], axis=1).astype(BF16)


def _odd_w_out(w):
    d = w.shape[1]
    att = w[:C_HEADS * C_V].reshape(C_HEADS, C_V, d)
    att = jnp.concatenate([att, jnp.zeros((C_HEADS, HEAD_PAD - C_V, d), F32)], axis=1)
    return jnp.concatenate([att.reshape(C_HEADS * HEAD_PAD, d), w[C_HEADS * C_V:]], axis=0).astype(BF16)


def _mod_table(mods_l, bsz):
    lat = mods_l[:bsz].reshape(bsz, 1, 6, D_MODEL)
    ctx = jnp.broadcast_to(mods_l[bsz].reshape(1, 1, 6, D_MODEL), (bsz, 1, 6, D_MODEL))
    tab = jnp.concatenate([lat, ctx], axis=1)
    return jnp.pad(tab, ((0, 0), (0, 0), (0, MOD_ROWS - 6), (0, 0)))


def kernel(x, c, ctx, c_ctx, mod_w, mod_b, norm1_g, norm2_g, ev_w_in, ev_sink, ev_sgu_norm_g, ev_sgu_w, ev_sgu_b, ev_w_out, od_w_in, od_q_norm_g, od_w_uq, od_kv_norm_g, od_w_ukv, od_conv_w, od_w_out, router_w, exp_w1, exp_w3, exp_w2, final_g):
    bsz, n_lat, _ = x.shape
    n_ctx = ctx.shape[1]
    t = n_lat + n_ctx
    cap_lat = EC_FACTOR * n_lat // N_EXPERTS
    cap_ctx = EC_FACTOR * n_ctx // N_EXPERTS

    mod_rows = -(-(bsz + 1) // SUBLANES) * SUBLANES
    cc = jnp.concatenate([c, c_ctx[None, :], jnp.zeros((mod_rows - bsz - 1, D_MODEL), F32)], axis=0)
    mods = _modulation(cc, mod_w, mod_b)
    tabs = [_mod_table(mods[l], bsz) for l in range(DEPTH)]

    cos_e, sin_e = _rope_tables_even(n_lat, n_ctx)
    cos_o, sin_o = _rope_tables_odd(n_lat, n_ctx)

    xs = jnp.concatenate([x, ctx], axis=1)
    w1 = exp_w1.astype(BF16)
    w3 = exp_w3.astype(BF16)
    w2 = exp_w2.astype(BF16)
    moe = None
    for layer in range(DEPTH):
        i = layer // 2
        need_ctx = layer < DEPTH - 1
        g1 = norm1_g[layer][None, :]
        g2 = norm2_g[layer][None, :]
        rw_f = router_w[layer].T
        rw_hi = rw_f.astype(BF16)
        rw_t = jnp.stack([rw_hi, (rw_f - rw_hi.astype(F32)).astype(BF16)])
        modp = tabs[layer - 1] if layer > 0 else None
        if layer % 2 == 0:
            xs, q, kd, vd, u, vn = _in_even(xs, moe, modp, tabs[layer], g1, _even_w_in(ev_w_in[i]),
                                            cos_e, sin_e, ev_sgu_norm_g[i][None, :], n_lat)
            o = _win_attn(ev_sink[i], q, kd, vd, n_lat, n_ctx)
            bs = jnp.repeat(ev_sgu_b[i].T, B_WIDTH // B_GROUPS, axis=1)
            xs, hp, aff = _out_even(o, u, vn, ev_sgu_w[i].astype(BF16), bs, ev_w_out[i].astype(BF16),
                                    xs, tabs[layer], g2, rw_t, n_lat)
        else:
            xs, q, k, v, gb, u = _in_odd(xs, moe, modp, tabs[layer], g1, _odd_w_in(od_w_in[i]),
                                         cos_o, sin_o, od_q_norm_g[i][None, :], od_kv_norm_g[i][None, :],
                                         _odd_w_uq(od_w_uq[i]), _odd_w_ukv(od_w_ukv[i]), n_lat)
            o_lat = _dense_attn(q, k, v, 0, n_lat, 0, t, 2 * ROW_TILE)
            o_ctx = _dense_attn(q, k, v, n_lat, n_ctx, n_lat, n_ctx, n_ctx)
            xs, hp, aff = _out_odd(o_lat, o_ctx, gb, u, od_conv_w[i], _odd_w_out(od_w_out[i]), xs, tabs[layer],
                                   g2, rw_t, n_lat)
        if moe is None:
            moe = jnp.zeros((bsz, t * ACC_CHUNKS, LANES), F32)
        idx, gate = _route(aff[:, :, :n_lat], cap_lat)
        moe = _expert_ffn(idx, gate, hp, w1, w3, w2, layer, moe, 1, n_lat, 0)
        if need_ctx:
            idx_c, gate_c = _route(aff[:, :, n_lat:], cap_ctx)
            idx_c = idx_c + (jnp.arange(bsz, dtype=jnp.int32) * n_ctx)[:, None, None]
            idx_c = jnp.transpose(idx_c, (1, 0, 2)).reshape(1, N_EXPERTS, bsz * cap_ctx)
            gate_c = jnp.transpose(gate_c, (1, 0, 2)).reshape(1, N_EXPERTS, bsz * cap_ctx)
            moe = _expert_ffn(idx_c, gate_c, hp, w1, w3, w2, layer, moe, bsz, n_ctx, n_lat)
    return _final(xs, moe, tabs[DEPTH - 1], final_g[None, :], n_lat)
```

```python
import functools

import jax
import jax.numpy as jnp
from jax import lax
from jax.experimental import pallas as pl
from jax.experimental.pallas import tpu as pltpu

F32 = jnp.float32
BF16 = jnp.bfloat16
HIGHEST = lax.Precision.HIGHEST

D_MODEL = 1024
DEPTH = 4
GRID_W = 64
NORM_EPS = 1e-6
ROPE_BASE = 10000.0
NEG_INF = -1e30
LOG2_E = 1.4426950408889634
PIN_SPLIT = 64.0

A_HEADS = 8
A_KV_HEADS = 2
A_HEAD_DIM = 64
WINDOW = 128
A_BLOCK = 128
B_WIDTH = 512
B_GROUPS = 4
B_CHUNK = 128
C_HEADS = 8
C_Q_LORA = 384
C_KV_LORA = 256
C_NOPE = 64
C_ROPE = 32
C_V = 64
D_WIDTH = 512
D_CONV = 3
N_EXPERTS = 16
EXPERT_FF = 1024
EC_FACTOR = 2

A_Q_W = A_HEADS * A_HEAD_DIM
A_KV_W = A_KV_HEADS * A_HEAD_DIM

LANES = 128
SUBLANES = 8
ROW_TILE = 256
SAMPLES_PER_STEP = 2
MOD_ROWS = 8
HP_CHUNKS = D_MODEL // (2 * LANES)
ACC_CHUNKS = D_MODEL // LANES
VMEM_LIMIT = 56 * 1024 * 1024

EVEN_COLS = 2 * A_KV_W + 2 * A_KV_W + A_Q_W + 2 * B_WIDTH
ODD_COLS = C_KV_LORA + LANES + C_Q_LORA + 3 * D_WIDTH
HEAD_PAD = LANES


def _cparams(sem):
    return pltpu.CompilerParams(dimension_semantics=sem, vmem_limit_bytes=VMEM_LIMIT)


def _nt_dot(a, b, precision=None):
    return lax.dot_general(a, b, (((1,), (1,)), ((), ())), precision=precision,
                           preferred_element_type=F32)


def _tn_dot(a, b, precision=None):
    return lax.dot_general(a, b, (((0,), (0,)), ((), ())), precision=precision,
                           preferred_element_type=F32)


def _mod_kernel(cc_ref, w_ref, b_ref, o_ref):
    cc = cc_ref[...]
    s = cc * jax.nn.sigmoid(cc)
    o_ref[0] = jnp.dot(s, w_ref[0], precision=HIGHEST, preferred_element_type=F32) + b_ref[0]


def _modulation(cc, mod_w, mod_b):
    rows = cc.shape[0]
    ncol = mod_w.shape[2] // D_MODEL
    return pl.pallas_call(
        _mod_kernel,
        grid=(DEPTH, ncol),
        in_specs=[
            pl.BlockSpec((rows, D_MODEL), lambda l, j: (0, 0)),
            pl.BlockSpec((1, D_MODEL, D_MODEL), lambda l, j: (l, 0, j)),
            pl.BlockSpec((1, 1, D_MODEL), lambda l, j: (l, 0, j)),
        ],
        out_specs=pl.BlockSpec((1, rows, D_MODEL), lambda l, j: (l, 0, j)),
        out_shape=jax.ShapeDtypeStruct((DEPTH, rows, mod_w.shape[2]), F32),
        compiler_params=_cparams(("arbitrary", "arbitrary")),
        name="adaln_modulation",
    )(cc, mod_w, mod_b.reshape(DEPTH, 1, -1))


def _residual_from_moe(x, moe_ref, s, gate_row, tm):
    cols = []
    for j in range(ACC_CHUNKS):
        chunk = moe_ref[s, pl.ds(j, tm, stride=ACC_CHUNKS), :]
        sl = slice(j * LANES, (j + 1) * LANES)
        cols.append(x[:, sl] + gate_row[:, sl] * chunk)
    return jnp.concatenate(cols, axis=1)


def _rms_mod(x, g, shift, scale):
    gain = g * (1.0 + scale)
    return x * lax.rsqrt(jnp.mean(x * x, axis=-1, keepdims=True) + NORM_EPS) * gain + shift


def _rms(x, g):
    return x * lax.rsqrt(jnp.mean(x * x, axis=-1, keepdims=True) + NORM_EPS) * g


def _rope_pairs(a, cos, sin, half, first):
    rot = jnp.where(first, pltpu.roll(a, LANES - half, 1), pltpu.roll(a, half, 1))
    return a * cos + rot * sin


def _in_even_kernel(has_moe, tm, *refs):
    if has_moe:
        (x_ref, moe_ref, modp_ref, mod_ref, g_ref, w_ref, cos_ref, sin_ref, lng_ref,
         xo_ref, q_ref, kd_ref, vd_ref, u_ref, vn_ref) = refs
    else:
        (x_ref, mod_ref, g_ref, w_ref, cos_ref, sin_ref, lng_ref,
         q_ref, kd_ref, vd_ref, u_ref, vn_ref) = refs
    cos = cos_ref[...]
    sin = sin_ref[...]
    lane = lax.broadcasted_iota(jnp.int32, (tm, LANES), 1)
    first = (lane % A_HEAD_DIM) < (A_HEAD_DIM // 2)
    half = A_HEAD_DIM // 2
    qoff = 4 * LANES
    uoff = qoff + A_Q_W
    scale = A_HEAD_DIM ** -0.5 * LOG2_E
    for s in range(x_ref.shape[0]):
        x = x_ref[s]
        if has_moe:
            x = _residual_from_moe(x, moe_ref, s, modp_ref[s, 0, 5:6, :], tm)
            xo_ref[s] = x
        h = _rms_mod(x, g_ref[...], mod_ref[s, 0, 0:1, :], mod_ref[s, 0, 1:2, :])
        z = jnp.dot(h.astype(BF16), w_ref[...], preferred_element_type=F32)
        for j in range(2):
            sl = slice(j * LANES, (j + 1) * LANES)
            kd_ref[s, :, sl] = _rope_pairs(z[:, sl], cos, sin, half, first).astype(BF16)
        vd_ref[s] = z[:, 2 * LANES:4 * LANES].astype(BF16)
        for j in range(A_Q_W // LANES):
            sl = slice(qoff + j * LANES, qoff + (j + 1) * LANES)
            q_ref[s, :, j * LANES:(j + 1) * LANES] = (
                _rope_pairs(z[:, sl], cos, sin, half, first) * scale).astype(BF16)
        u_ref[s] = jax.nn.gelu(z[:, uoff:uoff + B_WIDTH])
        gv = jax.nn.gelu(z[:, uoff + B_WIDTH:uoff + 2 * B_WIDTH])
        mu = jnp.mean(gv, axis=-1, keepdims=True)
        var = jnp.mean(jnp.square(gv - mu), axis=-1, keepdims=True)
        vn_ref[s] = ((gv - mu) * lax.rsqrt(var + NORM_EPS) * lng_ref[...]).astype(BF16)


def _samples_per_step(bsz, want):
    return want if bsz % want == 0 else SAMPLES_PER_STEP


def _tile_is_ctx(n_lat_tiles):
    return lambda i: jnp.where(i >= n_lat_tiles, 1, 0)


def _in_even(x, moe, modp, mod, g1, w_in, cos, sin, lng, n_lat):
    bsz, t, _ = x.shape
    tm = ROW_TILE
    nt = t // tm
    is_ctx = _tile_is_ctx(n_lat // tm)
    has_moe = moe is not None
    spb = _samples_per_step(bsz, 2 * SAMPLES_PER_STEP)
    row = lambda w: pl.BlockSpec((spb, tm, w), lambda b, i: (b, i, 0))
    mod_spec = pl.BlockSpec((spb, 1, MOD_ROWS, D_MODEL), lambda b, i: (b, is_ctx(i), 0, 0))
    full2 = lambda a: pl.BlockSpec(a.shape, lambda b, i: (0, 0))
    in_specs = [row(D_MODEL)]
    args = [x]
    if has_moe:
        in_specs += [pl.BlockSpec((spb, tm * ACC_CHUNKS, LANES), lambda b, i: (b, i, 0)), mod_spec]
        args += [moe, modp]
    in_specs += [mod_spec, full2(g1), full2(w_in),
                 pl.BlockSpec((tm, LANES), lambda b, i: (i, 0)),
                 pl.BlockSpec((tm, LANES), lambda b, i: (i, 0)), full2(lng)]
    args += [mod, g1, w_in, cos, sin, lng]
    out_specs = []
    out_shape = []
    if has_moe:
        out_specs.append(row(D_MODEL))
        out_shape.append(jax.ShapeDtypeStruct((bsz, t, D_MODEL), F32))
    out_specs += [row(A_Q_W), row(2 * LANES), row(2 * LANES), row(B_WIDTH), row(B_WIDTH)]
    out_shape += [jax.ShapeDtypeStruct((bsz, t, A_Q_W), BF16),
                  jax.ShapeDtypeStruct((bsz, t, 2 * LANES), BF16),
                  jax.ShapeDtypeStruct((bsz, t, 2 * LANES), BF16),
                  jax.ShapeDtypeStruct((bsz, t, B_WIDTH), F32),
                  jax.ShapeDtypeStruct((bsz, t, B_WIDTH), BF16)]
    outs = pl.pallas_call(
        functools.partial(_in_even_kernel, has_moe, tm),
        grid=(bsz // spb, nt), in_specs=in_specs, out_specs=out_specs, out_shape=out_shape,
        compiler_params=_cparams(("parallel", "arbitrary")),
        name="in_proj_even",
    )(*args)
    if not has_moe:
        outs = [x] + list(outs)
    return outs


def _win_attn_kernel(sink_ref, bias_ref, q_ref, kp_ref, kc_ref, kn_ref, kx_ref,
                     vp_ref, vc_ref, vn_ref, vx_ref, o_ref):
    blk = A_BLOCK
    g = A_HEADS // A_KV_HEADS
    bias = jnp.concatenate([bias_ref[0]] * g, axis=0)
    lane = lax.broadcasted_iota(jnp.int32, (blk, LANES), 1)
    low = lane < A_HEAD_DIM
    zero = jnp.zeros((), BF16)
    for smp, h in [(a, b) for a in range(q_ref.shape[0]) for b in range(A_KV_HEADS)]:
        hs = slice(h * LANES, (h + 1) * LANES)
        qs = []
        for cidx in range(g // 2):
            qc = q_ref[smp, :, (h * (g // 2) + cidx) * LANES:(h * (g // 2) + cidx + 1) * LANES]
            qs.append(jnp.where(low, qc, zero))
            qs.append(jnp.where(low, zero, qc))
        qh = jnp.concatenate(qs, axis=0)
        kcat = jnp.concatenate([kp_ref[smp, :, hs], kc_ref[smp, :, hs], kn_ref[smp, :, hs],
                                kx_ref[smp, :, hs]], axis=0)
        vcat = jnp.concatenate([vp_ref[smp, :, hs], vc_ref[smp, :, hs], vn_ref[smp, :, hs],
                                vx_ref[smp, :, hs]], axis=0)
        s = _nt_dot(qh, kcat) + bias
        sink = jnp.concatenate(
            [jnp.full((blk, 1), sink_ref[h * g + j] * LOG2_E, F32) for j in range(g)], axis=0)
        m = jnp.maximum(jnp.max(s, axis=-1, keepdims=True), sink)
        p = jnp.exp2(s - m)
        den = jnp.sum(p, axis=-1, keepdims=True) + jnp.exp2(sink - m)
        o = jnp.dot(p.astype(BF16), vcat, preferred_element_type=F32) * (1.0 / den)
        for cidx in range(g // 2):
            o_even = o[(2 * cidx) * blk:(2 * cidx + 1) * blk]
            o_odd = o[(2 * cidx + 1) * blk:(2 * cidx + 2) * blk]
            col = (h * (g // 2) + cidx) * LANES
            o_ref[smp, :, col:col + LANES] = jnp.where(low, o_even, o_odd).astype(BF16)


def _win_attn(sink, q, kd, vd, n_lat, n_ctx):
    bsz, t, _ = q.shape
    blk = A_BLOCK
    nb = t // blk
    nlb = n_lat // blk
    kvw = 2 * LANES
    spb = _samples_per_step(bsz, 2 * SAMPLES_PER_STEP)
    prev = pl.BlockSpec((spb, blk, kvw), lambda b, i: (b, jnp.maximum(i - 1, 0), 0))
    cur = pl.BlockSpec((spb, blk, kvw), lambda b, i: (b, i, 0))
    nxt = pl.BlockSpec((spb, blk, kvw), lambda b, i: (b, jnp.minimum(i + 1, nb - 1), 0))
    ctx = pl.BlockSpec((spb, n_ctx, kvw), lambda b, i: (b, t // n_ctx - 1, 0))
    r = jnp.arange(blk)[:, None]
    c = jnp.arange(blk)[None, :]
    yes = jnp.ones((blk, blk), jnp.bool_)
    no = jnp.zeros((blk, blk), jnp.bool_)
    ctx_ok = jnp.ones((blk, n_ctx), jnp.bool_)
    kinds = [(c >= r, yes, c <= r), (no, yes, c <= r), (c >= r, yes, no), (no, no, no)]
    bias = jnp.stack([jnp.where(jnp.concatenate([p, m, n, ctx_ok], axis=1), 0.0, NEG_INF)
                      for p, m, n in kinds]).astype(F32)
    kind = lambda i: jnp.where(i >= nlb, 3, jnp.where(i == 0, 1, jnp.where(i == nlb - 1, 2, 0)))
    return pl.pallas_call(
        _win_attn_kernel,
        grid=(bsz // spb, nb),
        in_specs=[pl.BlockSpec(memory_space=pltpu.SMEM),
                  pl.BlockSpec((1, blk, 3 * blk + n_ctx), lambda b, i: (kind(i), 0, 0)),
                  pl.BlockSpec((spb, blk, A_Q_W), lambda b, i: (b, i, 0)),
                  prev, cur, nxt, ctx, prev, cur, nxt, ctx],
        out_specs=pl.BlockSpec((spb, blk, A_Q_W), lambda b, i: (b, i, 0)),
        out_shape=jax.ShapeDtypeStruct((bsz, t, A_Q_W), BF16),
        compiler_params=_cparams(("parallel", "arbitrary")),
        name="window_attention",
    )(sink, bias, q, kd, kd, kd, kd, vd, vd, vd, vd)


def _out_epilogue(tm, s, y, x_ref, mod_ref, g2_ref, rw_ref, xo_ref, hp_ref, aff_ref):
    x = x_ref[s] + mod_ref[s, 0, 2:3, :] * y
    xo_ref[s] = x
    h2 = _rms_mod(x, g2_ref[...], mod_ref[s, 0, 3:4, :], mod_ref[s, 0, 4:5, :])
    h_hi = h2.astype(BF16)
    h_hi32 = h_hi.astype(F32)
    bits = lax.bitcast_convert_type(h_hi32, jnp.uint32)
    half = D_MODEL // 2
    packed = (bits[:, :half] >> 16) | (bits[:, half:] & jnp.uint32(0xFFFF0000))
    for j in range(HP_CHUNKS):
        hp_ref[s, pl.ds(j, tm, stride=HP_CHUNKS), :] = packed[:, j * LANES:(j + 1) * LANES]
    h_lo = (h2 - h_hi32).astype(BF16)
    logits = (_nt_dot(rw_ref[0], h_hi) + _nt_dot(rw_ref[0], h_lo)) + _nt_dot(rw_ref[1], h_hi)
    mx = jnp.max(logits, axis=0, keepdims=True)
    ex = jnp.exp(logits - mx)
    aff_ref[s] = ex / jnp.sum(ex, axis=0, keepdims=True)


def _out_even_kernel(tm, o_ref, u_ref, vn_ref, ws_ref, bs_ref, wo_ref, x_ref, mod_ref, g2_ref,
                     rw_ref, xo_ref, hp_ref, aff_ref):
    dot = functools.partial(jnp.dot, preferred_element_type=F32)
    gw = B_WIDTH // B_GROUPS
    for s in range(x_ref.shape[0]):
        rows = []
        for cidx in range(tm // B_CHUNK):
            rs = slice(cidx * B_CHUNK, (cidx + 1) * B_CHUNK)
            cols = []
            for g in range(B_GROUPS):
                cs = slice(g * gw, (g + 1) * gw)
                mixed = dot(ws_ref[g], vn_ref[s, rs, cs]) + bs_ref[:, cs]
                cols.append(u_ref[s, rs, cs] * mixed)
            rows.append(jnp.concatenate(cols, axis=1))
        sg = jnp.concatenate(rows, axis=0)
        y = dot(o_ref[s], wo_ref[:A_Q_W, :]) + dot(sg.astype(BF16), wo_ref[A_Q_W:, :])
        _out_epilogue(tm, s, y, x_ref, mod_ref, g2_ref, rw_ref, xo_ref, hp_ref, aff_ref)


def _out_specs_common(bsz, t, tm, spb):
    row = lambda w: pl.BlockSpec((spb, tm, w), lambda b, i: (b, i, 0))
    out_specs = [row(D_MODEL),
                 pl.BlockSpec((spb, tm * HP_CHUNKS, LANES), lambda b, i: (b, i, 0)),
                 pl.BlockSpec((spb, N_EXPERTS, tm), lambda b, i: (b, 0, i))]
    out_shape = [jax.ShapeDtypeStruct((bsz, t, D_MODEL), F32),
                 jax.ShapeDtypeStruct((bsz, t * HP_CHUNKS, LANES), jnp.uint32),
                 jax.ShapeDtypeStruct((bsz, N_EXPERTS, t), F32)]
    return out_specs, out_shape


def _out_even(o, u, vn, ws, bs, wo, x, mod, g2, rw_t, n_lat):
    bsz, t, _ = x.shape
    tm = ROW_TILE
    is_ctx = _tile_is_ctx(n_lat // tm)
    spb = _samples_per_step(bsz, 2 * SAMPLES_PER_STEP)
    row = lambda w: pl.BlockSpec((spb, tm, w), lambda b, i: (b, i, 0))
    full = lambda a: pl.BlockSpec(a.shape, lambda b, i: (0,) * a.ndim)
    mod_spec = pl.BlockSpec((spb, 1, MOD_ROWS, D_MODEL), lambda b, i: (b, is_ctx(i), 0, 0))
    out_specs, out_shape = _out_specs_common(bsz, t, tm, spb)
    return pl.pallas_call(
        functools.partial(_out_even_kernel, tm),
        grid=(bsz // spb, t // tm),
        in_specs=[row(A_Q_W), row(B_WIDTH), row(B_WIDTH), full(ws), full(bs), full(wo),
                  row(D_MODEL), mod_spec, full(g2), full(rw_t)],
        out_specs=out_specs, out_shape=out_shape,
        compiler_params=_cparams(("parallel", "arbitrary")),
        name="out_proj_even",
    )(o, u, vn, ws, bs, wo, x, mod, g2, rw_t)


def _in_odd_kernel(has_moe, tm, *refs):
    (x_ref, moe_ref, modp_ref, mod_ref, g_ref, w_ref, cos_ref, sin_ref, qg_ref, kvg_ref,
     wuq_ref, wukv_ref, xo_ref, q_ref, k_ref, v_ref, gb_ref, u_ref) = refs
    cos = cos_ref[...]
    sin = sin_ref[...]
    lane = lax.broadcasted_iota(jnp.int32, (tm, LANES), 1)
    half = C_ROPE // 2
    first = lane < C_NOPE + half
    cqo = C_KV_LORA + LANES
    co = cqo + C_Q_LORA
    scale = (C_NOPE + C_ROPE) ** -0.5 * LOG2_E
    ones_col = jnp.where(lane == C_V, 1.0, 0.0)
    voff = C_HEADS * HEAD_PAD
    for s in range(x_ref.shape[0]):
        x = _residual_from_moe(x_ref[s], moe_ref, s, modp_ref[s, 0, 5:6, :], tm)
        xo_ref[s] = x
        h = _rms_mod(x, g_ref[...], mod_ref[s, 0, 0:1, :], mod_ref[s, 0, 1:2, :])
        z = jnp.dot(h.astype(BF16), w_ref[...], preferred_element_type=F32)
        ckv = _rms(z[:, :C_KV_LORA], kvg_ref[...])
        kr = _rope_pairs(z[:, C_KV_LORA:C_KV_LORA + LANES], cos, sin, half, first)
        cq = _rms(z[:, cqo:cqo + C_Q_LORA], qg_ref[...])
        kv = jnp.dot(ckv.astype(BF16), wukv_ref[...], preferred_element_type=F32)
        qq = jnp.dot(cq.astype(BF16), wuq_ref[...], preferred_element_type=F32)
        for hd in range(C_HEADS):
            sl = slice(hd * HEAD_PAD, (hd + 1) * HEAD_PAD)
            q_ref[s, :, sl] = (_rope_pairs(qq[:, sl], cos, sin, half, first) * scale).astype(BF16)
            k_ref[s, :, sl] = (kv[:, sl] + kr).astype(BF16)
            v_ref[s, :, sl] = (kv[:, voff + hd * HEAD_PAD:voff + (hd + 1) * HEAD_PAD]
                               + ones_col).astype(BF16)
        gb_ref[s] = z[:, co:co + D_WIDTH]
        u_ref[s] = z[:, co + D_WIDTH:co + 2 * D_WIDTH] * z[:, co + 2 * D_WIDTH:co + 3 * D_WIDTH]


def _in_odd(x, moe, modp, mod, g1, w_in, cos, sin, qg, kvg, wuq, wukv, n_lat):
    bsz, t, _ = x.shape
    tm = ROW_TILE
    is_ctx = _tile_is_ctx(n_lat // tm)
    spb = SAMPLES_PER_STEP
    row = lambda w: pl.BlockSpec((spb, tm, w), lambda b, i: (b, i, 0))
    mod_spec = pl.BlockSpec((spb, 1, MOD_ROWS, D_MODEL), lambda b, i: (b, is_ctx(i), 0, 0))
    full2 = lambda a: pl.BlockSpec(a.shape, lambda b, i: (0, 0))
    hw = C_HEADS * HEAD_PAD
    return pl.pallas_call(
        functools.partial(_in_odd_kernel, True, tm),
        grid=(bsz // spb, t // tm),
        in_specs=[row(D_MODEL),
                  pl.BlockSpec((spb, tm * ACC_CHUNKS, LANES), lambda b, i: (b, i, 0)),
                  mod_spec, mod_spec, full2(g1), full2(w_in),
                  pl.BlockSpec((tm, LANES), lambda b, i: (i, 0)),
                  pl.BlockSpec((tm, LANES), lambda b, i: (i, 0)),
                  full2(qg), full2(kvg), full2(wuq), full2(wukv)],
        out_specs=[row(D_MODEL), row(hw), row(hw), row(hw), row(D_WIDTH), row(D_WIDTH)],
        out_shape=[jax.ShapeDtypeStruct((bsz, t, D_MODEL), F32),
                   jax.ShapeDtypeStruct((bsz, t, hw), BF16),
                   jax.ShapeDtypeStruct((bsz, t, hw), BF16),
                   jax.ShapeDtypeStruct((bsz, t, hw), BF16),
                   jax.ShapeDtypeStruct((bsz, t, D_WIDTH), F32),
                   jax.ShapeDtypeStruct((bsz, t, D_WIDTH), F32)],
        compiler_params=_cparams(("parallel", "arbitrary")),
        name="in_proj_odd",
    )(x, moe, modp, mod, g1, w_in, cos, sin, qg, kvg, wuq, wukv)


def _dense_attn_kernel(tk, q_ref, k_ref, v_ref, o_ref):
    q = q_ref[0]
    tq = q.shape[0]
    nk = k_ref.shape[1] // tk

    m = jnp.full((tq, 1), NEG_INF, F32)
    acc = jnp.zeros((tq, HEAD_PAD), F32)
    for j in range(nk):
        s = _nt_dot(q, k_ref[0, j * tk:(j + 1) * tk, :])
        m_new = jnp.maximum(m, jnp.max(s, axis=-1, keepdims=True))
        alpha = jnp.exp2(m - m_new)
        p = jnp.exp2(s - m_new)
        acc = alpha * acc + jnp.dot(p.astype(BF16), v_ref[0, j * tk:(j + 1) * tk, :],
                                    preferred_element_type=F32)
        m = m_new
    o_ref[0] = (acc / acc[:, C_V:C_V + 1]).astype(BF16)


def _dense_attn(q, k, v, q_start, q_len, k_start, k_len, tq):
    bsz, _, hw = q.shape
    tk = ROW_TILE
    q0 = q_start // tq
    kb = k_start // k_len
    return pl.pallas_call(
        functools.partial(_dense_attn_kernel, tk),
        grid=(bsz, C_HEADS, q_len // tq),
        in_specs=[pl.BlockSpec((1, tq, HEAD_PAD), lambda b, h, i: (b, q0 + i, h)),
                  pl.BlockSpec((1, k_len, HEAD_PAD), lambda b, h, i: (b, kb, h)),
                  pl.BlockSpec((1, k_len, HEAD_PAD), lambda b, h, i: (b, kb, h))],
        out_specs=pl.BlockSpec((1, tq, HEAD_PAD), lambda b, h, i: (b, i, h)),
        out_shape=jax.ShapeDtypeStruct((bsz, q_len, hw), BF16),
        compiler_params=_cparams(("parallel", "parallel", "arbitrary")),
        name="dense_attention",
    )(q, k, v)


def _out_odd_kernel(tm, n_lat_tiles, n_tiles, ol_ref, oc_ref, gb_ref, u_ref, up_ref, un_ref, cw_ref,
                    wo_ref, x_ref, mod_ref, g2_ref, rw_ref, xo_ref, hp_ref, aff_ref):
    i = pl.program_id(1)
    dot = functools.partial(jnp.dot, preferred_element_type=F32)
    has_prev = jnp.logical_and(i != 0, i != n_lat_tiles)
    has_next = jnp.logical_and(i != n_lat_tiles - 1, i != n_tiles - 1)
    hw = C_HEADS * HEAD_PAD
    for s in range(x_ref.shape[0]):
        o_att = jnp.where(i < n_lat_tiles, ol_ref[s], oc_ref[s])
        u = u_ref[s]
        prev_row = jnp.where(has_prev, up_ref[s, SUBLANES - 1:SUBLANES, :], 0.0)
        next_row = jnp.where(has_next, un_ref[s, 0:1, :], 0.0)
        ridx = lax.broadcasted_iota(jnp.int32, u.shape, 0)
        u_m1 = jnp.where(ridx == 0, prev_row, pltpu.roll(u, 1, 0))
        u_p1 = jnp.where(ridx == tm - 1, next_row, pltpu.roll(u, tm - 1, 0))
        conv = u_m1 * cw_ref[0:1, :] + u * cw_ref[1:2, :] + u_p1 * cw_ref[2:3, :]
        c = gb_ref[s] * conv
        y = dot(o_att, wo_ref[:hw, :]) + dot(c.astype(BF16), wo_ref[hw:, :])
        _out_epilogue(tm, s, y, x_ref, mod_ref, g2_ref, rw_ref, xo_ref, hp_ref, aff_ref)


def _out_odd(o_lat, o_ctx, gb, u, cw, wo, x, mod, g2, rw_t, n_lat):
    bsz, t, _ = x.shape
    tm = ROW_TILE
    nt = t // tm
    nlt = n_lat // tm
    is_ctx = _tile_is_ctx(nlt)
    spb = _samples_per_step(bsz, 2 * SAMPLES_PER_STEP)
    row = lambda w: pl.BlockSpec((spb, tm, w), lambda b, i: (b, i, 0))
    full = lambda a: pl.BlockSpec(a.shape, lambda b, i: (0,) * a.ndim)
    mod_spec = pl.BlockSpec((spb, 1, MOD_ROWS, D_MODEL), lambda b, i: (b, is_ctx(i), 0, 0))
    hw = C_HEADS * HEAD_PAD
    per = tm // SUBLANES
    last = t // SUBLANES - 1
    halo_prev = pl.BlockSpec((spb, SUBLANES, D_WIDTH), lambda b, i: (b, jnp.maximum(i * per - 1, 0), 0))
    halo_next = pl.BlockSpec((spb, SUBLANES, D_WIDTH), lambda b, i: (b, jnp.minimum((i + 1) * per, last), 0))
    out_specs, out_shape = _out_specs_common(bsz, t, tm, spb)
    return pl.pallas_call(
        functools.partial(_out_odd_kernel, tm, nlt, nt),
        grid=(bsz // spb, nt),
        in_specs=[pl.BlockSpec((spb, tm, hw), lambda b, i: (b, jnp.minimum(i, nlt - 1), 0)),
                  pl.BlockSpec((spb, tm, hw), lambda b, i: (b, jnp.maximum(i - nlt, 0), 0)),
                  row(D_WIDTH), row(D_WIDTH), halo_prev, halo_next,
                  full(cw), full(wo), row(D_MODEL), mod_spec, full(g2), full(rw_t)],
        out_specs=out_specs, out_shape=out_shape,
        compiler_params=_cparams(("parallel", "arbitrary")),
        name="out_proj_odd",
    )(o_lat, o_ctx, gb, u, u, u, cw, wo, x, mod, g2, rw_t)


def _route_kernel(cap, nblk, width, a_ref, idx_ref, gate_ref):
    n_e = N_EXPERTS
    a = a_ref[0]
    bits = lax.bitcast_convert_type(a, jnp.int32)

    def count(mask):
        c = jnp.sum(jnp.where(mask, 1.0, 0.0), axis=2, keepdims=True)
        return jnp.sum(c, axis=1, keepdims=True)

    def search(it, thr):
        cand = thr | jnp.left_shift(jnp.int32(1), 30 - it)
        return jnp.where(count(bits >= cand) >= cap, cand, thr)

    thr = lax.fori_loop(0, 31, search, jnp.zeros((n_e, 1, 1), jnp.int32))
    gt = bits > thr
    eq = bits == thr
    need = cap - count(gt)

    rows = n_e * nblk
    li = lax.broadcasted_iota(jnp.int32, (width, width), 0)
    lj = lax.broadcasted_iota(jnp.int32, (width, width), 1)
    upper = jnp.where(li <= lj, 1.0, 0.0).astype(BF16)
    ones_sq = jnp.ones((width, width), BF16)
    if nblk > 1:
        ri = lax.broadcasted_iota(jnp.int32, (rows, rows), 0)
        rj = lax.broadcasted_iota(jnp.int32, (rows, rows), 1)
        lower = jnp.where(jnp.logical_and(ri // nblk == rj // nblk, rj < ri), 1.0, 0.0).astype(BF16)

    def prefix(mask3):
        m2 = jnp.where(mask3, 1.0, 0.0).astype(BF16).reshape(rows, width)
        local = jnp.dot(m2, upper, preferred_element_type=F32)
        total = jnp.dot(m2, ones_sq, preferred_element_type=F32)
        if nblk > 1:
            excl = jnp.dot(lower, total.astype(BF16), preferred_element_type=F32)
        else:
            excl = jnp.zeros_like(total)
        return local, total, excl

    l_eq, _, x_eq = prefix(eq)
    tie_rank = (l_eq + x_eq).reshape(n_e, nblk, width)
    sel = jnp.logical_or(gt, jnp.logical_and(eq, tie_rank <= need))
    local, total, excl = prefix(sel)
    pin = local + excl

    slot_l = lax.broadcasted_iota(jnp.int32, (nblk, cap), 1).astype(F32)
    blk_s = lax.broadcasted_iota(jnp.int32, (nblk, cap), 0).astype(F32)
    slot_s = lax.broadcasted_iota(jnp.int32, (cap, width), 0).astype(F32)
    lane_w = lax.broadcasted_iota(jnp.int32, (cap, width), 1).astype(F32)
    ones_r = jnp.ones((SUBLANES, width), BF16)
    reps = cap // width if cap >= width else 1
    a2 = a.reshape(rows, width)
    if nblk > 1:
        pin_hi = jnp.floor(pin * (1.0 / PIN_SPLIT))
        a_1 = a2.astype(BF16)
        rem = a2 - a_1.astype(F32)
        a_2 = rem.astype(BF16)
        a_3 = (rem - a_2.astype(F32)).astype(BF16)
        table = jnp.concatenate([pin_hi.astype(BF16), (pin - PIN_SPLIT * pin_hi).astype(BF16),
                                 a_1, a_2, a_3], axis=1)
    for e in range(n_e):
        rs = slice(e * nblk, (e + 1) * nblk)
        if nblk > 1:
            lo = jnp.concatenate([excl[rs]] * reps, axis=1)[:, :cap]
            hi = lo + jnp.concatenate([total[rs]] * reps, axis=1)[:, :cap]
            oh_t = jnp.where(jnp.logical_and(lo <= slot_l, slot_l < hi), 1.0, 0.0)
            got = _tn_dot(oh_t.astype(BF16), table[rs])
            pin_g = PIN_SPLIT * got[:, :width] + got[:, width:2 * width]
            aff_g = (got[:, 2 * width:3 * width] + got[:, 3 * width:4 * width]) + got[:, 4 * width:]
            blk_row = jnp.sum(oh_t * blk_s, axis=0, keepdims=True)
        else:
            pin_g = jnp.broadcast_to(pin[rs], (cap, width))
            aff_g = jnp.broadcast_to(a2[rs], (cap, width))
            blk_row = jnp.zeros((1, cap), F32)
        ind = jnp.where(pin_g <= slot_s, 1.0, 0.0).astype(BF16)
        cnt_row = _nt_dot(ones_r, ind)[0:1]
        cnt_b = jnp.dot(ind, ones_sq, preferred_element_type=F32)
        picked = jnp.where(lane_w == cnt_b, aff_g, 0.0)
        idx_ref[0, e] = (blk_row * width + cnt_row).astype(jnp.int32)
        gate_ref[0, e] = jnp.sum(picked, axis=1, keepdims=True)


def _route(aff, cap):
    bsz, n_e, n = aff.shape
    width = LANES if n % (LANES * SUBLANES) == 0 else n
    nblk = n // width
    a4 = aff.reshape(bsz, n_e, nblk, width)
    idx, gate = pl.pallas_call(
        functools.partial(_route_kernel, cap, nblk, width),
        grid=(bsz,),
        in_specs=[pl.BlockSpec((1, n_e, nblk, width), lambda b: (b, 0, 0, 0))],
        out_specs=[pl.BlockSpec((1, n_e, 1, cap), lambda b: (b, 0, 0, 0)),
                   pl.BlockSpec((1, n_e, cap, 1), lambda b: (b, 0, 0, 0))],
        out_shape=[jax.ShapeDtypeStruct((bsz, n_e, 1, cap), jnp.int32),
                   jax.ShapeDtypeStruct((bsz, n_e, cap, 1), F32)],
        compiler_params=_cparams(("parallel",)),
        name="expert_choice_routing",
    )(a4)
    return idx.reshape(bsz, n_e, cap), gate.reshape(bsz, n_e, cap)


def _ffn_copies(hp_hbm, out_hbm, hsrc, acc, sem_in, sem_out, grp, nsub, n_per, row0):
    ins = []
    outs = []
    for sb in range(nsub):
        b = grp * nsub + sb
        ins.append(pltpu.make_async_copy(
            hp_hbm.at[b, pl.ds(row0 * HP_CHUNKS, n_per * HP_CHUNKS), :],
            hsrc.at[pl.ds(sb * n_per * HP_CHUNKS, n_per * HP_CHUNKS), :], sem_in.at[sb]))
        outs.append(pltpu.make_async_copy(
            acc.at[pl.ds(sb * n_per * ACC_CHUNKS, n_per * ACC_CHUNKS), :],
            out_hbm.at[b, pl.ds(row0 * ACC_CHUNKS, n_per * ACC_CHUNKS), :], sem_out.at[sb]))
    return ins, outs


def _ffn_gather(idx_ref, hsrc, xt, stride, mi):
    src = pl.multiple_of(idx_ref[0, 0, 0, mi] * HP_CHUNKS, HP_CHUNKS)
    xt[pl.ds(mi, HP_CHUNKS, stride=stride), :] = hsrc[pl.ds(src, HP_CHUNKS), :]


def _ffn_scatter(idx_ref, gate_ref, gate_on, acc, yt, stride, mis):
    dst = []
    val = []
    for mi in mis:
        d = pl.multiple_of(idx_ref[0, 0, 0, mi] * ACC_CHUNKS, ACC_CHUNKS)
        gate = gate_ref[0, 0, 0, mi]
        if gate_on is not None:
            gate = jnp.where(gate_on, gate, 0.0)
        slab = yt[pl.ds(mi, ACC_CHUNKS, stride=stride), :] * gate
        dst.append(d)
        val.append(acc[pl.ds(d, ACC_CHUNKS), :] + slab)
    for d, v in zip(dst, val):
        acc[pl.ds(d, ACC_CHUNKS), :] = v


def _ffn_kernel(m_slots, nsub, n_per, row0, unroll, idx_ref, idxn_ref, idxp_ref, gate_ref, gatep_ref,
                hp_hbm, w1_ref, w3_ref, w2_ref, _moe_in, out_hbm, hsrc, acc, xt, xs_buf, act_buf,
                yt, sem_in, sem_out):
    grp = pl.program_id(0)
    e = pl.program_id(1)
    stride = m_slots + SUBLANES
    ins, outs = _ffn_copies(hp_hbm, out_hbm, hsrc, acc, sem_in, sem_out, grp, nsub, n_per, row0)

    @pl.when(jnp.logical_and(grp == 0, e == 0))
    def _():
        yt[...] = jnp.zeros(yt.shape, F32)

    @pl.when(e == 0)
    def _():
        for cp in ins:
            cp.start()
        acc[...] = jnp.zeros(acc.shape, F32)
        for cp in ins:
            cp.wait()

        def first_gather(c, carry):
            for k in range(unroll):
                _ffn_gather(idx_ref, hsrc, xt, stride, c * unroll + k)
            return carry

        lax.fori_loop(0, m_slots // unroll, first_gather, 0)

    half = D_MODEL // 2
    for j in range(HP_CHUNKS):
        w = xt[pl.ds(j * stride, m_slots), :]
        xs_buf[:, j * LANES:(j + 1) * LANES] = lax.bitcast_convert_type(w << 16, F32).astype(BF16)
        xs_buf[:, half + j * LANES:half + (j + 1) * LANES] = lax.bitcast_convert_type(
            w & jnp.uint32(0xFFFF0000), F32).astype(BF16)

    for c in range(m_slots // unroll):
        _ffn_scatter(idxp_ref, gatep_ref, e > 0, acc, yt, stride, range(c * unroll, (c + 1) * unroll))
    for mi in range(m_slots):
        _ffn_gather(idxn_ref, hsrc, xt, stride, mi)

    dot = functools.partial(jnp.dot, preferred_element_type=F32)
    xs = xs_buf[...]
    a = dot(xs, w1_ref[0, 0])
    b = dot(xs, w3_ref[0, 0])
    act_buf[...] = (a * jax.nn.sigmoid(a) * b).astype(BF16)
    seen = xt[pl.ds(0, SUBLANES), :]
    for r in range(SUBLANES, m_slots, SUBLANES):
        seen = seen | xt[pl.ds(r, SUBLANES), :]
    seen = seen | lax.bitcast_convert_type(acc[pl.ds(0, SUBLANES), :], jnp.uint32)
    zero = lax.bitcast_convert_type(seen >> 32, F32)
    tile = (2 * SUBLANES, LANES)
    act_buf[:tile[0], :tile[1]] = (act_buf[:tile[0], :tile[1]].astype(F32)
                                   + jnp.concatenate([zero, zero], axis=0)).astype(BF16)
    y = dot(act_buf[...], w2_ref[0, 0])
    for j in range(ACC_CHUNKS):
        yt[pl.ds(j * stride, m_slots), :] = y[:, j * LANES:(j + 1) * LANES]

    @pl.when(e == N_EXPERTS - 1)
    def _():
        def last_scatter(c, carry):
            _ffn_scatter(idx_ref, gate_ref, None, acc, yt, stride,
                         [c * unroll + k for k in range(unroll)])
            return carry

        lax.fori_loop(0, m_slots // unroll, last_scatter, 0)
        for cp in outs:
            cp.start()
        for cp in outs:
            cp.wait()


def _expert_ffn(idx, gate, hp, w1, w3, w2, layer, moe_buf, nsub, n_per, row0):
    groups, n_e, m_slots = idx.shape
    n_tok = nsub * n_per
    stride = m_slots + SUBLANES
    unroll = 8
    cur = lambda g, e: (g, e, 0, 0)
    nxt = lambda g, e: (g, jnp.minimum(e + 1, n_e - 1), 0, 0)
    prv = lambda g, e: (g, jnp.maximum(e - 1, 0), 0, 0)
    smem = lambda imap: pl.BlockSpec((1, 1, 1, m_slots), imap, memory_space=pltpu.SMEM)
    wspec = lambda: pl.BlockSpec((1, 1, D_MODEL, EXPERT_FF), lambda g, e: (layer, e, 0, 0))
    idx4 = idx.reshape(groups, n_e, 1, m_slots)
    gate4 = gate.reshape(groups, n_e, 1, m_slots)
    return pl.pallas_call(
        functools.partial(_ffn_kernel, m_slots, nsub, n_per, row0, unroll),
        grid=(groups, n_e),
        in_specs=[smem(cur), smem(nxt), smem(prv), smem(cur), smem(prv),
                  pl.BlockSpec(memory_space=pl.ANY), wspec(), wspec(),
                  pl.BlockSpec((1, 1, EXPERT_FF, D_MODEL), lambda g, e: (layer, e, 0, 0)),
                  pl.BlockSpec(memory_space=pl.ANY)],
        out_specs=pl.BlockSpec(memory_space=pl.ANY),
        out_shape=jax.ShapeDtypeStruct(moe_buf.shape, F32),
        scratch_shapes=[pltpu.VMEM((n_tok * HP_CHUNKS, LANES), jnp.uint32),
                        pltpu.VMEM((n_tok * ACC_CHUNKS, LANES), F32),
                        pltpu.VMEM((HP_CHUNKS * stride, LANES), jnp.uint32),
                        pltpu.VMEM((m_slots, D_MODEL), BF16),
                        pltpu.VMEM((m_slots, EXPERT_FF), BF16),
                        pltpu.VMEM((ACC_CHUNKS * stride, LANES), F32),
                        pltpu.SemaphoreType.DMA((nsub,)),
                        pltpu.SemaphoreType.DMA((nsub,))],
        input_output_aliases={9: 0},
        compiler_params=_cparams(("arbitrary", "arbitrary")),
        name="expert_ffn",
    )(idx4, idx4, idx4, gate4, gate4, hp, w1, w3, w2, moe_buf)


def _final_kernel(tm, x_ref, moe_ref, modp_ref, g_ref, o_ref):
    x = _residual_from_moe(x_ref[0], moe_ref, 0, modp_ref[0, 0, 5:6, :], tm)
    o_ref[0] = _rms(x, g_ref[...])


def _final(x, moe, modp, g, n_lat):
    bsz = x.shape[0]
    tm = ROW_TILE
    return pl.pallas_call(
        functools.partial(_final_kernel, tm),
        grid=(bsz, n_lat // tm),
        in_specs=[pl.BlockSpec((1, tm, D_MODEL), lambda b, i: (b, i, 0)),
                  pl.BlockSpec((1, tm * ACC_CHUNKS, LANES), lambda b, i: (b, i, 0)),
                  pl.BlockSpec((1, 1, MOD_ROWS, D_MODEL), lambda b, i: (b, 0, 0, 0)),
                  pl.BlockSpec(g.shape, lambda b, i: (0, 0))],
        out_specs=pl.BlockSpec((1, tm, D_MODEL), lambda b, i: (b, i, 0)),
        out_shape=jax.ShapeDtypeStruct((bsz, n_lat, D_MODEL), F32),
        compiler_params=_cparams(("parallel", "arbitrary")),
        name="final_norm",
    )(x, moe, modp, g)


def _rope_angles(rows, rot_dim):
    row = jnp.repeat(jnp.arange(rows, dtype=F32), GRID_W)
    col = jnp.tile(jnp.arange(GRID_W, dtype=F32), rows)
    n_freq = rot_dim // 4
    inv_freq = ROPE_BASE ** (-jnp.arange(n_freq, dtype=F32) / n_freq)
    ang = jnp.concatenate([row[:, None] * inv_freq[None, :], col[:, None] * inv_freq[None, :]], axis=-1)
    return jnp.cos(ang), jnp.sin(ang)


def _rope_tables_even(n_lat, n_ctx):
    cos, sin = _rope_angles(n_lat // GRID_W, A_HEAD_DIM)
    cos_h = jnp.concatenate([cos, cos], axis=1)
    sin_h = jnp.concatenate([-sin, sin], axis=1)
    reps = LANES // A_HEAD_DIM
    cos_l = jnp.tile(cos_h, (1, reps))
    sin_l = jnp.tile(sin_h, (1, reps))
    return (jnp.concatenate([cos_l, jnp.ones((n_ctx, LANES), F32)], axis=0),
            jnp.concatenate([sin_l, jnp.zeros((n_ctx, LANES), F32)], axis=0))


def _rope_tables_odd(n_lat, n_ctx):
    cos, sin = _rope_angles(n_lat // GRID_W, C_ROPE)
    n = cos.shape[0]
    pad = LANES - C_NOPE - C_ROPE
    cos_l = jnp.concatenate([jnp.ones((n, C_NOPE), F32), cos, cos, jnp.ones((n, pad), F32)], axis=1)
    sin_l = jnp.concatenate([jnp.zeros((n, C_NOPE), F32), -sin, sin, jnp.zeros((n, pad), F32)], axis=1)
    return (jnp.concatenate([cos_l, jnp.ones((n_ctx, LANES), F32)], axis=0),
            jnp.concatenate([sin_l, jnp.zeros((n_ctx, LANES), F32)], axis=0))


def _even_w_in(w):
    k = w[:, :A_KV_W]
    v = w[:, A_KV_W:2 * A_KV_W]
    dup = lambda a: jnp.concatenate(
        [a[:, h * A_HEAD_DIM:(h + 1) * A_HEAD_DIM] for h in range(A_KV_HEADS) for _ in range(2)], axis=1)
    return jnp.concatenate([dup(k), dup(v), w[:, 2 * A_KV_W:]], axis=1).astype(BF16)


def _odd_w_in(w):
    d = w.shape[0]
    kr = w[:, C_KV_LORA:C_KV_LORA + C_ROPE]
    kr_group = jnp.concatenate(
        [jnp.zeros((d, C_NOPE), F32), kr, jnp.zeros((d, LANES - C_NOPE - C_ROPE), F32)], axis=1)
    return jnp.concatenate([w[:, :C_KV_LORA], kr_group, w[:, C_KV_LORA + C_ROPE:]], axis=1).astype(BF16)


def _odd_w_uq(w):
    r = w.shape[0]
    w3 = w.reshape(r, C_HEADS, C_NOPE + C_ROPE)
    pad = jnp.zeros((r, C_HEADS, HEAD_PAD - C_NOPE - C_ROPE), F32)
    return jnp.concatenate([w3, pad], axis=2).reshape(r, C_HEADS * HEAD_PAD).astype(BF16)


def _odd_w_ukv(w):
    r = w.shape[0]
    w3 = w.reshape(r, C_HEADS, C_NOPE + C_V)
    kpad = jnp.zeros((r, C_HEADS, HEAD_PAD - C_NOPE), F32)
    vpad = jnp.zeros((r, C_HEADS, HEAD_PAD - C_V), F32)
    kpart = jnp.concatenate([w3[:, :, :C_NOPE], kpad], axis=2).reshape(r, C_HEADS * HEAD_PAD)
    vpart = jnp.concatenate([w3[:, :, C_NOPE:], vpad], axis=2).reshape(r, C_HEADS * HEAD_PAD)
    return jnp.concatenate([kpart, vpart], axis=1).astype(BF16)


def _odd_w_out(w):
    d = w.shape[1]
    att = w[:C_HEADS * C_V].reshape(C_HEADS, C_V, d)
    att = jnp.concatenate([att, jnp.zeros((C_HEADS, HEAD_PAD - C_V, d), F32)], axis=1)
    return jnp.concatenate([att.reshape(C_HEADS * HEAD_PAD, d), w[C_HEADS * C_V:]], axis=0).astype(BF16)


def _mod_table(mods_l, bsz):
    lat = mods_l[:bsz].reshape(bsz, 1, 6, D_MODEL)
    ctx = jnp.broadcast_to(mods_l[bsz].reshape(1, 1, 6, D_MODEL), (bsz, 1, 6, D_MODEL))
    tab = jnp.concatenate([lat, ctx], axis=1)
    return jnp.pad(tab, ((0, 0), (0, 0), (0, MOD_ROWS - 6), (0, 0)))


def kernel(x, c, ctx, c_ctx, mod_w, mod_b, norm1_g, norm2_g, ev_w_in, ev_sink, ev_sgu_norm_g, ev_sgu_w, ev_sgu_b, ev_w_out, od_w_in, od_q_norm_g, od_w_uq, od_kv_norm_g, od_w_ukv, od_conv_w, od_w_out, router_w, exp_w1, exp_w3, exp_w2, final_g):
    bsz, n_lat, _ = x.shape
    n_ctx = ctx.shape[1]
    t = n_lat + n_ctx
    cap_lat = EC_FACTOR * n_lat // N_EXPERTS
    cap_ctx = EC_FACTOR * n_ctx // N_EXPERTS

    mod_rows = -(-(bsz + 1) // SUBLANES) * SUBLANES
    cc = jnp.concatenate([c, c_ctx[None, :], jnp.zeros((mod_rows - bsz - 1, D_MODEL), F32)], axis=0)
    mods = _modulation(cc, mod_w, mod_b)
    tabs = [_mod_table(mods[l], bsz) for l in range(DEPTH)]

    cos_e, sin_e = _rope_tables_even(n_lat, n_ctx)
    cos_o, sin_o = _rope_tables_odd(n_lat, n_ctx)

    xs = jnp.concatenate([x, ctx], axis=1)
    w1 = exp_w1.astype(BF16)
    w3 = exp_w3.astype(BF16)
    w2 = exp_w2.astype(BF16)
    moe = None
    for layer in range(DEPTH):
        i = layer // 2
        need_ctx = layer < DEPTH - 1
        g1 = norm1_g[layer][None, :]
        g2 = norm2_g[layer][None, :]
        rw_f = router_w[layer].T
        rw_hi = rw_f.astype(BF16)
        rw_t = jnp.stack([rw_hi, (rw_f - rw_hi.astype(F32)).astype(BF16)])
        modp = tabs[layer - 1] if layer > 0 else None
        if layer % 2 == 0:
            xs, q, kd, vd, u, vn = _in_even(xs, moe, modp, tabs[layer], g1, _even_w_in(ev_w_in[i]),
                                            cos_e, sin_e, ev_sgu_norm_g[i][None, :], n_lat)
            o = _win_attn(ev_sink[i], q, kd, vd, n_lat, n_ctx)
            bs = jnp.repeat(ev_sgu_b[i].T, B_WIDTH // B_GROUPS, axis=1)
            xs, hp, aff = _out_even(o, u, vn, ev_sgu_w[i].astype(BF16), bs, ev_w_out[i].astype(BF16),
                                    xs, tabs[layer], g2, rw_t, n_lat)
        else:
            xs, q, k, v, gb, u = _in_odd(xs, moe, modp, tabs[layer], g1, _odd_w_in(od_w_in[i]),
                                         cos_o, sin_o, od_q_norm_g[i][None, :], od_kv_norm_g[i][None, :],
                                         _odd_w_uq(od_w_uq[i]), _odd_w_ukv(od_w_ukv[i]), n_lat)
            o_lat = _dense_attn(q, k, v, 0, n_lat, 0, t, 4 * ROW_TILE)
            o_ctx = _dense_attn(q, k, v, n_lat, n_ctx, n_lat, n_ctx, n_ctx)
            xs, hp, aff = _out_odd(o_lat, o_ctx, gb, u, od_conv_w[i], _odd_w_out(od_w_out[i]), xs, tabs[layer],
                                   g2, rw_t, n_lat)
        if moe is None:
            moe = jnp.zeros((bsz, t * ACC_CHUNKS, LANES), F32)
        idx, gate = _route(aff[:, :, :n_lat], cap_lat)
        moe = _expert_ffn(idx, gate, hp, w1, w3, w2, layer, moe, 1, n_lat, 0)
        if need_ctx:
            idx_c, gate_c = _route(aff[:, :, n_lat:], cap_ctx)
            idx_c = idx_c + (jnp.arange(bsz, dtype=jnp.int32) * n_ctx)[:, None, None]
            idx_c = jnp.transpose(idx_c, (1, 0, 2)).reshape(1, N_EXPERTS, bsz * cap_ctx)
            gate_c = jnp.transpose(gate_c, (1, 0, 2)).reshape(1, N_EXPERTS, bsz * cap_ctx)
            moe = _expert_ffn(idx_c, gate_c, hp, w1, w3, w2, layer, moe, bsz, n_ctx, n_lat)
    return _final(xs, moe, tabs[DEPTH - 1], final_g[None, :], n_lat)
```

```python
import functools

import jax
import jax.numpy as jnp
from jax import lax
from jax.experimental import pallas as pl
from jax.experimental.pallas import tpu as pltpu

F32 = jnp.float32
BF16 = jnp.bfloat16
HIGHEST = lax.Precision.HIGHEST

D_MODEL = 1024
DEPTH = 4
GRID_W = 64
NORM_EPS = 1e-6
ROPE_BASE = 10000.0
NEG_INF = -1e30
LOG2_E = 1.4426950408889634
PIN_SPLIT = 64.0

A_HEADS = 8
A_KV_HEADS = 2
A_HEAD_DIM = 64
WINDOW = 128
A_BLOCK = 128
B_WIDTH = 512
B_GROUPS = 4
B_CHUNK = 128
C_HEADS = 8
C_Q_LORA = 384
C_KV_LORA = 256
C_NOPE = 64
C_ROPE = 32
C_V = 64
D_WIDTH = 512
D_CONV = 3
N_EXPERTS = 16
EXPERT_FF = 1024
EC_FACTOR = 2

A_Q_W = A_HEADS * A_HEAD_DIM
A_KV_W = A_KV_HEADS * A_HEAD_DIM

LANES = 128
SUBLANES = 8
ROW_TILE = 256
SAMPLES_PER_STEP = 2
MOD_ROWS = 8
HP_CHUNKS = D_MODEL // (2 * LANES)
ACC_CHUNKS = D_MODEL // LANES
VMEM_LIMIT = 56 * 1024 * 1024

EVEN_COLS = 2 * A_KV_W + 2 * A_KV_W + A_Q_W + 2 * B_WIDTH
ODD_COLS = C_KV_LORA + LANES + C_Q_LORA + 3 * D_WIDTH
HEAD_PAD = LANES


def _cparams(sem):
    return pltpu.CompilerParams(dimension_semantics=sem, vmem_limit_bytes=VMEM_LIMIT)


def _nt_dot(a, b, precision=None):
    return lax.dot_general(a, b, (((1,), (1,)), ((), ())), precision=precision,
                           preferred_element_type=F32)


def _tn_dot(a, b, precision=None):
    return lax.dot_general(a, b, (((0,), (0,)), ((), ())), precision=precision,
                           preferred_element_type=F32)


def _mod_kernel(cc_ref, w_ref, b_ref, o_ref):
    cc = cc_ref[...]
    s = cc * jax.nn.sigmoid(cc)
    o_ref[0] = jnp.dot(s, w_ref[0], precision=HIGHEST, preferred_element_type=F32) + b_ref[0]


def _modulation(cc, mod_w, mod_b):
    rows = cc.shape[0]
    ncol = mod_w.shape[2] // D_MODEL
    return pl.pallas_call(
        _mod_kernel,
        grid=(DEPTH, ncol),
        in_specs=[
            pl.BlockSpec((rows, D_MODEL), lambda l, j: (0, 0)),
            pl.BlockSpec((1, D_MODEL, D_MODEL), lambda l, j: (l, 0, j)),
            pl.BlockSpec((1, 1, D_MODEL), lambda l, j: (l, 0, j)),
        ],
        out_specs=pl.BlockSpec((1, rows, D_MODEL), lambda l, j: (l, 0, j)),
        out_shape=jax.ShapeDtypeStruct((DEPTH, rows, mod_w.shape[2]), F32),
        compiler_params=_cparams(("arbitrary", "arbitrary")),
        name="adaln_modulation",
    )(cc, mod_w, mod_b.reshape(DEPTH, 1, -1))


def _residual_from_moe(x, moe_ref, s, gate_row, tm):
    cols = []
    for j in range(ACC_CHUNKS):
        chunk = moe_ref[s, pl.ds(j, tm, stride=ACC_CHUNKS), :]
        sl = slice(j * LANES, (j + 1) * LANES)
        cols.append(x[:, sl] + gate_row[:, sl] * chunk)
    return jnp.concatenate(cols, axis=1)


def _rms_mod(x, g, shift, scale):
    gain = g * (1.0 + scale)
    return x * lax.rsqrt(jnp.mean(x * x, axis=-1, keepdims=True) + NORM_EPS) * gain + shift


def _rms(x, g):
    return x * lax.rsqrt(jnp.mean(x * x, axis=-1, keepdims=True) + NORM_EPS) * g


def _rope_pairs(a, cos, sin, half, first):
    rot = jnp.where(first, pltpu.roll(a, LANES - half, 1), pltpu.roll(a, half, 1))
    return a * cos + rot * sin


def _in_even_kernel(has_moe, tm, *refs):
    if has_moe:
        (x_ref, moe_ref, modp_ref, mod_ref, g_ref, w_ref, cos_ref, sin_ref, lng_ref,
         xo_ref, q_ref, kd_ref, vd_ref, u_ref, vn_ref) = refs
    else:
        (x_ref, mod_ref, g_ref, w_ref, cos_ref, sin_ref, lng_ref,
         q_ref, kd_ref, vd_ref, u_ref, vn_ref) = refs
    cos = cos_ref[...]
    sin = sin_ref[...]
    lane = lax.broadcasted_iota(jnp.int32, (tm, LANES), 1)
    first = (lane % A_HEAD_DIM) < (A_HEAD_DIM // 2)
    half = A_HEAD_DIM // 2
    qoff = 4 * LANES
    uoff = qoff + A_Q_W
    scale = A_HEAD_DIM ** -0.5 * LOG2_E
    for s in range(x_ref.shape[0]):
        x = x_ref[s]
        if has_moe:
            x = _residual_from_moe(x, moe_ref, s, modp_ref[s, 0, 5:6, :], tm)
            xo_ref[s] = x
        h = _rms_mod(x, g_ref[...], mod_ref[s, 0, 0:1, :], mod_ref[s, 0, 1:2, :])
        z = jnp.dot(h.astype(BF16), w_ref[...], preferred_element_type=F32)
        for j in range(2):
            sl = slice(j * LANES, (j + 1) * LANES)
            kd_ref[s, :, sl] = _rope_pairs(z[:, sl], cos, sin, half, first).astype(BF16)
        vd_ref[s] = z[:, 2 * LANES:4 * LANES].astype(BF16)
        for j in range(A_Q_W // LANES):
            sl = slice(qoff + j * LANES, qoff + (j + 1) * LANES)
            q_ref[s, :, j * LANES:(j + 1) * LANES] = (
                _rope_pairs(z[:, sl], cos, sin, half, first) * scale).astype(BF16)
        u_ref[s] = jax.nn.gelu(z[:, uoff:uoff + B_WIDTH])
        gv = jax.nn.gelu(z[:, uoff + B_WIDTH:uoff + 2 * B_WIDTH])
        mu = jnp.mean(gv, axis=-1, keepdims=True)
        var = jnp.mean(jnp.square(gv - mu), axis=-1, keepdims=True)
        vn_ref[s] = ((gv - mu) * lax.rsqrt(var + NORM_EPS) * lng_ref[...]).astype(BF16)


def _samples_per_step(bsz, want):
    return want if bsz % want == 0 else SAMPLES_PER_STEP


def _tile_is_ctx(n_lat_tiles):
    return lambda i: jnp.where(i >= n_lat_tiles, 1, 0)


def _in_even(x, moe, modp, mod, g1, w_in, cos, sin, lng, n_lat):
    bsz, t, _ = x.shape
    tm = ROW_TILE
    nt = t // tm
    is_ctx = _tile_is_ctx(n_lat // tm)
    has_moe = moe is not None
    spb = _samples_per_step(bsz, 2 * SAMPLES_PER_STEP)
    row = lambda w: pl.BlockSpec((spb, tm, w), lambda b, i: (b, i, 0))
    mod_spec = pl.BlockSpec((spb, 1, MOD_ROWS, D_MODEL), lambda b, i: (b, is_ctx(i), 0, 0))
    full2 = lambda a: pl.BlockSpec(a.shape, lambda b, i: (0, 0))
    in_specs = [row(D_MODEL)]
    args = [x]
    if has_moe:
        in_specs += [pl.BlockSpec((spb, tm * ACC_CHUNKS, LANES), lambda b, i: (b, i, 0)), mod_spec]
        args += [moe, modp]
    in_specs += [mod_spec, full2(g1), full2(w_in),
                 pl.BlockSpec((tm, LANES), lambda b, i: (i, 0)),
                 pl.BlockSpec((tm, LANES), lambda b, i: (i, 0)), full2(lng)]
    args += [mod, g1, w_in, cos, sin, lng]
    out_specs = []
    out_shape = []
    if has_moe:
        out_specs.append(row(D_MODEL))
        out_shape.append(jax.ShapeDtypeStruct((bsz, t, D_MODEL), F32))
    out_specs += [row(A_Q_W), row(2 * LANES), row(2 * LANES), row(B_WIDTH), row(B_WIDTH)]
    out_shape += [jax.ShapeDtypeStruct((bsz, t, A_Q_W), BF16),
                  jax.ShapeDtypeStruct((bsz, t, 2 * LANES), BF16),
                  jax.ShapeDtypeStruct((bsz, t, 2 * LANES), BF16),
                  jax.ShapeDtypeStruct((bsz, t, B_WIDTH), F32),
                  jax.ShapeDtypeStruct((bsz, t, B_WIDTH), BF16)]
    outs = pl.pallas_call(
        functools.partial(_in_even_kernel, has_moe, tm),
        grid=(bsz // spb, nt), in_specs=in_specs, out_specs=out_specs, out_shape=out_shape,
        compiler_params=_cparams(("parallel", "arbitrary")),
        name="in_proj_even",
    )(*args)
    if not has_moe:
        outs = [x] + list(outs)
    return outs


def _win_attn_kernel(sink_ref, bias_ref, q_ref, kp_ref, kc_ref, kn_ref, kx_ref,
                     vp_ref, vc_ref, vn_ref, vx_ref, o_ref):
    blk = A_BLOCK
    g = A_HEADS // A_KV_HEADS
    nwin = 3 * blk
    nkeys = nwin + kx_ref.shape[1]
    bias = jnp.concatenate([bias_ref[0]] * g, axis=0)
    ones_col = jnp.where(lax.broadcasted_iota(jnp.int32, (nkeys, LANES), 1) == 0, 1.0, 0.0).astype(BF16)
    lane = lax.broadcasted_iota(jnp.int32, (blk, LANES), 1)
    low = lane < A_HEAD_DIM
    zero = jnp.zeros((), BF16)
    for smp, h in [(a, b) for a in range(q_ref.shape[0]) for b in range(A_KV_HEADS)]:
        hs = slice(h * LANES, (h + 1) * LANES)
        qs = []
        for cidx in range(g // 2):
            qc = q_ref[smp, :, (h * (g // 2) + cidx) * LANES:(h * (g // 2) + cidx + 1) * LANES]
            qs.append(jnp.where(low, qc, zero))
            qs.append(jnp.where(low, zero, qc))
        qh = jnp.concatenate(qs, axis=0)
        kcat = jnp.concatenate([kp_ref[smp, :, hs], kc_ref[smp, :, hs], kn_ref[smp, :, hs],
                                kx_ref[smp, :, hs]], axis=0)
        vcat = jnp.concatenate([vp_ref[smp, :, hs], vc_ref[smp, :, hs], vn_ref[smp, :, hs],
                                vx_ref[smp, :, hs]], axis=0)
        s = _nt_dot(qh, kcat)
        s = jnp.concatenate([s[:, :nwin] + bias, s[:, nwin:]], axis=1)
        sink = jnp.concatenate(
            [jnp.full((blk, 1), sink_ref[h * g + j] * LOG2_E, F32) for j in range(g)], axis=0)
        m = jnp.maximum(jnp.max(s, axis=-1, keepdims=True), sink)
        p = jnp.exp2(s - m)
        pv = jnp.dot(p.astype(BF16), jnp.concatenate([vcat, ones_col], axis=1),
                     preferred_element_type=F32)
        den = pv[:, LANES:LANES + 1] + jnp.exp2(sink - m)
        o = pv[:, :LANES] * (1.0 / den)
        for cidx in range(g // 2):
            o_even = o[(2 * cidx) * blk:(2 * cidx + 1) * blk]
            o_odd = o[(2 * cidx + 1) * blk:(2 * cidx + 2) * blk]
            col = (h * (g // 2) + cidx) * LANES
            o_ref[smp, :, col:col + LANES] = jnp.where(low, o_even, o_odd).astype(BF16)


def _win_attn(sink, q, kd, vd, n_lat, n_ctx):
    bsz, t, _ = q.shape
    blk = A_BLOCK
    nb = t // blk
    nlb = n_lat // blk
    kvw = 2 * LANES
    spb = _samples_per_step(bsz, 2 * SAMPLES_PER_STEP)
    prev = pl.BlockSpec((spb, blk, kvw), lambda b, i: (b, jnp.maximum(i - 1, 0), 0))
    cur = pl.BlockSpec((spb, blk, kvw), lambda b, i: (b, i, 0))
    nxt = pl.BlockSpec((spb, blk, kvw), lambda b, i: (b, jnp.minimum(i + 1, nb - 1), 0))
    ctx = pl.BlockSpec((spb, n_ctx, kvw), lambda b, i: (b, t // n_ctx - 1, 0))
    r = jnp.arange(blk)[:, None]
    c = jnp.arange(blk)[None, :]
    yes = jnp.ones((blk, blk), jnp.bool_)
    no = jnp.zeros((blk, blk), jnp.bool_)
    kinds = [(c >= r, yes, c <= r), (no, yes, c <= r), (c >= r, yes, no), (no, no, no)]
    bias = jnp.stack([jnp.where(jnp.concatenate([p, m, n], axis=1), 0.0, NEG_INF)
                      for p, m, n in kinds]).astype(F32)
    kind = lambda i: jnp.where(i >= nlb, 3, jnp.where(i == 0, 1, jnp.where(i == nlb - 1, 2, 0)))
    return pl.pallas_call(
        _win_attn_kernel,
        grid=(bsz // spb, nb),
        in_specs=[pl.BlockSpec(memory_space=pltpu.SMEM),
                  pl.BlockSpec((1, blk, 3 * blk), lambda b, i: (kind(i), 0, 0)),
                  pl.BlockSpec((spb, blk, A_Q_W), lambda b, i: (b, i, 0)),
                  prev, cur, nxt, ctx, prev, cur, nxt, ctx],
        out_specs=pl.BlockSpec((spb, blk, A_Q_W), lambda b, i: (b, i, 0)),
        out_shape=jax.ShapeDtypeStruct((bsz, t, A_Q_W), BF16),
        compiler_params=_cparams(("parallel", "arbitrary")),
        name="window_attention",
    )(sink, bias, q, kd, kd, kd, kd, vd, vd, vd, vd)


def _out_epilogue(tm, s, y, x_ref, mod_ref, g2_ref, rw_ref, xo_ref, hp_ref, aff_ref):
    x = x_ref[s] + mod_ref[s, 0, 2:3, :] * y
    xo_ref[s] = x
    h2 = _rms_mod(x, g2_ref[...], mod_ref[s, 0, 3:4, :], mod_ref[s, 0, 4:5, :])
    h_hi = h2.astype(BF16)
    h_hi32 = h_hi.astype(F32)
    bits = lax.bitcast_convert_type(h_hi32, jnp.uint32)
    half = D_MODEL // 2
    packed = (bits[:, :half] >> 16) | (bits[:, half:] & jnp.uint32(0xFFFF0000))
    for j in range(HP_CHUNKS):
        hp_ref[s, pl.ds(j, tm, stride=HP_CHUNKS), :] = packed[:, j * LANES:(j + 1) * LANES]
    h_lo = (h2 - h_hi32).astype(BF16)
    logits = (_nt_dot(rw_ref[0], h_hi) + _nt_dot(rw_ref[0], h_lo)) + _nt_dot(rw_ref[1], h_hi)
    mx = jnp.max(logits, axis=0, keepdims=True)
    ex = jnp.exp(logits - mx)
    aff_ref[s] = ex / jnp.sum(ex, axis=0, keepdims=True)


def _out_even_kernel(tm, o_ref, u_ref, vn_ref, ws_ref, bs_ref, wo_ref, x_ref, mod_ref, g2_ref,
                     rw_ref, xo_ref, hp_ref, aff_ref):
    dot = functools.partial(jnp.dot, preferred_element_type=F32)
    gw = B_WIDTH // B_GROUPS
    for s in range(x_ref.shape[0]):
        rows = []
        for cidx in range(tm // B_CHUNK):
            rs = slice(cidx * B_CHUNK, (cidx + 1) * B_CHUNK)
            cols = []
            for g in range(B_GROUPS):
                cs = slice(g * gw, (g + 1) * gw)
                mixed = dot(ws_ref[g], vn_ref[s, rs, cs]) + bs_ref[:, cs]
                cols.append(u_ref[s, rs, cs] * mixed)
            rows.append(jnp.concatenate(cols, axis=1))
        sg = jnp.concatenate(rows, axis=0)
        y = dot(o_ref[s], wo_ref[:A_Q_W, :]) + dot(sg.astype(BF16), wo_ref[A_Q_W:, :])
        _out_epilogue(tm, s, y, x_ref, mod_ref, g2_ref, rw_ref, xo_ref, hp_ref, aff_ref)


def _out_specs_common(bsz, t, tm, spb):
    row = lambda w: pl.BlockSpec((spb, tm, w), lambda b, i: (b, i, 0))
    out_specs = [row(D_MODEL),
                 pl.BlockSpec((spb, tm * HP_CHUNKS, LANES), lambda b, i: (b, i, 0)),
                 pl.BlockSpec((spb, N_EXPERTS, tm), lambda b, i: (b, 0, i))]
    out_shape = [jax.ShapeDtypeStruct((bsz, t, D_MODEL), F32),
                 jax.ShapeDtypeStruct((bsz, t * HP_CHUNKS, LANES), jnp.uint32),
                 jax.ShapeDtypeStruct((bsz, N_EXPERTS, t), F32)]
    return out_specs, out_shape


def _out_even(o, u, vn, ws, bs, wo, x, mod, g2, rw_t, n_lat):
    bsz, t, _ = x.shape
    tm = ROW_TILE
    is_ctx = _tile_is_ctx(n_lat // tm)
    spb = _samples_per_step(bsz, 2 * SAMPLES_PER_STEP)
    row = lambda w: pl.BlockSpec((spb, tm, w), lambda b, i: (b, i, 0))
    full = lambda a: pl.BlockSpec(a.shape, lambda b, i: (0,) * a.ndim)
    mod_spec = pl.BlockSpec((spb, 1, MOD_ROWS, D_MODEL), lambda b, i: (b, is_ctx(i), 0, 0))
    out_specs, out_shape = _out_specs_common(bsz, t, tm, spb)
    return pl.pallas_call(
        functools.partial(_out_even_kernel, tm),
        grid=(bsz // spb, t // tm),
        in_specs=[row(A_Q_W), row(B_WIDTH), row(B_WIDTH), full(ws), full(bs), full(wo),
                  row(D_MODEL), mod_spec, full(g2), full(rw_t)],
        out_specs=out_specs, out_shape=out_shape,
        compiler_params=_cparams(("parallel", "arbitrary")),
        name="out_proj_even",
    )(o, u, vn, ws, bs, wo, x, mod, g2, rw_t)


def _in_odd_kernel(has_moe, tm, *refs):
    (x_ref, moe_ref, modp_ref, mod_ref, g_ref, w_ref, cos_ref, sin_ref, qg_ref, kvg_ref,
     wuq_ref, wukv_ref, xo_ref, q_ref, k_ref, v_ref, gb_ref, u_ref) = refs
    cos = cos_ref[...]
    sin = sin_ref[...]
    lane = lax.broadcasted_iota(jnp.int32, (tm, LANES), 1)
    half = C_ROPE // 2
    first = lane < C_NOPE + half
    cqo = C_KV_LORA + LANES
    co = cqo + C_Q_LORA
    scale = (C_NOPE + C_ROPE) ** -0.5 * LOG2_E
    ones_col = jnp.where(lane == C_V, 1.0, 0.0)
    voff = C_HEADS * HEAD_PAD
    for s in range(x_ref.shape[0]):
        x = _residual_from_moe(x_ref[s], moe_ref, s, modp_ref[s, 0, 5:6, :], tm)
        xo_ref[s] = x
        h = _rms_mod(x, g_ref[...], mod_ref[s, 0, 0:1, :], mod_ref[s, 0, 1:2, :])
        z = jnp.dot(h.astype(BF16), w_ref[...], preferred_element_type=F32)
        ckv = _rms(z[:, :C_KV_LORA], kvg_ref[...])
        kr = _rope_pairs(z[:, C_KV_LORA:C_KV_LORA + LANES], cos, sin, half, first)
        cq = _rms(z[:, cqo:cqo + C_Q_LORA], qg_ref[...])
        kv = jnp.dot(ckv.astype(BF16), wukv_ref[...], preferred_element_type=F32)
        qq = jnp.dot(cq.astype(BF16), wuq_ref[...], preferred_element_type=F32)
        for hd in range(C_HEADS):
            sl = slice(hd * HEAD_PAD, (hd + 1) * HEAD_PAD)
            q_ref[s, :, sl] = (_rope_pairs(qq[:, sl], cos, sin, half, first) * scale).astype(BF16)
            k_ref[s, :, sl] = (kv[:, sl] + kr).astype(BF16)
            v_ref[s, :, sl] = (kv[:, voff + hd * HEAD_PAD:voff + (hd + 1) * HEAD_PAD]
                               + ones_col).astype(BF16)
        gb_ref[s] = z[:, co:co + D_WIDTH]
        u_ref[s] = z[:, co + D_WIDTH:co + 2 * D_WIDTH] * z[:, co + 2 * D_WIDTH:co + 3 * D_WIDTH]


def _in_odd(x, moe, modp, mod, g1, w_in, cos, sin, qg, kvg, wuq, wukv, n_lat):
    bsz, t, _ = x.shape
    tm = ROW_TILE
    is_ctx = _tile_is_ctx(n_lat // tm)
    spb = SAMPLES_PER_STEP
    row = lambda w: pl.BlockSpec((spb, tm, w), lambda b, i: (b, i, 0))
    mod_spec = pl.BlockSpec((spb, 1, MOD_ROWS, D_MODEL), lambda b, i: (b, is_ctx(i), 0, 0))
    full2 = lambda a: pl.BlockSpec(a.shape, lambda b, i: (0, 0))
    hw = C_HEADS * HEAD_PAD
    return pl.pallas_call(
        functools.partial(_in_odd_kernel, True, tm),
        grid=(bsz // spb, t // tm),
        in_specs=[row(D_MODEL),
                  pl.BlockSpec((spb, tm * ACC_CHUNKS, LANES), lambda b, i: (b, i, 0)),
                  mod_spec, mod_spec, full2(g1), full2(w_in),
                  pl.BlockSpec((tm, LANES), lambda b, i: (i, 0)),
                  pl.BlockSpec((tm, LANES), lambda b, i: (i, 0)),
                  full2(qg), full2(kvg), full2(wuq), full2(wukv)],
        out_specs=[row(D_MODEL), row(hw), row(hw), row(hw), row(D_WIDTH), row(D_WIDTH)],
        out_shape=[jax.ShapeDtypeStruct((bsz, t, D_MODEL), F32),
                   jax.ShapeDtypeStruct((bsz, t, hw), BF16),
                   jax.ShapeDtypeStruct((bsz, t, hw), BF16),
                   jax.ShapeDtypeStruct((bsz, t, hw), BF16),
                   jax.ShapeDtypeStruct((bsz, t, D_WIDTH), F32),
                   jax.ShapeDtypeStruct((bsz, t, D_WIDTH), F32)],
        compiler_params=_cparams(("parallel", "arbitrary")),
        name="in_proj_odd",
    )(x, moe, modp, mod, g1, w_in, cos, sin, qg, kvg, wuq, wukv)


def _dense_attn_kernel(tk, q_ref, k_ref, v_ref, o_ref):
    q = q_ref[0]
    tq = q.shape[0]
    nk = k_ref.shape[1] // tk

    m = jnp.full((tq, 1), NEG_INF, F32)
    acc = jnp.zeros((tq, HEAD_PAD), F32)
    for j in range(nk):
        s = _nt_dot(q, k_ref[0, j * tk:(j + 1) * tk, :])
        m_new = jnp.maximum(m, jnp.max(s, axis=-1, keepdims=True))
        alpha = jnp.exp2(m - m_new)
        p = jnp.exp2(s - m_new)
        acc = alpha * acc + jnp.dot(p.astype(BF16), v_ref[0, j * tk:(j + 1) * tk, :],
                                    preferred_element_type=F32)
        m = m_new
    o_ref[0] = (acc / acc[:, C_V:C_V + 1]).astype(BF16)


def _dense_attn(q, k, v, q_start, q_len, k_start, k_len, tq):
    bsz, _, hw = q.shape
    tk = ROW_TILE
    q0 = q_start // tq
    kb = k_start // k_len
    return pl.pallas_call(
        functools.partial(_dense_attn_kernel, tk),
        grid=(bsz, C_HEADS, q_len // tq),
        in_specs=[pl.BlockSpec((1, tq, HEAD_PAD), lambda b, h, i: (b, q0 + i, h)),
                  pl.BlockSpec((1, k_len, HEAD_PAD), lambda b, h, i: (b, kb, h)),
                  pl.BlockSpec((1, k_len, HEAD_PAD), lambda b, h, i: (b, kb, h))],
        out_specs=pl.BlockSpec((1, tq, HEAD_PAD), lambda b, h, i: (b, i, h)),
        out_shape=jax.ShapeDtypeStruct((bsz, q_len, hw), BF16),
        compiler_params=_cparams(("parallel", "parallel", "arbitrary")),
        name="dense_attention",
    )(q, k, v)


def _out_odd_kernel(tm, n_lat_tiles, n_tiles, ol_ref, oc_ref, gb_ref, u_ref, up_ref, un_ref, cw_ref,
                    wo_ref, x_ref, mod_ref, g2_ref, rw_ref, xo_ref, hp_ref, aff_ref):
    i = pl.program_id(1)
    dot = functools.partial(jnp.dot, preferred_element_type=F32)
    has_prev = jnp.logical_and(i != 0, i != n_lat_tiles)
    has_next = jnp.logical_and(i != n_lat_tiles - 1, i != n_tiles - 1)
    hw = C_HEADS * HEAD_PAD
    for s in range(x_ref.shape[0]):
        o_att = jnp.where(i < n_lat_tiles, ol_ref[s], oc_ref[s])
        u = u_ref[s]
        prev_row = jnp.where(has_prev, up_ref[s, SUBLANES - 1:SUBLANES, :], 0.0)
        next_row = jnp.where(has_next, un_ref[s, 0:1, :], 0.0)
        ridx = lax.broadcasted_iota(jnp.int32, u.shape, 0)
        u_m1 = jnp.where(ridx == 0, prev_row, pltpu.roll(u, 1, 0))
        u_p1 = jnp.where(ridx == tm - 1, next_row, pltpu.roll(u, tm - 1, 0))
        conv = u_m1 * cw_ref[0:1, :] + u * cw_ref[1:2, :] + u_p1 * cw_ref[2:3, :]
        c = gb_ref[s] * conv
        y = dot(o_att, wo_ref[:hw, :]) + dot(c.astype(BF16), wo_ref[hw:, :])
        _out_epilogue(tm, s, y, x_ref, mod_ref, g2_ref, rw_ref, xo_ref, hp_ref, aff_ref)


def _out_odd(o_lat, o_ctx, gb, u, cw, wo, x, mod, g2, rw_t, n_lat):
    bsz, t, _ = x.shape
    tm = ROW_TILE
    nt = t // tm
    nlt = n_lat // tm
    is_ctx = _tile_is_ctx(nlt)
    spb = _samples_per_step(bsz, 2 * SAMPLES_PER_STEP)
    row = lambda w: pl.BlockSpec((spb, tm, w), lambda b, i: (b, i, 0))
    full = lambda a: pl.BlockSpec(a.shape, lambda b, i: (0,) * a.ndim)
    mod_spec = pl.BlockSpec((spb, 1, MOD_ROWS, D_MODEL), lambda b, i: (b, is_ctx(i), 0, 0))
    hw = C_HEADS * HEAD_PAD
    per = tm // SUBLANES
    last = t // SUBLANES - 1
    halo_prev = pl.BlockSpec((spb, SUBLANES, D_WIDTH), lambda b, i: (b, jnp.maximum(i * per - 1, 0), 0))
    halo_next = pl.BlockSpec((spb, SUBLANES, D_WIDTH), lambda b, i: (b, jnp.minimum((i + 1) * per, last), 0))
    out_specs, out_shape = _out_specs_common(bsz, t, tm, spb)
    return pl.pallas_call(
        functools.partial(_out_odd_kernel, tm, nlt, nt),
        grid=(bsz // spb, nt),
        in_specs=[pl.BlockSpec((spb, tm, hw), lambda b, i: (b, jnp.minimum(i, nlt - 1), 0)),
                  pl.BlockSpec((spb, tm, hw), lambda b, i: (b, jnp.maximum(i - nlt, 0), 0)),
                  row(D_WIDTH), row(D_WIDTH), halo_prev, halo_next,
                  full(cw), full(wo), row(D_MODEL), mod_spec, full(g2), full(rw_t)],
        out_specs=out_specs, out_shape=out_shape,
        compiler_params=_cparams(("parallel", "arbitrary")),
        name="out_proj_odd",
    )(o_lat, o_ctx, gb, u, u, u, cw, wo, x, mod, g2, rw_t)


def _route_kernel(cap, nblk, width, a_ref, idx_ref, gate_ref):
    n_e = N_EXPERTS
    a = a_ref[0]
    bits = lax.bitcast_convert_type(a, jnp.int32)

    def count(mask):
        c = jnp.sum(jnp.where(mask, 1.0, 0.0), axis=2, keepdims=True)
        return jnp.sum(c, axis=1, keepdims=True)

    def search(it, thr):
        cand = thr | jnp.left_shift(jnp.int32(1), 30 - it)
        return jnp.where(count(bits >= cand) >= cap, cand, thr)

    thr = lax.fori_loop(0, 31, search, jnp.zeros((n_e, 1, 1), jnp.int32))
    gt = bits > thr
    eq = bits == thr
    need = cap - count(gt)

    rows = n_e * nblk
    li = lax.broadcasted_iota(jnp.int32, (width, width), 0)
    lj = lax.broadcasted_iota(jnp.int32, (width, width), 1)
    upper = jnp.where(li <= lj, 1.0, 0.0).astype(BF16)
    ones_sq = jnp.ones((width, width), BF16)
    if nblk > 1:
        ri = lax.broadcasted_iota(jnp.int32, (rows, rows), 0)
        rj = lax.broadcasted_iota(jnp.int32, (rows, rows), 1)
        lower = jnp.where(jnp.logical_and(ri // nblk == rj // nblk, rj < ri), 1.0, 0.0).astype(BF16)

    def prefix(mask3):
        m2 = jnp.where(mask3, 1.0, 0.0).astype(BF16).reshape(rows, width)
        local = jnp.dot(m2, upper, preferred_element_type=F32)
        total = jnp.dot(m2, ones_sq, preferred_element_type=F32)
        if nblk > 1:
            excl = jnp.dot(lower, total.astype(BF16), preferred_element_type=F32)
        else:
            excl = jnp.zeros_like(total)
        return local, total, excl

    l_eq, _, x_eq = prefix(eq)
    tie_rank = (l_eq + x_eq).reshape(n_e, nblk, width)
    sel = jnp.logical_or(gt, jnp.logical_and(eq, tie_rank <= need))
    local, total, excl = prefix(sel)
    pin = local + excl

    slot_l = lax.broadcasted_iota(jnp.int32, (nblk, cap), 1).astype(F32)
    blk_s = lax.broadcasted_iota(jnp.int32, (nblk, cap), 0).astype(F32)
    slot_s = lax.broadcasted_iota(jnp.int32, (cap, width), 0).astype(F32)
    lane_w = lax.broadcasted_iota(jnp.int32, (cap, width), 1).astype(F32)
    ones_r = jnp.ones((SUBLANES, width), BF16)
    reps = cap // width if cap >= width else 1
    a2 = a.reshape(rows, width)
    if nblk > 1:
        pin_hi = jnp.floor(pin * (1.0 / PIN_SPLIT))
        a_1 = a2.astype(BF16)
        rem = a2 - a_1.astype(F32)
        a_2 = rem.astype(BF16)
        a_3 = (rem - a_2.astype(F32)).astype(BF16)
        table = jnp.concatenate([pin_hi.astype(BF16), (pin - PIN_SPLIT * pin_hi).astype(BF16),
                                 a_1, a_2, a_3], axis=1)
    for e in range(n_e):
        rs = slice(e * nblk, (e + 1) * nblk)
        if nblk > 1:
            lo = jnp.concatenate([excl[rs]] * reps, axis=1)[:, :cap]
            hi = lo + jnp.concatenate([total[rs]] * reps, axis=1)[:, :cap]
            oh_t = jnp.where(jnp.logical_and(lo <= slot_l, slot_l < hi), 1.0, 0.0)
            got = _tn_dot(oh_t.astype(BF16), table[rs])
            pin_g = PIN_SPLIT * got[:, :width] + got[:, width:2 * width]
            aff_g = (got[:, 2 * width:3 * width] + got[:, 3 * width:4 * width]) + got[:, 4 * width:]
            blk_row = jnp.sum(oh_t * blk_s, axis=0, keepdims=True)
        else:
            pin_g = jnp.broadcast_to(pin[rs], (cap, width))
            aff_g = jnp.broadcast_to(a2[rs], (cap, width))
            blk_row = jnp.zeros((1, cap), F32)
        ind = jnp.where(pin_g <= slot_s, 1.0, 0.0).astype(BF16)
        cnt_row = _nt_dot(ones_r, ind)[0:1]
        cnt_b = jnp.dot(ind, ones_sq, preferred_element_type=F32)
        picked = jnp.where(lane_w == cnt_b, aff_g, 0.0)
        idx_ref[0, e] = (blk_row * width + cnt_row).astype(jnp.int32)
        gate_ref[0, e] = jnp.sum(picked, axis=1, keepdims=True)


def _route(aff, cap):
    bsz, n_e, n = aff.shape
    width = LANES if n % (LANES * SUBLANES) == 0 else n
    nblk = n // width
    a4 = aff.reshape(bsz, n_e, nblk, width)
    idx, gate = pl.pallas_call(
        functools.partial(_route_kernel, cap, nblk, width),
        grid=(bsz,),
        in_specs=[pl.BlockSpec((1, n_e, nblk, width), lambda b: (b, 0, 0, 0))],
        out_specs=[pl.BlockSpec((1, n_e, 1, cap), lambda b: (b, 0, 0, 0)),
                   pl.BlockSpec((1, n_e, cap, 1), lambda b: (b, 0, 0, 0))],
        out_shape=[jax.ShapeDtypeStruct((bsz, n_e, 1, cap), jnp.int32),
                   jax.ShapeDtypeStruct((bsz, n_e, cap, 1), F32)],
        compiler_params=_cparams(("parallel",)),
        name="expert_choice_routing",
    )(a4)
    return idx.reshape(bsz, n_e, cap), gate.reshape(bsz, n_e, cap)


def _ffn_copies(hp_hbm, out_hbm, hsrc, acc, sem_in, sem_out, grp, nsub, n_per, row0):
    ins = []
    outs = []
    for sb in range(nsub):
        b = grp * nsub + sb
        ins.append(pltpu.make_async_copy(
            hp_hbm.at[b, pl.ds(row0 * HP_CHUNKS, n_per * HP_CHUNKS), :],
            hsrc.at[pl.ds(sb * n_per * HP_CHUNKS, n_per * HP_CHUNKS), :], sem_in.at[sb]))
        outs.append(pltpu.make_async_copy(
            acc.at[pl.ds(sb * n_per * ACC_CHUNKS, n_per * ACC_CHUNKS), :],
            out_hbm.at[b, pl.ds(row0 * ACC_CHUNKS, n_per * ACC_CHUNKS), :], sem_out.at[sb]))
    return ins, outs


def _ffn_gather(idx_ref, hsrc, xt, stride, mi):
    src = pl.multiple_of(idx_ref[0, 0, 0, mi] * HP_CHUNKS, HP_CHUNKS)
    xt[pl.ds(mi, HP_CHUNKS, stride=stride), :] = hsrc[pl.ds(src, HP_CHUNKS), :]


def _ffn_scatter(idx_ref, gate_ref, gate_on, acc, yt, stride, mis):
    dst = []
    val = []
    for mi in mis:
        d = pl.multiple_of(idx_ref[0, 0, 0, mi] * ACC_CHUNKS, ACC_CHUNKS)
        gate = gate_ref[0, 0, 0, mi]
        if gate_on is not None:
            gate = jnp.where(gate_on, gate, 0.0)
        slab = yt[pl.ds(mi, ACC_CHUNKS, stride=stride), :] * gate
        dst.append(d)
        val.append(acc[pl.ds(d, ACC_CHUNKS), :] + slab)
    for d, v in zip(dst, val):
        acc[pl.ds(d, ACC_CHUNKS), :] = v


def _ffn_kernel(m_slots, nsub, n_per, row0, unroll, idx_ref, idxn_ref, idxp_ref, gate_ref, gatep_ref,
                hp_hbm, w1a_ref, w1b_ref, w3a_ref, w3b_ref, w2a_ref, w2b_ref, _moe_in, out_hbm,
                hsrc, acc, xt, xs_buf, act_buf, yt, sem_in, sem_out):
    grp = pl.program_id(0)
    e = pl.program_id(1)
    stride = m_slots + SUBLANES
    ins, outs = _ffn_copies(hp_hbm, out_hbm, hsrc, acc, sem_in, sem_out, grp, nsub, n_per, row0)

    @pl.when(jnp.logical_and(grp == 0, e == 0))
    def _():
        yt[...] = jnp.zeros(yt.shape, F32)

    @pl.when(e == 0)
    def _():
        for cp in ins:
            cp.start()
        acc[...] = jnp.zeros(acc.shape, F32)
        for cp in ins:
            cp.wait()

        def first_gather(c, carry):
            for k in range(unroll):
                _ffn_gather(idx_ref, hsrc, xt, stride, c * unroll + k)
            return carry

        lax.fori_loop(0, m_slots // unroll, first_gather, 0)

    half = D_MODEL // 2
    for j in range(HP_CHUNKS):
        w = xt[pl.ds(j * stride, m_slots), :]
        xs_buf[:, j * LANES:(j + 1) * LANES] = lax.bitcast_convert_type(w << 16, F32).astype(BF16)
        xs_buf[:, half + j * LANES:half + (j + 1) * LANES] = lax.bitcast_convert_type(
            w & jnp.uint32(0xFFFF0000), F32).astype(BF16)

    for c in range(m_slots // unroll):
        _ffn_scatter(idxp_ref, gatep_ref, e > 0, acc, yt, stride, range(c * unroll, (c + 1) * unroll))
    for mi in range(m_slots):
        _ffn_gather(idxn_ref, hsrc, xt, stride, mi)

    dot = functools.partial(jnp.dot, preferred_element_type=F32)
    xs = xs_buf[...]
    fh = EXPERT_FF // 2
    for j, (wa_ref, wb_ref) in enumerate(((w1a_ref, w3a_ref), (w1b_ref, w3b_ref))):
        a = dot(xs, wa_ref[0, 0])
        b = dot(xs, wb_ref[0, 0])
        act_buf[:, j * fh:(j + 1) * fh] = (a * jax.nn.sigmoid(a) * b).astype(BF16)
    seen = xt[pl.ds(0, SUBLANES), :]
    for r in range(SUBLANES, m_slots, SUBLANES):
        seen = seen | xt[pl.ds(r, SUBLANES), :]
    seen = seen | lax.bitcast_convert_type(acc[pl.ds(0, SUBLANES), :], jnp.uint32)
    zero = lax.bitcast_convert_type(seen >> 32, F32)
    tile = (2 * SUBLANES, LANES)
    act_buf[:tile[0], :tile[1]] = (act_buf[:tile[0], :tile[1]].astype(F32)
                                   + jnp.concatenate([zero, zero], axis=0)).astype(BF16)
    y = dot(act_buf[:, :fh], w2a_ref[0, 0]) + dot(act_buf[:, fh:], w2b_ref[0, 0])
    for j in range(ACC_CHUNKS):
        yt[pl.ds(j * stride, m_slots), :] = y[:, j * LANES:(j + 1) * LANES]

    @pl.when(e == N_EXPERTS - 1)
    def _():
        def last_scatter(c, carry):
            _ffn_scatter(idx_ref, gate_ref, None, acc, yt, stride,
                         [c * unroll + k for k in range(unroll)])
            return carry

        lax.fori_loop(0, m_slots // unroll, last_scatter, 0)
        for cp in outs:
            cp.start()
        for cp in outs:
            cp.wait()


def _expert_ffn(idx, gate, hp, w1, w3, w2, layer, moe_buf, nsub, n_per, row0):
    groups, n_e, m_slots = idx.shape
    n_tok = nsub * n_per
    stride = m_slots + SUBLANES
    unroll = 8
    cur = lambda g, e: (g, e, 0, 0)
    nxt = lambda g, e: (g, jnp.minimum(e + 1, n_e - 1), 0, 0)
    prv = lambda g, e: (g, jnp.maximum(e - 1, 0), 0, 0)
    smem = lambda imap: pl.BlockSpec((1, 1, 1, m_slots), imap, memory_space=pltpu.SMEM)
    fh = EXPERT_FF // 2
    w_in = lambda j: pl.BlockSpec((1, 1, D_MODEL, fh), lambda g, e: (layer, e, 0, j))
    w_out = lambda j: pl.BlockSpec((1, 1, fh, D_MODEL), lambda g, e: (layer, e, j, 0))
    idx4 = idx.reshape(groups, n_e, 1, m_slots)
    gate4 = gate.reshape(groups, n_e, 1, m_slots)
    return pl.pallas_call(
        functools.partial(_ffn_kernel, m_slots, nsub, n_per, row0, unroll),
        grid=(groups, n_e),
        in_specs=[smem(cur), smem(nxt), smem(prv), smem(cur), smem(prv),
                  pl.BlockSpec(memory_space=pl.ANY), w_in(0), w_in(1), w_in(0), w_in(1),
                  w_out(0), w_out(1), pl.BlockSpec(memory_space=pl.ANY)],
        out_specs=pl.BlockSpec(memory_space=pl.ANY),
        out_shape=jax.ShapeDtypeStruct(moe_buf.shape, F32),
        scratch_shapes=[pltpu.VMEM((n_tok * HP_CHUNKS, LANES), jnp.uint32),
                        pltpu.VMEM((n_tok * ACC_CHUNKS, LANES), F32),
                        pltpu.VMEM((HP_CHUNKS * stride, LANES), jnp.uint32),
                        pltpu.VMEM((m_slots, D_MODEL), BF16),
                        pltpu.VMEM((m_slots, EXPERT_FF), BF16),
                        pltpu.VMEM((ACC_CHUNKS * stride, LANES), F32),
                        pltpu.SemaphoreType.DMA((nsub,)),
                        pltpu.SemaphoreType.DMA((nsub,))],
        input_output_aliases={12: 0},
        compiler_params=_cparams(("arbitrary", "arbitrary")),
        name="expert_ffn",
    )(idx4, idx4, idx4, gate4, gate4, hp, w1, w1, w3, w3, w2, w2, moe_buf)


def _final_kernel(tm, x_ref, moe_ref, modp_ref, g_ref, o_ref):
    x = _residual_from_moe(x_ref[0], moe_ref, 0, modp_ref[0, 0, 5:6, :], tm)
    o_ref[0] = _rms(x, g_ref[...])


def _final(x, moe, modp, g, n_lat):
    bsz = x.shape[0]
    tm = ROW_TILE
    return pl.pallas_call(
        functools.partial(_final_kernel, tm),
        grid=(bsz, n_lat // tm),
        in_specs=[pl.BlockSpec((1, tm, D_MODEL), lambda b, i: (b, i, 0)),
                  pl.BlockSpec((1, tm * ACC_CHUNKS, LANES), lambda b, i: (b, i, 0)),
                  pl.BlockSpec((1, 1, MOD_ROWS, D_MODEL), lambda b, i: (b, 0, 0, 0)),
                  pl.BlockSpec(g.shape, lambda b, i: (0, 0))],
        out_specs=pl.BlockSpec((1, tm, D_MODEL), lambda b, i: (b, i, 0)),
        out_shape=jax.ShapeDtypeStruct((bsz, n_lat, D_MODEL), F32),
        compiler_params=_cparams(("parallel", "arbitrary")),
        name="final_norm",
    )(x, moe, modp, g)


def _rope_angles(rows, rot_dim):
    row = jnp.repeat(jnp.arange(rows, dtype=F32), GRID_W)
    col = jnp.tile(jnp.arange(GRID_W, dtype=F32), rows)
    n_freq = rot_dim // 4
    inv_freq = ROPE_BASE ** (-jnp.arange(n_freq, dtype=F32) / n_freq)
    ang = jnp.concatenate([row[:, None] * inv_freq[None, :], col[:, None] * inv_freq[None, :]], axis=-1)
    return jnp.cos(ang), jnp.sin(ang)


def _rope_tables_even(n_lat, n_ctx):
    cos, sin = _rope_angles(n_lat // GRID_W, A_HEAD_DIM)
    cos_h = jnp.concatenate([cos, cos], axis=1)
    sin_h = jnp.concatenate([-sin, sin], axis=1)
    reps = LANES // A_HEAD_DIM
    cos_l = jnp.tile(cos_h, (1, reps))
    sin_l = jnp.tile(sin_h, (1, reps))
    return (jnp.concatenate([cos_l, jnp.ones((n_ctx, LANES), F32)], axis=0),
            jnp.concatenate([sin_l, jnp.zeros((n_ctx, LANES), F32)], axis=0))


def _rope_tables_odd(n_lat, n_ctx):
    cos, sin = _rope_angles(n_lat // GRID_W, C_ROPE)
    n = cos.shape[0]
    pad = LANES - C_NOPE - C_ROPE
    cos_l = jnp.concatenate([jnp.ones((n, C_NOPE), F32), cos, cos, jnp.ones((n, pad), F32)], axis=1)
    sin_l = jnp.concatenate([jnp.zeros((n, C_NOPE), F32), -sin, sin, jnp.zeros((n, pad), F32)], axis=1)
    return (jnp.concatenate([cos_l, jnp.ones((n_ctx, LANES), F32)], axis=0),
            jnp.concatenate([sin_l, jnp.zeros((n_ctx, LANES), F32)], axis=0))


def _even_w_in(w):
    k = w[:, :A_KV_W]
    v = w[:, A_KV_W:2 * A_KV_W]
    dup = lambda a: jnp.concatenate(
        [a[:, h * A_HEAD_DIM:(h + 1) * A_HEAD_DIM] for h in range(A_KV_HEADS) for _ in range(2)], axis=1)
    return jnp.concatenate([dup(k), dup(v), w[:, 2 * A_KV_W:]], axis=1).astype(BF16)


def _odd_w_in(w):
    d = w.shape[0]
    kr = w[:, C_KV_LORA:C_KV_LORA + C_ROPE]
    kr_group = jnp.concatenate(
        [jnp.zeros((d, C_NOPE), F32), kr, jnp.zeros((d, LANES - C_NOPE - C_ROPE), F32)], axis=1)
    return jnp.concatenate([w[:, :C_KV_LORA], kr_group, w[:, C_KV_LORA + C_ROPE:]], axis=1).astype(BF16)


def _odd_w_uq(w):
    r = w.shape[0]
    w3 = w.reshape(r, C_HEADS, C_NOPE + C_ROPE)
    pad = jnp.zeros((r, C_HEADS, HEAD_PAD - C_NOPE - C_ROPE), F32)
    return jnp.concatenate([w3, pad], axis=2).reshape(r, C_HEADS * HEAD_PAD).astype(BF16)


def _odd_w_ukv(w):
    r = w.shape[0]
    w3 = w.reshape(r, C_HEADS, C_NOPE + C_V)
    kpad = jnp.zeros((r, C_HEADS, HEAD_PAD - C_NOPE), F32)
    vpad = jnp.zeros((r, C_HEADS, HEAD_PAD - C_V), F32)
    kpart = jnp.concatenate([w3[:, :, :C_NOPE], kpad], axis=2).reshape(r, C_HEADS * HEAD_PAD)
    vpart = jnp.concatenate([w3[:, :, C_NOPE:], vpad], axis=2).reshape(r, C_HEADS * HEAD_PAD)
    return jnp.concatenate([kpart, vpart], axis=1).astype(BF16)


def _odd_w_out(w):
    d = w.shape[1]
    att = w[:C_HEADS * C_V].reshape(C_HEADS, C_V, d)
    att = jnp.concatenate([att, jnp.zeros((C_HEADS, HEAD_PAD - C_V, d), F32)], axis=1)
    return jnp.concatenate([att.reshape(C_HEADS * HEAD_PAD, d), w[C_HEADS * C_V:]], axis=0).astype(BF16)


def _mod_table(mods_l, bsz):
    lat = mods_l[:bsz].reshape(bsz, 1, 6, D_MODEL)
    ctx = jnp.broadcast_to(mods_l[bsz].reshape(1, 1, 6, D_MODEL), (bsz, 1, 6, D_MODEL))
    tab = jnp.concatenate([lat, ctx], axis=1)
    return jnp.pad(tab, ((0, 0), (0, 0), (0, MOD_ROWS - 6), (0, 0)))


def kernel(x, c, ctx, c_ctx, mod_w, mod_b, norm1_g, norm2_g, ev_w_in, ev_sink, ev_sgu_norm_g, ev_sgu_w, ev_sgu_b, ev_w_out, od_w_in, od_q_norm_g, od_w_uq, od_kv_norm_g, od_w_ukv, od_conv_w, od_w_out, router_w, exp_w1, exp_w3, exp_w2, final_g):
    bsz, n_lat, _ = x.shape
    n_ctx = ctx.shape[1]
    t = n_lat + n_ctx
    cap_lat = EC_FACTOR * n_lat // N_EXPERTS
    cap_ctx = EC_FACTOR * n_ctx // N_EXPERTS

    mod_rows = -(-(bsz + 1) // SUBLANES) * SUBLANES
    cc = jnp.concatenate([c, c_ctx[None, :], jnp.zeros((mod_rows - bsz - 1, D_MODEL), F32)], axis=0)
    mods = _modulation(cc, mod_w, mod_b)
    tabs = [_mod_table(mods[l], bsz) for l in range(DEPTH)]

    cos_e, sin_e = _rope_tables_even(n_lat, n_ctx)
    cos_o, sin_o = _rope_tables_odd(n_lat, n_ctx)

    xs = jnp.concatenate([x, ctx], axis=1)
    w1 = exp_w1.astype(BF16)
    w3 = exp_w3.astype(BF16)
    w2 = exp_w2.astype(BF16)
    moe = None
    for layer in range(DEPTH):
        i = layer // 2
        need_ctx = layer < DEPTH - 1
        g1 = norm1_g[layer][None, :]
        g2 = norm2_g[layer][None, :]
        rw_f = router_w[layer].T
        rw_hi = rw_f.astype(BF16)
        rw_t = jnp.stack([rw_hi, (rw_f - rw_hi.astype(F32)).astype(BF16)])
        modp = tabs[layer - 1] if layer > 0 else None
        if layer % 2 == 0:
            xs, q, kd, vd, u, vn = _in_even(xs, moe, modp, tabs[layer], g1, _even_w_in(ev_w_in[i]),
                                            cos_e, sin_e, ev_sgu_norm_g[i][None, :], n_lat)
            o = _win_attn(ev_sink[i], q, kd, vd, n_lat, n_ctx)
            bs = jnp.repeat(ev_sgu_b[i].T, B_WIDTH // B_GROUPS, axis=1)
            xs, hp, aff = _out_even(o, u, vn, ev_sgu_w[i].astype(BF16), bs, ev_w_out[i].astype(BF16),
                                    xs, tabs[layer], g2, rw_t, n_lat)
        else:
            xs, q, k, v, gb, u = _in_odd(xs, moe, modp, tabs[layer], g1, _odd_w_in(od_w_in[i]),
                                         cos_o, sin_o, od_q_norm_g[i][None, :], od_kv_norm_g[i][None, :],
                                         _odd_w_uq(od_w_uq[i]), _odd_w_ukv(od_w_ukv[i]), n_lat)
            o_lat = _dense_attn(q, k, v, 0, n_lat, 0, t, 4 * ROW_TILE)
            o_ctx = _dense_attn(q, k, v, n_lat, n_ctx, n_lat, n_ctx, n_ctx)
            xs, hp, aff = _out_odd(o_lat, o_ctx, gb, u, od_conv_w[i], _odd_w_out(od_w_out[i]), xs, tabs[layer],
                                   g2, rw_t, n_lat)
        if moe is None:
            moe = jnp.zeros((bsz, t * ACC_CHUNKS, LANES), F32)
        idx, gate = _route(aff[:, :, :n_lat], cap_lat)
        moe = _expert_ffn(idx, gate, hp, w1, w3, w2, layer, moe, 1, n_lat, 0)
        if need_ctx:
            idx_c, gate_c = _route(aff[:, :, n_lat:], cap_ctx)
            idx_c = idx_c + (jnp.arange(bsz, dtype=jnp.int32) * n_ctx)[:, None, None]
            idx_c = jnp.transpose(idx_c, (1, 0, 2)).reshape(1, N_EXPERTS, bsz * cap_ctx)
            gate_c = jnp.transpose(gate_c, (1, 0, 2)).reshape(1, N_EXPERTS, bsz * cap_ctx)
            moe = _expert_ffn(idx_c, gate_c, hp, w1, w3, w2, layer, moe, bsz, n_ctx, n_lat)
    return _final(xs, moe, tabs[DEPTH - 1], final_g[None, :], n_lat)
```

```python
import functools

import jax
import jax.numpy as jnp
from jax import lax
from jax.experimental import pallas as pl
from jax.experimental.pallas import tpu as pltpu

F32 = jnp.float32
BF16 = jnp.bfloat16
HIGHEST = lax.Precision.HIGHEST

D_MODEL = 1024
DEPTH = 4
GRID_W = 64
NORM_EPS = 1e-6
ROPE_BASE = 10000.0
NEG_INF = -1e30
LOG2_E = 1.4426950408889634
PIN_SPLIT = 64.0

A_HEADS = 8
A_KV_HEADS = 2
A_HEAD_DIM = 64
WINDOW = 128
A_BLOCK = 128
B_WIDTH = 512
B_GROUPS = 4
B_CHUNK = 128
C_HEADS = 8
C_Q_LORA = 384
C_KV_LORA = 256
C_NOPE = 64
C_ROPE = 32
C_V = 64
D_WIDTH = 512
D_CONV = 3
N_EXPERTS = 16
EXPERT_FF = 1024
EC_FACTOR = 2

A_Q_W = A_HEADS * A_HEAD_DIM
A_KV_W = A_KV_HEADS * A_HEAD_DIM

LANES = 128
SUBLANES = 8
ROW_TILE = 256
SAMPLES_PER_STEP = 2
MOD_ROWS = 8
HP_CHUNKS = D_MODEL // (2 * LANES)
ACC_CHUNKS = D_MODEL // LANES
VMEM_LIMIT = 56 * 1024 * 1024

EVEN_COLS = 2 * A_KV_W + 2 * A_KV_W + A_Q_W + 2 * B_WIDTH
ODD_COLS = C_KV_LORA + LANES + C_Q_LORA + 3 * D_WIDTH
HEAD_PAD = LANES


def _cparams(sem):
    return pltpu.CompilerParams(dimension_semantics=sem, vmem_limit_bytes=VMEM_LIMIT)


def _nt_dot(a, b, precision=None):
    return lax.dot_general(a, b, (((1,), (1,)), ((), ())), precision=precision,
                           preferred_element_type=F32)


def _tn_dot(a, b, precision=None):
    return lax.dot_general(a, b, (((0,), (0,)), ((), ())), precision=precision,
                           preferred_element_type=F32)


def _mod_kernel(cc_ref, w_ref, b_ref, o_ref):
    cc = cc_ref[...]
    s = cc * jax.nn.sigmoid(cc)
    o_ref[0] = jnp.dot(s, w_ref[0], precision=HIGHEST, preferred_element_type=F32) + b_ref[0]


def _modulation(cc, mod_w, mod_b):
    rows = cc.shape[0]
    ncol = mod_w.shape[2] // D_MODEL
    return pl.pallas_call(
        _mod_kernel,
        grid=(DEPTH, ncol),
        in_specs=[
            pl.BlockSpec((rows, D_MODEL), lambda l, j: (0, 0)),
            pl.BlockSpec((1, D_MODEL, D_MODEL), lambda l, j: (l, 0, j)),
            pl.BlockSpec((1, 1, D_MODEL), lambda l, j: (l, 0, j)),
        ],
        out_specs=pl.BlockSpec((1, rows, D_MODEL), lambda l, j: (l, 0, j)),
        out_shape=jax.ShapeDtypeStruct((DEPTH, rows, mod_w.shape[2]), F32),
        compiler_params=_cparams(("arbitrary", "arbitrary")),
        name="adaln_modulation",
    )(cc, mod_w, mod_b.reshape(DEPTH, 1, -1))


def _residual_from_moe(x, moe_ref, s, gate_row, tm):
    cols = []
    for j in range(ACC_CHUNKS):
        chunk = moe_ref[s, pl.ds(j, tm, stride=ACC_CHUNKS), :]
        sl = slice(j * LANES, (j + 1) * LANES)
        cols.append(x[:, sl] + gate_row[:, sl] * chunk)
    return jnp.concatenate(cols, axis=1)


def _rms_mod(x, g, shift, scale):
    gain = g * (1.0 + scale)
    return x * lax.rsqrt(jnp.mean(x * x, axis=-1, keepdims=True) + NORM_EPS) * gain + shift


def _rms(x, g):
    return x * lax.rsqrt(jnp.mean(x * x, axis=-1, keepdims=True) + NORM_EPS) * g


def _rope_pairs(a, cos, sin, half, first):
    rot = jnp.where(first, pltpu.roll(a, LANES - half, 1), pltpu.roll(a, half, 1))
    return a * cos + rot * sin


def _in_even_kernel(has_moe, tm, *refs):
    if has_moe:
        (x_ref, moe_ref, modp_ref, mod_ref, g_ref, w_ref, cos_ref, sin_ref, lng_ref,
         xo_ref, q_ref, kd_ref, vd_ref, u_ref, vn_ref) = refs
    else:
        (x_ref, mod_ref, g_ref, w_ref, cos_ref, sin_ref, lng_ref,
         q_ref, kd_ref, vd_ref, u_ref, vn_ref) = refs
    cos = cos_ref[...]
    sin = sin_ref[...]
    lane = lax.broadcasted_iota(jnp.int32, (tm, LANES), 1)
    first = (lane % A_HEAD_DIM) < (A_HEAD_DIM // 2)
    half = A_HEAD_DIM // 2
    qoff = 4 * LANES
    uoff = qoff + A_Q_W
    scale = A_HEAD_DIM ** -0.5 * LOG2_E
    for s in range(x_ref.shape[0]):
        x = x_ref[s]
        if has_moe:
            x = _residual_from_moe(x, moe_ref, s, modp_ref[s, 0, 5:6, :], tm)
            xo_ref[s] = x
        h = _rms_mod(x, g_ref[...], mod_ref[s, 0, 0:1, :], mod_ref[s, 0, 1:2, :])
        z = jnp.dot(h.astype(BF16), w_ref[...], preferred_element_type=F32)
        for j in range(2):
            sl = slice(j * LANES, (j + 1) * LANES)
            kd_ref[s, :, sl] = _rope_pairs(z[:, sl], cos, sin, half, first).astype(BF16)
        vd_ref[s] = z[:, 2 * LANES:4 * LANES].astype(BF16)
        for j in range(A_Q_W // LANES):
            sl = slice(qoff + j * LANES, qoff + (j + 1) * LANES)
            q_ref[s, :, j * LANES:(j + 1) * LANES] = (
                _rope_pairs(z[:, sl], cos, sin, half, first) * scale).astype(BF16)
        u_ref[s] = jax.nn.gelu(z[:, uoff:uoff + B_WIDTH])
        gv = jax.nn.gelu(z[:, uoff + B_WIDTH:uoff + 2 * B_WIDTH])
        mu = jnp.mean(gv, axis=-1, keepdims=True)
        var = jnp.mean(jnp.square(gv - mu), axis=-1, keepdims=True)
        vn_ref[s] = ((gv - mu) * lax.rsqrt(var + NORM_EPS) * lng_ref[...]).astype(BF16)


def _samples_per_step(bsz, want):
    return want if bsz % want == 0 else SAMPLES_PER_STEP


def _tile_is_ctx(n_lat_tiles):
    return lambda i: jnp.where(i >= n_lat_tiles, 1, 0)


def _in_even(x, moe, modp, mod, g1, w_in, cos, sin, lng, n_lat):
    bsz, t, _ = x.shape
    tm = ROW_TILE
    nt = t // tm
    is_ctx = _tile_is_ctx(n_lat // tm)
    has_moe = moe is not None
    spb = _samples_per_step(bsz, 2 * SAMPLES_PER_STEP)
    row = lambda w: pl.BlockSpec((spb, tm, w), lambda b, i: (b, i, 0))
    mod_spec = pl.BlockSpec((spb, 1, MOD_ROWS, D_MODEL), lambda b, i: (b, is_ctx(i), 0, 0))
    full2 = lambda a: pl.BlockSpec(a.shape, lambda b, i: (0, 0))
    in_specs = [row(D_MODEL)]
    args = [x]
    if has_moe:
        in_specs += [pl.BlockSpec((spb, tm * ACC_CHUNKS, LANES), lambda b, i: (b, i, 0)), mod_spec]
        args += [moe, modp]
    in_specs += [mod_spec, full2(g1), full2(w_in),
                 pl.BlockSpec((tm, LANES), lambda b, i: (i, 0)),
                 pl.BlockSpec((tm, LANES), lambda b, i: (i, 0)), full2(lng)]
    args += [mod, g1, w_in, cos, sin, lng]
    out_specs = []
    out_shape = []
    if has_moe:
        out_specs.append(row(D_MODEL))
        out_shape.append(jax.ShapeDtypeStruct((bsz, t, D_MODEL), F32))
    out_specs += [row(A_Q_W), row(2 * LANES), row(2 * LANES), row(B_WIDTH), row(B_WIDTH)]
    out_shape += [jax.ShapeDtypeStruct((bsz, t, A_Q_W), BF16),
                  jax.ShapeDtypeStruct((bsz, t, 2 * LANES), BF16),
                  jax.ShapeDtypeStruct((bsz, t, 2 * LANES), BF16),
                  jax.ShapeDtypeStruct((bsz, t, B_WIDTH), F32),
                  jax.ShapeDtypeStruct((bsz, t, B_WIDTH), BF16)]
    outs = pl.pallas_call(
        functools.partial(_in_even_kernel, has_moe, tm),
        grid=(bsz // spb, nt), in_specs=in_specs, out_specs=out_specs, out_shape=out_shape,
        compiler_params=_cparams(("parallel", "arbitrary")),
        name="in_proj_even",
    )(*args)
    if not has_moe:
        outs = [x] + list(outs)
    return outs


def _win_attn_kernel(sink_ref, bias_ref, q_ref, kp_ref, kc_ref, kn_ref, kx_ref,
                     vp_ref, vc_ref, vn_ref, vx_ref, o_ref):
    blk = A_BLOCK
    g = A_HEADS // A_KV_HEADS
    nwin = 3 * blk
    nkeys = nwin + kx_ref.shape[1]
    bias = jnp.concatenate([bias_ref[0]] * g, axis=0)
    ones_col = jnp.where(lax.broadcasted_iota(jnp.int32, (nkeys, LANES), 1) == 0, 1.0, 0.0).astype(BF16)
    lane = lax.broadcasted_iota(jnp.int32, (blk, LANES), 1)
    low = lane < A_HEAD_DIM
    zero = jnp.zeros((), BF16)
    for smp, h in [(a, b) for a in range(q_ref.shape[0]) for b in range(A_KV_HEADS)]:
        hs = slice(h * LANES, (h + 1) * LANES)
        qs = []
        for cidx in range(g // 2):
            qc = q_ref[smp, :, (h * (g // 2) + cidx) * LANES:(h * (g // 2) + cidx + 1) * LANES]
            qs.append(jnp.where(low, qc, zero))
            qs.append(jnp.where(low, zero, qc))
        qh = jnp.concatenate(qs, axis=0)
        kcat = jnp.concatenate([kp_ref[smp, :, hs], kc_ref[smp, :, hs], kn_ref[smp, :, hs],
                                kx_ref[smp, :, hs]], axis=0)
        vcat = jnp.concatenate([vp_ref[smp, :, hs], vc_ref[smp, :, hs], vn_ref[smp, :, hs],
                                vx_ref[smp, :, hs]], axis=0)
        s = _nt_dot(qh, kcat)
        s = jnp.concatenate([s[:, :nwin] + bias, s[:, nwin:]], axis=1)
        sink = jnp.concatenate(
            [jnp.full((blk, 1), sink_ref[h * g + j] * LOG2_E, F32) for j in range(g)], axis=0)
        m = jnp.maximum(jnp.max(s, axis=-1, keepdims=True), sink)
        p = jnp.exp2(s - m)
        pv = jnp.dot(p.astype(BF16), jnp.concatenate([vcat, ones_col], axis=1),
                     preferred_element_type=F32)
        den = pv[:, LANES:LANES + 1] + jnp.exp2(sink - m)
        o = pv[:, :LANES] * (1.0 / den)
        for cidx in range(g // 2):
            o_even = o[(2 * cidx) * blk:(2 * cidx + 1) * blk]
            o_odd = o[(2 * cidx + 1) * blk:(2 * cidx + 2) * blk]
            col = (h * (g // 2) + cidx) * LANES
            o_ref[smp, :, col:col + LANES] = jnp.where(low, o_even, o_odd).astype(BF16)


def _win_attn(sink, q, kd, vd, n_lat, n_ctx):
    bsz, t, _ = q.shape
    blk = A_BLOCK
    nb = t // blk
    nlb = n_lat // blk
    kvw = 2 * LANES
    spb = _samples_per_step(bsz, 2 * SAMPLES_PER_STEP)
    prev = pl.BlockSpec((spb, blk, kvw), lambda b, i: (b, jnp.maximum(i - 1, 0), 0))
    cur = pl.BlockSpec((spb, blk, kvw), lambda b, i: (b, i, 0))
    nxt = pl.BlockSpec((spb, blk, kvw), lambda b, i: (b, jnp.minimum(i + 1, nb - 1), 0))
    ctx = pl.BlockSpec((spb, n_ctx, kvw), lambda b, i: (b, t // n_ctx - 1, 0))
    r = jnp.arange(blk)[:, None]
    c = jnp.arange(blk)[None, :]
    yes = jnp.ones((blk, blk), jnp.bool_)
    no = jnp.zeros((blk, blk), jnp.bool_)
    kinds = [(c >= r, yes, c <= r), (no, yes, c <= r), (c >= r, yes, no), (no, no, no)]
    bias = jnp.stack([jnp.where(jnp.concatenate([p, m, n], axis=1), 0.0, NEG_INF)
                      for p, m, n in kinds]).astype(F32)
    kind = lambda i: jnp.where(i >= nlb, 3, jnp.where(i == 0, 1, jnp.where(i == nlb - 1, 2, 0)))
    return pl.pallas_call(
        _win_attn_kernel,
        grid=(bsz // spb, nb),
        in_specs=[pl.BlockSpec(memory_space=pltpu.SMEM),
                  pl.BlockSpec((1, blk, 3 * blk), lambda b, i: (kind(i), 0, 0)),
                  pl.BlockSpec((spb, blk, A_Q_W), lambda b, i: (b, i, 0)),
                  prev, cur, nxt, ctx, prev, cur, nxt, ctx],
        out_specs=pl.BlockSpec((spb, blk, A_Q_W), lambda b, i: (b, i, 0)),
        out_shape=jax.ShapeDtypeStruct((bsz, t, A_Q_W), BF16),
        compiler_params=_cparams(("parallel", "arbitrary")),
        name="window_attention",
    )(sink, bias, q, kd, kd, kd, kd, vd, vd, vd, vd)


def _out_epilogue(tm, s, y, x_ref, mod_ref, g2_ref, rw_ref, xo_ref, hp_ref, aff_ref):
    x = x_ref[s] + mod_ref[s, 0, 2:3, :] * y
    xo_ref[s] = x
    h2 = _rms_mod(x, g2_ref[...], mod_ref[s, 0, 3:4, :], mod_ref[s, 0, 4:5, :])
    h_hi = h2.astype(BF16)
    h_hi32 = h_hi.astype(F32)
    bits = lax.bitcast_convert_type(h_hi32, jnp.uint32)
    half = D_MODEL // 2
    packed = (bits[:, :half] >> 16) | (bits[:, half:] & jnp.uint32(0xFFFF0000))
    for j in range(HP_CHUNKS):
        hp_ref[s, pl.ds(j, tm, stride=HP_CHUNKS), :] = packed[:, j * LANES:(j + 1) * LANES]
    h_lo = (h2 - h_hi32).astype(BF16)
    logits = (_nt_dot(rw_ref[0], h_hi) + _nt_dot(rw_ref[0], h_lo)) + _nt_dot(rw_ref[1], h_hi)
    mx = jnp.max(logits, axis=0, keepdims=True)
    ex = jnp.exp(logits - mx)
    aff_ref[s] = ex / jnp.sum(ex, axis=0, keepdims=True)


def _out_even_kernel(tm, o_ref, u_ref, vn_ref, ws_ref, bs_ref, wo_ref, x_ref, mod_ref, g2_ref,
                     rw_ref, xo_ref, hp_ref, aff_ref):
    dot = functools.partial(jnp.dot, preferred_element_type=F32)
    gw = B_WIDTH // B_GROUPS
    spb = x_ref.shape[0]
    rows = []
    for s in range(spb):
        for cidx in range(tm // B_CHUNK):
            rs = slice(cidx * B_CHUNK, (cidx + 1) * B_CHUNK)
            cols = []
            for g in range(B_GROUPS):
                cs = slice(g * gw, (g + 1) * gw)
                mixed = dot(ws_ref[g], vn_ref[s, rs, cs]) + bs_ref[:, cs]
                cols.append(u_ref[s, rs, cs] * mixed)
            rows.append(jnp.concatenate(cols, axis=1))
    sg = jnp.concatenate(rows, axis=0).astype(BF16)
    o_all = jnp.concatenate([o_ref[s] for s in range(spb)], axis=0)
    y = dot(o_all, wo_ref[:A_Q_W, :]) + dot(sg, wo_ref[A_Q_W:, :])
    for s in range(spb):
        _out_epilogue(tm, s, y[s * tm:(s + 1) * tm], x_ref, mod_ref, g2_ref, rw_ref, xo_ref, hp_ref,
                      aff_ref)


def _out_specs_common(bsz, t, tm, spb):
    row = lambda w: pl.BlockSpec((spb, tm, w), lambda b, i: (b, i, 0))
    out_specs = [row(D_MODEL),
                 pl.BlockSpec((spb, tm * HP_CHUNKS, LANES), lambda b, i: (b, i, 0)),
                 pl.BlockSpec((spb, N_EXPERTS, tm), lambda b, i: (b, 0, i))]
    out_shape = [jax.ShapeDtypeStruct((bsz, t, D_MODEL), F32),
                 jax.ShapeDtypeStruct((bsz, t * HP_CHUNKS, LANES), jnp.uint32),
                 jax.ShapeDtypeStruct((bsz, N_EXPERTS, t), F32)]
    return out_specs, out_shape


def _out_even(o, u, vn, ws, bs, wo, x, mod, g2, rw_t, n_lat):
    bsz, t, _ = x.shape
    tm = ROW_TILE
    is_ctx = _tile_is_ctx(n_lat // tm)
    spb = _samples_per_step(bsz, 2 * SAMPLES_PER_STEP)
    row = lambda w: pl.BlockSpec((spb, tm, w), lambda b, i: (b, i, 0))
    full = lambda a: pl.BlockSpec(a.shape, lambda b, i: (0,) * a.ndim)
    mod_spec = pl.BlockSpec((spb, 1, MOD_ROWS, D_MODEL), lambda b, i: (b, is_ctx(i), 0, 0))
    out_specs, out_shape = _out_specs_common(bsz, t, tm, spb)
    return pl.pallas_call(
        functools.partial(_out_even_kernel, tm),
        grid=(bsz // spb, t // tm),
        in_specs=[row(A_Q_W), row(B_WIDTH), row(B_WIDTH), full(ws), full(bs), full(wo),
                  row(D_MODEL), mod_spec, full(g2), full(rw_t)],
        out_specs=out_specs, out_shape=out_shape,
        compiler_params=_cparams(("parallel", "arbitrary")),
        name="out_proj_even",
    )(o, u, vn, ws, bs, wo, x, mod, g2, rw_t)


def _in_odd_kernel(has_moe, tm, *refs):
    (x_ref, moe_ref, modp_ref, mod_ref, g_ref, w_ref, cos_ref, sin_ref, qg_ref, kvg_ref,
     wuq_ref, wukv_ref, xo_ref, q_ref, k_ref, v_ref, gb_ref, u_ref) = refs
    cos = cos_ref[...]
    sin = sin_ref[...]
    lane = lax.broadcasted_iota(jnp.int32, (tm, LANES), 1)
    half = C_ROPE // 2
    first = lane < C_NOPE + half
    cqo = C_KV_LORA + LANES
    co = cqo + C_Q_LORA
    scale = (C_NOPE + C_ROPE) ** -0.5 * LOG2_E
    ones_col = jnp.where(lane == C_V, 1.0, 0.0)
    voff = C_HEADS * HEAD_PAD
    for s in range(x_ref.shape[0]):
        x = _residual_from_moe(x_ref[s], moe_ref, s, modp_ref[s, 0, 5:6, :], tm)
        xo_ref[s] = x
        h = _rms_mod(x, g_ref[...], mod_ref[s, 0, 0:1, :], mod_ref[s, 0, 1:2, :])
        z = jnp.dot(h.astype(BF16), w_ref[...], preferred_element_type=F32)
        ckv = _rms(z[:, :C_KV_LORA], kvg_ref[...])
        kr = _rope_pairs(z[:, C_KV_LORA:C_KV_LORA + LANES], cos, sin, half, first)
        cq = _rms(z[:, cqo:cqo + C_Q_LORA], qg_ref[...])
        kv = jnp.dot(ckv.astype(BF16), wukv_ref[...], preferred_element_type=F32)
        qq = jnp.dot(cq.astype(BF16), wuq_ref[...], preferred_element_type=F32)
        for hd in range(C_HEADS):
            sl = slice(hd * HEAD_PAD, (hd + 1) * HEAD_PAD)
            q_ref[s, :, sl] = (_rope_pairs(qq[:, sl], cos, sin, half, first) * scale).astype(BF16)
            k_ref[s, :, sl] = (kv[:, sl] + kr).astype(BF16)
            v_ref[s, :, sl] = (kv[:, voff + hd * HEAD_PAD:voff + (hd + 1) * HEAD_PAD]
                               + ones_col).astype(BF16)
        gb_ref[s] = z[:, co:co + D_WIDTH]
        u_ref[s] = z[:, co + D_WIDTH:co + 2 * D_WIDTH] * z[:, co + 2 * D_WIDTH:co + 3 * D_WIDTH]


def _in_odd(x, moe, modp, mod, g1, w_in, cos, sin, qg, kvg, wuq, wukv, n_lat):
    bsz, t, _ = x.shape
    tm = ROW_TILE
    is_ctx = _tile_is_ctx(n_lat // tm)
    spb = SAMPLES_PER_STEP
    row = lambda w: pl.BlockSpec((spb, tm, w), lambda b, i: (b, i, 0))
    mod_spec = pl.BlockSpec((spb, 1, MOD_ROWS, D_MODEL), lambda b, i: (b, is_ctx(i), 0, 0))
    full2 = lambda a: pl.BlockSpec(a.shape, lambda b, i: (0, 0))
    hw = C_HEADS * HEAD_PAD
    return pl.pallas_call(
        functools.partial(_in_odd_kernel, True, tm),
        grid=(bsz // spb, t // tm),
        in_specs=[row(D_MODEL),
                  pl.BlockSpec((spb, tm * ACC_CHUNKS, LANES), lambda b, i: (b, i, 0)),
                  mod_spec, mod_spec, full2(g1), full2(w_in),
                  pl.BlockSpec((tm, LANES), lambda b, i: (i, 0)),
                  pl.BlockSpec((tm, LANES), lambda b, i: (i, 0)),
                  full2(qg), full2(kvg), full2(wuq), full2(wukv)],
        out_specs=[row(D_MODEL), row(hw), row(hw), row(hw), row(D_WIDTH), row(D_WIDTH)],
        out_shape=[jax.ShapeDtypeStruct((bsz, t, D_MODEL), F32),
                   jax.ShapeDtypeStruct((bsz, t, hw), BF16),
                   jax.ShapeDtypeStruct((bsz, t, hw), BF16),
                   jax.ShapeDtypeStruct((bsz, t, hw), BF16),
                   jax.ShapeDtypeStruct((bsz, t, D_WIDTH), F32),
                   jax.ShapeDtypeStruct((bsz, t, D_WIDTH), F32)],
        compiler_params=_cparams(("parallel", "arbitrary")),
        name="in_proj_odd",
    )(x, moe, modp, mod, g1, w_in, cos, sin, qg, kvg, wuq, wukv)


def _dense_attn_kernel(tk, q_ref, k_ref, v_ref, o_ref):
    tq = q_ref.shape[1]
    nk = k_ref.shape[1] // tk
    for hd in range(q_ref.shape[2] // HEAD_PAD):
        hs = slice(hd * HEAD_PAD, (hd + 1) * HEAD_PAD)
        q = q_ref[0, :, hs]
        m = jnp.full((tq, 1), NEG_INF, F32)
        acc = jnp.zeros((tq, HEAD_PAD), F32)
        for j in range(nk):
            s = _nt_dot(q, k_ref[0, j * tk:(j + 1) * tk, hs])
            m_new = jnp.maximum(m, jnp.max(s, axis=-1, keepdims=True))
            alpha = jnp.exp2(m - m_new)
            p = jnp.exp2(s - m_new)
            acc = alpha * acc + jnp.dot(p.astype(BF16), v_ref[0, j * tk:(j + 1) * tk, hs],
                                        preferred_element_type=F32)
            m = m_new
        o_ref[0, :, hs] = (acc / acc[:, C_V:C_V + 1]).astype(BF16)


def _dense_attn(q, k, v, q_start, q_len, k_start, k_len, tq, heads_per_step):
    bsz, _, hw = q.shape
    tk = ROW_TILE
    q0 = q_start // tq
    kb = k_start // k_len
    hblk = heads_per_step * HEAD_PAD
    return pl.pallas_call(
        functools.partial(_dense_attn_kernel, tk),
        grid=(bsz, C_HEADS // heads_per_step, q_len // tq),
        in_specs=[pl.BlockSpec((1, tq, hblk), lambda b, h, i: (b, q0 + i, h)),
                  pl.BlockSpec((1, k_len, hblk), lambda b, h, i: (b, kb, h)),
                  pl.BlockSpec((1, k_len, hblk), lambda b, h, i: (b, kb, h))],
        out_specs=pl.BlockSpec((1, tq, hblk), lambda b, h, i: (b, i, h)),
        out_shape=jax.ShapeDtypeStruct((bsz, q_len, hw), BF16),
        compiler_params=_cparams(("parallel", "parallel", "arbitrary")),
        name="dense_attention",
    )(q, k, v)


def _out_odd_kernel(tm, n_lat_tiles, n_tiles, ol_ref, oc_ref, gb_ref, u_ref, up_ref, un_ref, cw_ref,
                    wo_ref, x_ref, mod_ref, g2_ref, rw_ref, xo_ref, hp_ref, aff_ref):
    i = pl.program_id(1)
    dot = functools.partial(jnp.dot, preferred_element_type=F32)
    has_prev = jnp.logical_and(i != 0, i != n_lat_tiles)
    has_next = jnp.logical_and(i != n_lat_tiles - 1, i != n_tiles - 1)
    hw = C_HEADS * HEAD_PAD
    spb = x_ref.shape[0]
    o_att = []
    cs = []
    for s in range(spb):
        o_att.append(jnp.where(i < n_lat_tiles, ol_ref[s], oc_ref[s]))
        u = u_ref[s]
        prev_row = jnp.where(has_prev, up_ref[s, SUBLANES - 1:SUBLANES, :], 0.0)
        next_row = jnp.where(has_next, un_ref[s, 0:1, :], 0.0)
        ridx = lax.broadcasted_iota(jnp.int32, u.shape, 0)
        u_m1 = jnp.where(ridx == 0, prev_row, pltpu.roll(u, 1, 0))
        u_p1 = jnp.where(ridx == tm - 1, next_row, pltpu.roll(u, tm - 1, 0))
        conv = u_m1 * cw_ref[0:1, :] + u * cw_ref[1:2, :] + u_p1 * cw_ref[2:3, :]
        cs.append((gb_ref[s] * conv).astype(BF16))
    y = (dot(jnp.concatenate(o_att, axis=0), wo_ref[:hw, :])
         + dot(jnp.concatenate(cs, axis=0), wo_ref[hw:, :]))
    for s in range(spb):
        _out_epilogue(tm, s, y[s * tm:(s + 1) * tm], x_ref, mod_ref, g2_ref, rw_ref, xo_ref, hp_ref,
                      aff_ref)


def _out_odd(o_lat, o_ctx, gb, u, cw, wo, x, mod, g2, rw_t, n_lat):
    bsz, t, _ = x.shape
    tm = ROW_TILE
    nt = t // tm
    nlt = n_lat // tm
    is_ctx = _tile_is_ctx(nlt)
    spb = _samples_per_step(bsz, 2 * SAMPLES_PER_STEP)
    row = lambda w: pl.BlockSpec((spb, tm, w), lambda b, i: (b, i, 0))
    full = lambda a: pl.BlockSpec(a.shape, lambda b, i: (0,) * a.ndim)
    mod_spec = pl.BlockSpec((spb, 1, MOD_ROWS, D_MODEL), lambda b, i: (b, is_ctx(i), 0, 0))
    hw = C_HEADS * HEAD_PAD
    per = tm // SUBLANES
    last = t // SUBLANES - 1
    halo_prev = pl.BlockSpec((spb, SUBLANES, D_WIDTH), lambda b, i: (b, jnp.maximum(i * per - 1, 0), 0))
    halo_next = pl.BlockSpec((spb, SUBLANES, D_WIDTH), lambda b, i: (b, jnp.minimum((i + 1) * per, last), 0))
    out_specs, out_shape = _out_specs_common(bsz, t, tm, spb)
    return pl.pallas_call(
        functools.partial(_out_odd_kernel, tm, nlt, nt),
        grid=(bsz // spb, nt),
        in_specs=[pl.BlockSpec((spb, tm, hw), lambda b, i: (b, jnp.minimum(i, nlt - 1), 0)),
                  pl.BlockSpec((spb, tm, hw), lambda b, i: (b, jnp.maximum(i - nlt, 0), 0)),
                  row(D_WIDTH), row(D_WIDTH), halo_prev, halo_next,
                  full(cw), full(wo), row(D_MODEL), mod_spec, full(g2), full(rw_t)],
        out_specs=out_specs, out_shape=out_shape,
        compiler_params=_cparams(("parallel", "arbitrary")),
        name="out_proj_odd",
    )(o_lat, o_ctx, gb, u, u, u, cw, wo, x, mod, g2, rw_t)


def _route_kernel(cap, nblk, width, a_ref, idx_ref, gate_ref):
    n_e = N_EXPERTS
    a = a_ref[0]
    bits = lax.bitcast_convert_type(a, jnp.int32)

    def count(mask):
        c = jnp.sum(jnp.where(mask, 1.0, 0.0), axis=2, keepdims=True)
        return jnp.sum(c, axis=1, keepdims=True)

    def search(it, thr):
        cand = thr | jnp.left_shift(jnp.int32(1), 30 - it)
        return jnp.where(count(bits >= cand) >= cap, cand, thr)

    thr = lax.fori_loop(0, 31, search, jnp.zeros((n_e, 1, 1), jnp.int32))
    gt = bits > thr
    eq = bits == thr
    need = cap - count(gt)

    rows = n_e * nblk
    li = lax.broadcasted_iota(jnp.int32, (width, width), 0)
    lj = lax.broadcasted_iota(jnp.int32, (width, width), 1)
    upper = jnp.where(li <= lj, 1.0, 0.0).astype(BF16)
    ones_sq = jnp.ones((width, width), BF16)
    if nblk > 1:
        ri = lax.broadcasted_iota(jnp.int32, (rows, rows), 0)
        rj = lax.broadcasted_iota(jnp.int32, (rows, rows), 1)
        lower = jnp.where(jnp.logical_and(ri // nblk == rj // nblk, rj < ri), 1.0, 0.0).astype(BF16)

    def prefix(mask3):
        m2 = jnp.where(mask3, 1.0, 0.0).astype(BF16).reshape(rows, width)
        local = jnp.dot(m2, upper, preferred_element_type=F32)
        total = jnp.dot(m2, ones_sq, preferred_element_type=F32)
        if nblk > 1:
            excl = jnp.dot(lower, total.astype(BF16), preferred_element_type=F32)
        else:
            excl = jnp.zeros_like(total)
        return local, total, excl

    l_eq, _, x_eq = prefix(eq)
    tie_rank = (l_eq + x_eq).reshape(n_e, nblk, width)
    sel = jnp.logical_or(gt, jnp.logical_and(eq, tie_rank <= need))
    local, total, excl = prefix(sel)
    pin = local + excl

    slot_l = lax.broadcasted_iota(jnp.int32, (nblk, cap), 1).astype(F32)
    blk_s = lax.broadcasted_iota(jnp.int32, (nblk, cap), 0).astype(F32)
    slot_s = lax.broadcasted_iota(jnp.int32, (cap, width), 0).astype(F32)
    lane_w = lax.broadcasted_iota(jnp.int32, (cap, width), 1).astype(F32)
    ones_r = jnp.ones((SUBLANES, width), BF16)
    reps = cap // width if cap >= width else 1
    a2 = a.reshape(rows, width)
    if nblk > 1:
        pin_hi = jnp.floor(pin * (1.0 / PIN_SPLIT))
        a_1 = a2.astype(BF16)
        rem = a2 - a_1.astype(F32)
        a_2 = rem.astype(BF16)
        a_3 = (rem - a_2.astype(F32)).astype(BF16)
        table = jnp.concatenate([pin_hi.astype(BF16), (pin - PIN_SPLIT * pin_hi).astype(BF16),
                                 a_1, a_2, a_3], axis=1)
    for e in range(n_e):
        rs = slice(e * nblk, (e + 1) * nblk)
        if nblk > 1:
            lo = jnp.concatenate([excl[rs]] * reps, axis=1)[:, :cap]
            hi = lo + jnp.concatenate([total[rs]] * reps, axis=1)[:, :cap]
            oh_t = jnp.where(jnp.logical_and(lo <= slot_l, slot_l < hi), 1.0, 0.0)
            got = _tn_dot(oh_t.astype(BF16), table[rs])
            pin_g = PIN_SPLIT * got[:, :width] + got[:, width:2 * width]
            aff_g = (got[:, 2 * width:3 * width] + got[:, 3 * width:4 * width]) + got[:, 4 * width:]
            blk_row = jnp.sum(oh_t * blk_s, axis=0, keepdims=True)
        else:
            pin_g = jnp.broadcast_to(pin[rs], (cap, width))
            aff_g = jnp.broadcast_to(a2[rs], (cap, width))
            blk_row = jnp.zeros((1, cap), F32)
        ind = jnp.where(pin_g <= slot_s, 1.0, 0.0).astype(BF16)
        cnt_row = _nt_dot(ones_r, ind)[0:1]
        cnt_b = jnp.dot(ind, ones_sq, preferred_element_type=F32)
        picked = jnp.where(lane_w == cnt_b, aff_g, 0.0)
        idx_ref[0, e] = (blk_row * width + cnt_row).astype(jnp.int32)
        gate_ref[0, e] = jnp.sum(picked, axis=1, keepdims=True)


def _route(aff, cap):
    bsz, n_e, n = aff.shape
    width = LANES if n % (LANES * SUBLANES) == 0 else n
    nblk = n // width
    a4 = aff.reshape(bsz, n_e, nblk, width)
    idx, gate = pl.pallas_call(
        functools.partial(_route_kernel, cap, nblk, width),
        grid=(bsz,),
        in_specs=[pl.BlockSpec((1, n_e, nblk, width), lambda b: (b, 0, 0, 0))],
        out_specs=[pl.BlockSpec((1, n_e, 1, cap), lambda b: (b, 0, 0, 0)),
                   pl.BlockSpec((1, n_e, cap, 1), lambda b: (b, 0, 0, 0))],
        out_shape=[jax.ShapeDtypeStruct((bsz, n_e, 1, cap), jnp.int32),
                   jax.ShapeDtypeStruct((bsz, n_e, cap, 1), F32)],
        compiler_params=_cparams(("parallel",)),
        name="expert_choice_routing",
    )(a4)
    return idx.reshape(bsz, n_e, cap), gate.reshape(bsz, n_e, cap)


def _ffn_copies(hp_hbm, out_hbm, hsrc, acc, sem_in, sem_out, grp, nsub, n_per, row0):
    ins = []
    outs = []
    for sb in range(nsub):
        b = grp * nsub + sb
        ins.append(pltpu.make_async_copy(
            hp_hbm.at[b, pl.ds(row0 * HP_CHUNKS, n_per * HP_CHUNKS), :],
            hsrc.at[pl.ds(sb * n_per * HP_CHUNKS, n_per * HP_CHUNKS), :], sem_in.at[sb]))
        outs.append(pltpu.make_async_copy(
            acc.at[pl.ds(sb * n_per * ACC_CHUNKS, n_per * ACC_CHUNKS), :],
            out_hbm.at[b, pl.ds(row0 * ACC_CHUNKS, n_per * ACC_CHUNKS), :], sem_out.at[sb]))
    return ins, outs


def _ffn_gather(idx_ref, hsrc, xt, stride, mi):
    src = pl.multiple_of(idx_ref[0, 0, 0, mi] * HP_CHUNKS, HP_CHUNKS)
    xt[pl.ds(mi, HP_CHUNKS, stride=stride), :] = hsrc[pl.ds(src, HP_CHUNKS), :]


def _ffn_scatter(idx_ref, gate_ref, gate_on, acc, yt, stride, mis):
    dst = []
    val = []
    for mi in mis:
        d = pl.multiple_of(idx_ref[0, 0, 0, mi] * ACC_CHUNKS, ACC_CHUNKS)
        gate = gate_ref[0, 0, 0, mi]
        if gate_on is not None:
            gate = jnp.where(gate_on, gate, 0.0)
        slab = yt[pl.ds(mi, ACC_CHUNKS, stride=stride), :] * gate
        dst.append(d)
        val.append(acc[pl.ds(d, ACC_CHUNKS), :] + slab)
    for d, v in zip(dst, val):
        acc[pl.ds(d, ACC_CHUNKS), :] = v


def _ffn_kernel(m_slots, nsub, n_per, row0, unroll, idx_ref, idxn_ref, idxp_ref, gate_ref, gatep_ref,
                hp_hbm, w1a_ref, w1b_ref, w3a_ref, w3b_ref, w2a_ref, w2b_ref, _moe_in, out_hbm,
                hsrc, acc, xt, xs_buf, act_buf, yt, sem_in, sem_out):
    grp = pl.program_id(0)
    e = pl.program_id(1)
    stride = m_slots + SUBLANES
    ins, outs = _ffn_copies(hp_hbm, out_hbm, hsrc, acc, sem_in, sem_out, grp, nsub, n_per, row0)

    @pl.when(jnp.logical_and(grp == 0, e == 0))
    def _():
        yt[...] = jnp.zeros(yt.shape, F32)

    @pl.when(e == 0)
    def _():
        for cp in ins:
            cp.start()
        acc[...] = jnp.zeros(acc.shape, F32)
        for cp in ins:
            cp.wait()

        def first_gather(c, carry):
            for k in range(unroll):
                _ffn_gather(idx_ref, hsrc, xt, stride, c * unroll + k)
            return carry

        lax.fori_loop(0, m_slots // unroll, first_gather, 0)

    half = D_MODEL // 2
    for j in range(HP_CHUNKS):
        w = xt[pl.ds(j * stride, m_slots), :]
        xs_buf[:, j * LANES:(j + 1) * LANES] = lax.bitcast_convert_type(w << 16, F32).astype(BF16)
        xs_buf[:, half + j * LANES:half + (j + 1) * LANES] = lax.bitcast_convert_type(
            w & jnp.uint32(0xFFFF0000), F32).astype(BF16)

    for c in range(m_slots // unroll):
        _ffn_scatter(idxp_ref, gatep_ref, e > 0, acc, yt, stride, range(c * unroll, (c + 1) * unroll))
    for mi in range(m_slots):
        _ffn_gather(idxn_ref, hsrc, xt, stride, mi)

    dot = functools.partial(jnp.dot, preferred_element_type=F32)
    xs = xs_buf[...]
    fh = EXPERT_FF // 2
    for j, (wa_ref, wb_ref) in enumerate(((w1a_ref, w3a_ref), (w1b_ref, w3b_ref))):
        a = dot(xs, wa_ref[0, 0])
        b = dot(xs, wb_ref[0, 0])
        act_buf[:, j * fh:(j + 1) * fh] = (a * jax.nn.sigmoid(a) * b).astype(BF16)
    seen = xt[pl.ds(0, SUBLANES), :]
    for r in range(SUBLANES, m_slots, SUBLANES):
        seen = seen | xt[pl.ds(r, SUBLANES), :]
    seen = seen | lax.bitcast_convert_type(acc[pl.ds(0, SUBLANES), :], jnp.uint32)
    zero = lax.bitcast_convert_type(seen >> 32, F32)
    tile = (2 * SUBLANES, LANES)
    act_buf[:tile[0], :tile[1]] = (act_buf[:tile[0], :tile[1]].astype(F32)
                                   + jnp.concatenate([zero, zero], axis=0)).astype(BF16)
    y = dot(act_buf[:, :fh], w2a_ref[0, 0]) + dot(act_buf[:, fh:], w2b_ref[0, 0])
    for j in range(ACC_CHUNKS):
        yt[pl.ds(j * stride, m_slots), :] = y[:, j * LANES:(j + 1) * LANES]

    @pl.when(e == N_EXPERTS - 1)
    def _():
        def last_scatter(c, carry):
            _ffn_scatter(idx_ref, gate_ref, None, acc, yt, stride,
                         [c * unroll + k for k in range(unroll)])
            return carry

        lax.fori_loop(0, m_slots // unroll, last_scatter, 0)
        for cp in outs:
            cp.start()
        for cp in outs:
            cp.wait()


def _expert_ffn(idx, gate, hp, w1, w3, w2, layer, moe_buf, nsub, n_per, row0):
    groups, n_e, m_slots = idx.shape
    n_tok = nsub * n_per
    stride = m_slots + SUBLANES
    unroll = 8
    cur = lambda g, e: (g, e, 0, 0)
    nxt = lambda g, e: (g, jnp.minimum(e + 1, n_e - 1), 0, 0)
    prv = lambda g, e: (g, jnp.maximum(e - 1, 0), 0, 0)
    smem = lambda imap: pl.BlockSpec((1, 1, 1, m_slots), imap, memory_space=pltpu.SMEM)
    fh = EXPERT_FF // 2
    w_in = lambda j: pl.BlockSpec((1, 1, D_MODEL, fh), lambda g, e: (layer, e, 0, j))
    w_out = lambda j: pl.BlockSpec((1, 1, fh, D_MODEL), lambda g, e: (layer, e, j, 0))
    idx4 = idx.reshape(groups, n_e, 1, m_slots)
    gate4 = gate.reshape(groups, n_e, 1, m_slots)
    return pl.pallas_call(
        functools.partial(_ffn_kernel, m_slots, nsub, n_per, row0, unroll),
        grid=(groups, n_e),
        in_specs=[smem(cur), smem(nxt), smem(prv), smem(cur), smem(prv),
                  pl.BlockSpec(memory_space=pl.ANY), w_in(0), w_in(1), w_in(0), w_in(1),
                  w_out(0), w_out(1), pl.BlockSpec(memory_space=pl.ANY)],
        out_specs=pl.BlockSpec(memory_space=pl.ANY),
        out_shape=jax.ShapeDtypeStruct(moe_buf.shape, F32),
        scratch_shapes=[pltpu.VMEM((n_tok * HP_CHUNKS, LANES), jnp.uint32),
                        pltpu.VMEM((n_tok * ACC_CHUNKS, LANES), F32),
                        pltpu.VMEM((HP_CHUNKS * stride, LANES), jnp.uint32),
                        pltpu.VMEM((m_slots, D_MODEL), BF16),
                        pltpu.VMEM((m_slots, EXPERT_FF), BF16),
                        pltpu.VMEM((ACC_CHUNKS * stride, LANES), F32),
                        pltpu.SemaphoreType.DMA((nsub,)),
                        pltpu.SemaphoreType.DMA((nsub,))],
        input_output_aliases={12: 0},
        compiler_params=_cparams(("arbitrary", "arbitrary")),
        name="expert_ffn",
    )(idx4, idx4, idx4, gate4, gate4, hp, w1, w1, w3, w3, w2, w2, moe_buf)


def _final_kernel(tm, x_ref, moe_ref, modp_ref, g_ref, o_ref):
    x = _residual_from_moe(x_ref[0], moe_ref, 0, modp_ref[0, 0, 5:6, :], tm)
    o_ref[0] = _rms(x, g_ref[...])


def _final(x, moe, modp, g, n_lat):
    bsz = x.shape[0]
    tm = 4 * ROW_TILE
    return pl.pallas_call(
        functools.partial(_final_kernel, tm),
        grid=(bsz, n_lat // tm),
        in_specs=[pl.BlockSpec((1, tm, D_MODEL), lambda b, i: (b, i, 0)),
                  pl.BlockSpec((1, tm * ACC_CHUNKS, LANES), lambda b, i: (b, i, 0)),
                  pl.BlockSpec((1, 1, MOD_ROWS, D_MODEL), lambda b, i: (b, 0, 0, 0)),
                  pl.BlockSpec(g.shape, lambda b, i: (0, 0))],
        out_specs=pl.BlockSpec((1, tm, D_MODEL), lambda b, i: (b, i, 0)),
        out_shape=jax.ShapeDtypeStruct((bsz, n_lat, D_MODEL), F32),
        compiler_params=_cparams(("parallel", "arbitrary")),
        name="final_norm",
    )(x, moe, modp, g)


def _rope_angles(rows, rot_dim):
    row = jnp.repeat(jnp.arange(rows, dtype=F32), GRID_W)
    col = jnp.tile(jnp.arange(GRID_W, dtype=F32), rows)
    n_freq = rot_dim // 4
    inv_freq = ROPE_BASE ** (-jnp.arange(n_freq, dtype=F32) / n_freq)
    ang = jnp.concatenate([row[:, None] * inv_freq[None, :], col[:, None] * inv_freq[None, :]], axis=-1)
    return jnp.cos(ang), jnp.sin(ang)


def _rope_tables_even(n_lat, n_ctx):
    cos, sin = _rope_angles(n_lat // GRID_W, A_HEAD_DIM)
    cos_h = jnp.concatenate([cos, cos], axis=1)
    sin_h = jnp.concatenate([-sin, sin], axis=1)
    reps = LANES // A_HEAD_DIM
    cos_l = jnp.tile(cos_h, (1, reps))
    sin_l = jnp.tile(sin_h, (1, reps))
    return (jnp.concatenate([cos_l, jnp.ones((n_ctx, LANES), F32)], axis=0),
            jnp.concatenate([sin_l, jnp.zeros((n_ctx, LANES), F32)], axis=0))


def _rope_tables_odd(n_lat, n_ctx):
    cos, sin = _rope_angles(n_lat // GRID_W, C_ROPE)
    n = cos.shape[0]
    pad = LANES - C_NOPE - C_ROPE
    cos_l = jnp.concatenate([jnp.ones((n, C_NOPE), F32), cos, cos, jnp.ones((n, pad), F32)], axis=1)
    sin_l = jnp.concatenate([jnp.zeros((n, C_NOPE), F32), -sin, sin, jnp.zeros((n, pad), F32)], axis=1)
    return (jnp.concatenate([cos_l, jnp.ones((n_ctx, LANES), F32)], axis=0),
            jnp.concatenate([sin_l, jnp.zeros((n_ctx, LANES), F32)], axis=0))


def _even_w_in(w):
    k = w[:, :A_KV_W]
    v = w[:, A_KV_W:2 * A_KV_W]
    dup = lambda a: jnp.concatenate(
        [a[:, h * A_HEAD_DIM:(h + 1) * A_HEAD_DIM] for h in range(A_KV_HEADS) for _ in range(2)], axis=1)
    return jnp.concatenate([dup(k), dup(v), w[:, 2 * A_KV_W:]], axis=1).astype(BF16)


def _odd_w_in(w):
    d = w.shape[0]
    kr = w[:, C_KV_LORA:C_KV_LORA + C_ROPE]
    kr_group = jnp.concatenate(
        [jnp.zeros((d, C_NOPE), F32), kr, jnp.zeros((d, LANES - C_NOPE - C_ROPE), F32)], axis=1)
    return jnp.concatenate([w[:, :C_KV_LORA], kr_group, w[:, C_KV_LORA + C_ROPE:]], axis=1).astype(BF16)


def _odd_w_uq(w):
    r = w.shape[0]
    w3 = w.reshape(r, C_HEADS, C_NOPE + C_ROPE)
    pad = jnp.zeros((r, C_HEADS, HEAD_PAD - C_NOPE - C_ROPE), F32)
    return jnp.concatenate([w3, pad], axis=2).reshape(r, C_HEADS * HEAD_PAD).astype(BF16)


def _odd_w_ukv(w):
    r = w.shape[0]
    w3 = w.reshape(r, C_HEADS, C_NOPE + C_V)
    kpad = jnp.zeros((r, C_HEADS, HEAD_PAD - C_NOPE), F32)
    vpad = jnp.zeros((r, C_HEADS, HEAD_PAD - C_V), F32)
    kpart = jnp.concatenate([w3[:, :, :C_NOPE], kpad], axis=2).reshape(r, C_HEADS * HEAD_PAD)
    vpart = jnp.concatenate([w3[:, :, C_NOPE:], vpad], axis=2).reshape(r, C_HEADS * HEAD_PAD)
    return jnp.concatenate([kpart, vpart], axis=1).astype(BF16)


def _odd_w_out(w):
    d = w.shape[1]
    att = w[:C_HEADS * C_V].reshape(C_HEADS, C_V, d)
    att = jnp.concatenate([att, jnp.zeros((C_HEADS, HEAD_PAD - C_V, d), F32)], axis=1)
    return jnp.concatenate([att.reshape(C_HEADS * HEAD_PAD, d), w[C_HEADS * C_V:]], axis=0).astype(BF16)


def _mod_table(mods_l, bsz):
    lat = mods_l[:bsz].reshape(bsz, 1, 6, D_MODEL)
    ctx = jnp.broadcast_to(mods_l[bsz].reshape(1, 1, 6, D_MODEL), (bsz, 1, 6, D_MODEL))
    tab = jnp.concatenate([lat, ctx], axis=1)
    return jnp.pad(tab, ((0, 0), (0, 0), (0, MOD_ROWS - 6), (0, 0)))


def kernel(x, c, ctx, c_ctx, mod_w, mod_b, norm1_g, norm2_g, ev_w_in, ev_sink, ev_sgu_norm_g, ev_sgu_w, ev_sgu_b, ev_w_out, od_w_in, od_q_norm_g, od_w_uq, od_kv_norm_g, od_w_ukv, od_conv_w, od_w_out, router_w, exp_w1, exp_w3, exp_w2, final_g):
    bsz, n_lat, _ = x.shape
    n_ctx = ctx.shape[1]
    t = n_lat + n_ctx
    cap_lat = EC_FACTOR * n_lat // N_EXPERTS
    cap_ctx = EC_FACTOR * n_ctx // N_EXPERTS

    mod_rows = -(-(bsz + 1) // SUBLANES) * SUBLANES
    cc = jnp.concatenate([c, c_ctx[None, :], jnp.zeros((mod_rows - bsz - 1, D_MODEL), F32)], axis=0)
    mods = _modulation(cc, mod_w, mod_b)
    tabs = [_mod_table(mods[l], bsz) for l in range(DEPTH)]

    cos_e, sin_e = _rope_tables_even(n_lat, n_ctx)
    cos_o, sin_o = _rope_tables_odd(n_lat, n_ctx)

    xs = jnp.concatenate([x, ctx], axis=1)
    w1 = exp_w1.astype(BF16)
    w3 = exp_w3.astype(BF16)
    w2 = exp_w2.astype(BF16)
    moe = None
    for layer in range(DEPTH):
        i = layer // 2
        need_ctx = layer < DEPTH - 1
        g1 = norm1_g[layer][None, :]
        g2 = norm2_g[layer][None, :]
        rw_f = router_w[layer].T
        rw_hi = rw_f.astype(BF16)
        rw_t = jnp.stack([rw_hi, (rw_f - rw_hi.astype(F32)).astype(BF16)])
        modp = tabs[layer - 1] if layer > 0 else None
        if layer % 2 == 0:
            xs, q, kd, vd, u, vn = _in_even(xs, moe, modp, tabs[layer], g1, _even_w_in(ev_w_in[i]),
                                            cos_e, sin_e, ev_sgu_norm_g[i][None, :], n_lat)
            o = _win_attn(ev_sink[i], q, kd, vd, n_lat, n_ctx)
            bs = jnp.repeat(ev_sgu_b[i].T, B_WIDTH // B_GROUPS, axis=1)
            xs, hp, aff = _out_even(o, u, vn, ev_sgu_w[i].astype(BF16), bs, ev_w_out[i].astype(BF16),
                                    xs, tabs[layer], g2, rw_t, n_lat)
        else:
            xs, q, k, v, gb, u = _in_odd(xs, moe, modp, tabs[layer], g1, _odd_w_in(od_w_in[i]),
                                         cos_o, sin_o, od_q_norm_g[i][None, :], od_kv_norm_g[i][None, :],
                                         _odd_w_uq(od_w_uq[i]), _odd_w_ukv(od_w_ukv[i]), n_lat)
            o_lat = _dense_attn(q, k, v, 0, n_lat, 0, t, 4 * ROW_TILE, 1)
            o_ctx = _dense_attn(q, k, v, n_lat, n_ctx, n_lat, n_ctx, n_ctx, C_HEADS)
            xs, hp, aff = _out_odd(o_lat, o_ctx, gb, u, od_conv_w[i], _odd_w_out(od_w_out[i]), xs, tabs[layer],
                                   g2, rw_t, n_lat)
        if moe is None:
            moe = jnp.zeros((bsz, t * ACC_CHUNKS, LANES), F32)
        idx, gate = _route(aff[:, :, :n_lat], cap_lat)
        moe = _expert_ffn(idx, gate, hp, w1, w3, w2, layer, moe, 1, n_lat, 0)
        if need_ctx:
            idx_c, gate_c = _route(aff[:, :, n_lat:], cap_ctx)
            idx_c = idx_c + (jnp.arange(bsz, dtype=jnp.int32) * n_ctx)[:, None, None]
            idx_c = jnp.transpose(idx_c, (1, 0, 2)).reshape(1, N_EXPERTS, bsz * cap_ctx)
            gate_c = jnp.transpose(gate_c, (1, 0, 2)).reshape(1, N_EXPERTS, bsz * cap_ctx)
            moe = _expert_ffn(idx_c, gate_c, hp, w1, w3, w2, layer, moe, bsz, n_ctx, n_lat)
    return _final(xs, moe, tabs[DEPTH - 1], final_g[None, :], n_lat)
```

```python
import functools

import jax
import jax.numpy as jnp
from jax import lax
from jax.experimental import pallas as pl
from jax.experimental.pallas import tpu as pltpu

F32 = jnp.float32
BF16 = jnp.bfloat16
HIGHEST = lax.Precision.HIGHEST

D_MODEL = 1024
DEPTH = 4
GRID_W = 64
NORM_EPS = 1e-6
ROPE_BASE = 10000.0
NEG_INF = -1e30
LOG2_E = 1.4426950408889634
PIN_SPLIT = 64.0

A_HEADS = 8
A_KV_HEADS = 2
A_HEAD_DIM = 64
WINDOW = 128
A_BLOCK = 128
B_WIDTH = 512
B_GROUPS = 4
B_CHUNK = 128
C_HEADS = 8
C_Q_LORA = 384
C_KV_LORA = 256
C_NOPE = 64
C_ROPE = 32
C_V = 64
D_WIDTH = 512
D_CONV = 3
N_EXPERTS = 16
EXPERT_FF = 1024
EC_FACTOR = 2

A_Q_W = A_HEADS * A_HEAD_DIM
A_KV_W = A_KV_HEADS * A_HEAD_DIM

LANES = 128
SUBLANES = 8
ROW_TILE = 256
SAMPLES_PER_STEP = 2
MOD_ROWS = 8
HP_CHUNKS = D_MODEL // (2 * LANES)
ACC_CHUNKS = D_MODEL // LANES
VMEM_LIMIT = 56 * 1024 * 1024

EVEN_COLS = 2 * A_KV_W + 2 * A_KV_W + A_Q_W + 2 * B_WIDTH
ODD_COLS = C_KV_LORA + LANES + C_Q_LORA + 3 * D_WIDTH
HEAD_PAD = LANES


def _cparams(sem):
    return pltpu.CompilerParams(dimension_semantics=sem, vmem_limit_bytes=VMEM_LIMIT)


def _nt_dot(a, b, precision=None):
    return lax.dot_general(a, b, (((1,), (1,)), ((), ())), precision=precision,
                           preferred_element_type=F32)


def _tn_dot(a, b, precision=None):
    return lax.dot_general(a, b, (((0,), (0,)), ((), ())), precision=precision,
                           preferred_element_type=F32)


def _mod_kernel(cc_ref, w_ref, b_ref, o_ref):
    cc = cc_ref[...]
    s = cc * jax.nn.sigmoid(cc)
    o_ref[0] = jnp.dot(s, w_ref[0], precision=HIGHEST, preferred_element_type=F32) + b_ref[0]


def _modulation(cc, mod_w, mod_b):
    rows = cc.shape[0]
    ncol = mod_w.shape[2] // D_MODEL
    return pl.pallas_call(
        _mod_kernel,
        grid=(DEPTH, ncol),
        in_specs=[
            pl.BlockSpec((rows, D_MODEL), lambda l, j: (0, 0)),
            pl.BlockSpec((1, D_MODEL, D_MODEL), lambda l, j: (l, 0, j)),
            pl.BlockSpec((1, 1, D_MODEL), lambda l, j: (l, 0, j)),
        ],
        out_specs=pl.BlockSpec((1, rows, D_MODEL), lambda l, j: (l, 0, j)),
        out_shape=jax.ShapeDtypeStruct((DEPTH, rows, mod_w.shape[2]), F32),
        compiler_params=_cparams(("arbitrary", "arbitrary")),
        name="adaln_modulation",
    )(cc, mod_w, mod_b.reshape(DEPTH, 1, -1))


def _residual_from_moe(x, moe_ref, s, gate_row, tm):
    cols = []
    for j in range(ACC_CHUNKS):
        chunk = moe_ref[s, pl.ds(j, tm, stride=ACC_CHUNKS), :]
        sl = slice(j * LANES, (j + 1) * LANES)
        cols.append(x[:, sl] + gate_row[:, sl] * chunk)
    return jnp.concatenate(cols, axis=1)


def _rms_mod(x, g, shift, scale):
    gain = g * (1.0 + scale)
    return x * lax.rsqrt(jnp.mean(x * x, axis=-1, keepdims=True) + NORM_EPS) * gain + shift


def _rms(x, g):
    return x * lax.rsqrt(jnp.mean(x * x, axis=-1, keepdims=True) + NORM_EPS) * g


def _rope_pairs(a, cos, sin, half, first):
    rot = jnp.where(first, pltpu.roll(a, LANES - half, 1), pltpu.roll(a, half, 1))
    return a * cos + rot * sin


def _in_even_kernel(has_moe, tm, *refs):
    if has_moe:
        (x_ref, moe_ref, modp_ref, mod_ref, g_ref, w_ref, cos_ref, sin_ref, lng_ref,
         xo_ref, q_ref, kd_ref, vd_ref, u_ref, vn_ref) = refs
    else:
        (x_ref, mod_ref, g_ref, w_ref, cos_ref, sin_ref, lng_ref,
         q_ref, kd_ref, vd_ref, u_ref, vn_ref) = refs
    cos = cos_ref[...]
    sin = sin_ref[...]
    lane = lax.broadcasted_iota(jnp.int32, (tm, LANES), 1)
    first = (lane % A_HEAD_DIM) < (A_HEAD_DIM // 2)
    half = A_HEAD_DIM // 2
    qoff = 4 * LANES
    uoff = qoff + A_Q_W
    scale = A_HEAD_DIM ** -0.5 * LOG2_E
    for s in range(x_ref.shape[0]):
        x = x_ref[s]
        if has_moe:
            x = _residual_from_moe(x, moe_ref, s, modp_ref[s, 0, 5:6, :], tm)
            xo_ref[s] = x
        h = _rms_mod(x, g_ref[...], mod_ref[s, 0, 0:1, :], mod_ref[s, 0, 1:2, :])
        z = jnp.dot(h.astype(BF16), w_ref[...], preferred_element_type=F32)
        for j in range(2):
            sl = slice(j * LANES, (j + 1) * LANES)
            kd_ref[s, :, sl] = _rope_pairs(z[:, sl], cos, sin, half, first).astype(BF16)
        vd_ref[s] = z[:, 2 * LANES:4 * LANES].astype(BF16)
        for j in range(A_Q_W // LANES):
            sl = slice(qoff + j * LANES, qoff + (j + 1) * LANES)
            q_ref[s, :, j * LANES:(j + 1) * LANES] = (
                _rope_pairs(z[:, sl], cos, sin, half, first) * scale).astype(BF16)
        u_ref[s] = jax.nn.gelu(z[:, uoff:uoff + B_WIDTH])
        gv = jax.nn.gelu(z[:, uoff + B_WIDTH:uoff + 2 * B_WIDTH])
        mu = jnp.mean(gv, axis=-1, keepdims=True)
        var = jnp.mean(jnp.square(gv - mu), axis=-1, keepdims=True)
        vn_ref[s] = ((gv - mu) * lax.rsqrt(var + NORM_EPS) * lng_ref[...]).astype(BF16)


def _samples_per_step(bsz, want):
    return want if bsz % want == 0 else SAMPLES_PER_STEP


def _tile_is_ctx(n_lat_tiles):
    return lambda i: jnp.where(i >= n_lat_tiles, 1, 0)


def _in_even(x, moe, modp, mod, g1, w_in, cos, sin, lng, n_lat):
    bsz, t, _ = x.shape
    tm = ROW_TILE
    nt = t // tm
    is_ctx = _tile_is_ctx(n_lat // tm)
    has_moe = moe is not None
    spb = _samples_per_step(bsz, 2 * SAMPLES_PER_STEP)
    row = lambda w: pl.BlockSpec((spb, tm, w), lambda b, i: (b, i, 0))
    mod_spec = pl.BlockSpec((spb, 1, MOD_ROWS, D_MODEL), lambda b, i: (b, is_ctx(i), 0, 0))
    full2 = lambda a: pl.BlockSpec(a.shape, lambda b, i: (0, 0))
    in_specs = [row(D_MODEL)]
    args = [x]
    if has_moe:
        in_specs += [pl.BlockSpec((spb, tm * ACC_CHUNKS, LANES), lambda b, i: (b, i, 0)), mod_spec]
        args += [moe, modp]
    in_specs += [mod_spec, full2(g1), full2(w_in),
                 pl.BlockSpec((tm, LANES), lambda b, i: (i, 0)),
                 pl.BlockSpec((tm, LANES), lambda b, i: (i, 0)), full2(lng)]
    args += [mod, g1, w_in, cos, sin, lng]
    out_specs = []
    out_shape = []
    if has_moe:
        out_specs.append(row(D_MODEL))
        out_shape.append(jax.ShapeDtypeStruct((bsz, t, D_MODEL), F32))
    out_specs += [row(A_Q_W), row(2 * LANES), row(2 * LANES), row(B_WIDTH), row(B_WIDTH)]
    out_shape += [jax.ShapeDtypeStruct((bsz, t, A_Q_W), BF16),
                  jax.ShapeDtypeStruct((bsz, t, 2 * LANES), BF16),
                  jax.ShapeDtypeStruct((bsz, t, 2 * LANES), BF16),
                  jax.ShapeDtypeStruct((bsz, t, B_WIDTH), F32),
                  jax.ShapeDtypeStruct((bsz, t, B_WIDTH), BF16)]
    outs = pl.pallas_call(
        functools.partial(_in_even_kernel, has_moe, tm),
        grid=(bsz // spb, nt), in_specs=in_specs, out_specs=out_specs, out_shape=out_shape,
        compiler_params=_cparams(("parallel", "arbitrary")),
        name="in_proj_even",
    )(*args)
    if not has_moe:
        outs = [x] + list(outs)
    return outs


def _win_attn_kernel(sink_ref, bias_ref, q_ref, kp_ref, kc_ref, kn_ref, kx_ref,
                     vp_ref, vc_ref, vn_ref, vx_ref, o_ref):
    blk = A_BLOCK
    g = A_HEADS // A_KV_HEADS
    nwin = 3 * blk
    nkeys = nwin + kx_ref.shape[1]
    bias = jnp.concatenate([bias_ref[0]] * g, axis=0)
    ones_col = jnp.where(lax.broadcasted_iota(jnp.int32, (nkeys, LANES), 1) == 0, 1.0, 0.0).astype(BF16)
    lane = lax.broadcasted_iota(jnp.int32, (blk, LANES), 1)
    low = lane < A_HEAD_DIM
    zero = jnp.zeros((), BF16)
    for smp, h in [(a, b) for a in range(q_ref.shape[0]) for b in range(A_KV_HEADS)]:
        hs = slice(h * LANES, (h + 1) * LANES)
        qs = []
        for cidx in range(g // 2):
            qc = q_ref[smp, :, (h * (g // 2) + cidx) * LANES:(h * (g // 2) + cidx + 1) * LANES]
            qs.append(jnp.where(low, qc, zero))
            qs.append(jnp.where(low, zero, qc))
        qh = jnp.concatenate(qs, axis=0)
        kcat = jnp.concatenate([kp_ref[smp, :, hs], kc_ref[smp, :, hs], kn_ref[smp, :, hs],
                                kx_ref[smp, :, hs]], axis=0)
        vcat = jnp.concatenate([vp_ref[smp, :, hs], vc_ref[smp, :, hs], vn_ref[smp, :, hs],
                                vx_ref[smp, :, hs]], axis=0)
        s = _nt_dot(qh, kcat)
        s = jnp.concatenate([s[:, :nwin] + bias, s[:, nwin:]], axis=1)
        sink = jnp.concatenate(
            [jnp.full((blk, 1), sink_ref[h * g + j] * LOG2_E, F32) for j in range(g)], axis=0)
        m = jnp.maximum(jnp.max(s, axis=-1, keepdims=True), sink)
        p = jnp.exp2(s - m)
        pv = jnp.dot(p.astype(BF16), jnp.concatenate([vcat, ones_col], axis=1),
                     preferred_element_type=F32)
        den = pv[:, LANES:LANES + 1] + jnp.exp2(sink - m)
        o = pv[:, :LANES] * (1.0 / den)
        for cidx in range(g // 2):
            o_even = o[(2 * cidx) * blk:(2 * cidx + 1) * blk]
            o_odd = o[(2 * cidx + 1) * blk:(2 * cidx + 2) * blk]
            col = (h * (g // 2) + cidx) * LANES
            o_ref[smp, :, col:col + LANES] = jnp.where(low, o_even, o_odd).astype(BF16)


def _win_attn(sink, q, kd, vd, n_lat, n_ctx):
    bsz, t, _ = q.shape
    blk = A_BLOCK
    nb = t // blk
    nlb = n_lat // blk
    kvw = 2 * LANES
    spb = _samples_per_step(bsz, 2 * SAMPLES_PER_STEP)
    prev = pl.BlockSpec((spb, blk, kvw), lambda b, i: (b, jnp.maximum(i - 1, 0), 0))
    cur = pl.BlockSpec((spb, blk, kvw), lambda b, i: (b, i, 0))
    nxt = pl.BlockSpec((spb, blk, kvw), lambda b, i: (b, jnp.minimum(i + 1, nb - 1), 0))
    ctx = pl.BlockSpec((spb, n_ctx, kvw), lambda b, i: (b, t // n_ctx - 1, 0))
    r = jnp.arange(blk)[:, None]
    c = jnp.arange(blk)[None, :]
    yes = jnp.ones((blk, blk), jnp.bool_)
    no = jnp.zeros((blk, blk), jnp.bool_)
    kinds = [(c >= r, yes, c <= r), (no, yes, c <= r), (c >= r, yes, no), (no, no, no)]
    bias = jnp.stack([jnp.where(jnp.concatenate([p, m, n], axis=1), 0.0, NEG_INF)
                      for p, m, n in kinds]).astype(F32)
    kind = lambda i: jnp.where(i >= nlb, 3, jnp.where(i == 0, 1, jnp.where(i == nlb - 1, 2, 0)))
    return pl.pallas_call(
        _win_attn_kernel,
        grid=(bsz // spb, nb),
        in_specs=[pl.BlockSpec(memory_space=pltpu.SMEM),
                  pl.BlockSpec((1, blk, 3 * blk), lambda b, i: (kind(i), 0, 0)),
                  pl.BlockSpec((spb, blk, A_Q_W), lambda b, i: (b, i, 0)),
                  prev, cur, nxt, ctx, prev, cur, nxt, ctx],
        out_specs=pl.BlockSpec((spb, blk, A_Q_W), lambda b, i: (b, i, 0)),
        out_shape=jax.ShapeDtypeStruct((bsz, t, A_Q_W), BF16),
        compiler_params=_cparams(("parallel", "arbitrary")),
        name="window_attention",
    )(sink, bias, q, kd, kd, kd, kd, vd, vd, vd, vd)


def _out_epilogue(tm, s, y, x_ref, mod_ref, g2_ref, rw_ref, xo_ref, hp_ref, aff_ref):
    x = x_ref[s] + mod_ref[s, 0, 2:3, :] * y
    xo_ref[s] = x
    h2 = _rms_mod(x, g2_ref[...], mod_ref[s, 0, 3:4, :], mod_ref[s, 0, 4:5, :])
    h_hi = h2.astype(BF16)
    h_hi32 = h_hi.astype(F32)
    bits = lax.bitcast_convert_type(h_hi32, jnp.uint32)
    half = D_MODEL // 2
    packed = (bits[:, :half] >> 16) | (bits[:, half:] & jnp.uint32(0xFFFF0000))
    for j in range(HP_CHUNKS):
        hp_ref[s, pl.ds(j, tm, stride=HP_CHUNKS), :] = packed[:, j * LANES:(j + 1) * LANES]
    h_lo = (h2 - h_hi32).astype(BF16)
    logits = (_nt_dot(rw_ref[0], h_hi) + _nt_dot(rw_ref[0], h_lo)) + _nt_dot(rw_ref[1], h_hi)
    mx = jnp.max(logits, axis=0, keepdims=True)
    ex = jnp.exp(logits - mx)
    aff_ref[s] = ex / jnp.sum(ex, axis=0, keepdims=True)


def _out_even_kernel(tm, o_ref, u_ref, vn_ref, ws_ref, bs_ref, wo_ref, x_ref, mod_ref, g2_ref,
                     rw_ref, xo_ref, hp_ref, aff_ref):
    dot = functools.partial(jnp.dot, preferred_element_type=F32)
    gw = B_WIDTH // B_GROUPS
    spb = x_ref.shape[0]
    rows = []
    for s in range(spb):
        for cidx in range(tm // B_CHUNK):
            rs = slice(cidx * B_CHUNK, (cidx + 1) * B_CHUNK)
            cols = []
            for g in range(B_GROUPS):
                cs = slice(g * gw, (g + 1) * gw)
                mixed = dot(ws_ref[g], vn_ref[s, rs, cs]) + bs_ref[:, cs]
                cols.append(u_ref[s, rs, cs] * mixed)
            rows.append(jnp.concatenate(cols, axis=1))
    sg = jnp.concatenate(rows, axis=0).astype(BF16)
    o_all = jnp.concatenate([o_ref[s] for s in range(spb)], axis=0)
    y = dot(o_all, wo_ref[:A_Q_W, :]) + dot(sg, wo_ref[A_Q_W:, :])
    for s in range(spb):
        _out_epilogue(tm, s, y[s * tm:(s + 1) * tm], x_ref, mod_ref, g2_ref, rw_ref, xo_ref, hp_ref,
                      aff_ref)


def _out_specs_common(bsz, t, tm, spb):
    row = lambda w: pl.BlockSpec((spb, tm, w), lambda b, i: (b, i, 0))
    out_specs = [row(D_MODEL),
                 pl.BlockSpec((spb, tm * HP_CHUNKS, LANES), lambda b, i: (b, i, 0)),
                 pl.BlockSpec((spb, N_EXPERTS, tm), lambda b, i: (b, 0, i))]
    out_shape = [jax.ShapeDtypeStruct((bsz, t, D_MODEL), F32),
                 jax.ShapeDtypeStruct((bsz, t * HP_CHUNKS, LANES), jnp.uint32),
                 jax.ShapeDtypeStruct((bsz, N_EXPERTS, t), F32)]
    return out_specs, out_shape


def _out_even(o, u, vn, ws, bs, wo, x, mod, g2, rw_t, n_lat):
    bsz, t, _ = x.shape
    tm = ROW_TILE
    is_ctx = _tile_is_ctx(n_lat // tm)
    spb = _samples_per_step(bsz, 2 * SAMPLES_PER_STEP)
    row = lambda w: pl.BlockSpec((spb, tm, w), lambda b, i: (b, i, 0))
    full = lambda a: pl.BlockSpec(a.shape, lambda b, i: (0,) * a.ndim)
    mod_spec = pl.BlockSpec((spb, 1, MOD_ROWS, D_MODEL), lambda b, i: (b, is_ctx(i), 0, 0))
    out_specs, out_shape = _out_specs_common(bsz, t, tm, spb)
    return pl.pallas_call(
        functools.partial(_out_even_kernel, tm),
        grid=(bsz // spb, t // tm),
        in_specs=[row(A_Q_W), row(B_WIDTH), row(B_WIDTH), full(ws), full(bs), full(wo),
                  row(D_MODEL), mod_spec, full(g2), full(rw_t)],
        out_specs=out_specs, out_shape=out_shape,
        compiler_params=_cparams(("parallel", "arbitrary")),
        name="out_proj_even",
    )(o, u, vn, ws, bs, wo, x, mod, g2, rw_t)


def _in_odd_kernel(has_moe, tm, *refs):
    (x_ref, moe_ref, modp_ref, mod_ref, g_ref, w_ref, cos_ref, sin_ref, qg_ref, kvg_ref,
     wuq_ref, wukv_ref, xo_ref, q_ref, k_ref, v_ref, gb_ref, u_ref) = refs
    cos = cos_ref[...]
    sin = sin_ref[...]
    lane = lax.broadcasted_iota(jnp.int32, (tm, LANES), 1)
    half = C_ROPE // 2
    first = lane < C_NOPE + half
    cqo = C_KV_LORA + LANES
    co = cqo + C_Q_LORA
    scale = (C_NOPE + C_ROPE) ** -0.5 * LOG2_E
    ones_col = jnp.where(lane == C_V, 1.0, 0.0)
    voff = C_HEADS * HEAD_PAD
    for s in range(x_ref.shape[0]):
        x = _residual_from_moe(x_ref[s], moe_ref, s, modp_ref[s, 0, 5:6, :], tm)
        xo_ref[s] = x
        h = _rms_mod(x, g_ref[...], mod_ref[s, 0, 0:1, :], mod_ref[s, 0, 1:2, :])
        z = jnp.dot(h.astype(BF16), w_ref[...], preferred_element_type=F32)
        ckv = _rms(z[:, :C_KV_LORA], kvg_ref[...])
        kr = _rope_pairs(z[:, C_KV_LORA:C_KV_LORA + LANES], cos, sin, half, first)
        cq = _rms(z[:, cqo:cqo + C_Q_LORA], qg_ref[...])
        kv = jnp.dot(ckv.astype(BF16), wukv_ref[...], preferred_element_type=F32)
        qq = jnp.dot(cq.astype(BF16), wuq_ref[...], preferred_element_type=F32)
        for hd in range(C_HEADS):
            sl = slice(hd * HEAD_PAD, (hd + 1) * HEAD_PAD)
            q_ref[s, :, sl] = (_rope_pairs(qq[:, sl], cos, sin, half, first) * scale).astype(BF16)
            k_ref[s, :, sl] = (kv[:, sl] + kr).astype(BF16)
            v_ref[s, :, sl] = (kv[:, voff + hd * HEAD_PAD:voff + (hd + 1) * HEAD_PAD]
                               + ones_col).astype(BF16)
        gb_ref[s] = z[:, co:co + D_WIDTH]
        u_ref[s] = z[:, co + D_WIDTH:co + 2 * D_WIDTH] * z[:, co + 2 * D_WIDTH:co + 3 * D_WIDTH]


def _in_odd(x, moe, modp, mod, g1, w_in, cos, sin, qg, kvg, wuq, wukv, n_lat):
    bsz, t, _ = x.shape
    tm = ROW_TILE
    is_ctx = _tile_is_ctx(n_lat // tm)
    spb = SAMPLES_PER_STEP
    row = lambda w: pl.BlockSpec((spb, tm, w), lambda b, i: (b, i, 0))
    mod_spec = pl.BlockSpec((spb, 1, MOD_ROWS, D_MODEL), lambda b, i: (b, is_ctx(i), 0, 0))
    full2 = lambda a: pl.BlockSpec(a.shape, lambda b, i: (0, 0))
    hw = C_HEADS * HEAD_PAD
    return pl.pallas_call(
        functools.partial(_in_odd_kernel, True, tm),
        grid=(bsz // spb, t // tm),
        in_specs=[row(D_MODEL),
                  pl.BlockSpec((spb, tm * ACC_CHUNKS, LANES), lambda b, i: (b, i, 0)),
                  mod_spec, mod_spec, full2(g1), full2(w_in),
                  pl.BlockSpec((tm, LANES), lambda b, i: (i, 0)),
                  pl.BlockSpec((tm, LANES), lambda b, i: (i, 0)),
                  full2(qg), full2(kvg), full2(wuq), full2(wukv)],
        out_specs=[row(D_MODEL), row(hw), row(hw), row(hw), row(D_WIDTH), row(D_WIDTH)],
        out_shape=[jax.ShapeDtypeStruct((bsz, t, D_MODEL), F32),
                   jax.ShapeDtypeStruct((bsz, t, hw), BF16),
                   jax.ShapeDtypeStruct((bsz, t, hw), BF16),
                   jax.ShapeDtypeStruct((bsz, t, hw), BF16),
                   jax.ShapeDtypeStruct((bsz, t, D_WIDTH), F32),
                   jax.ShapeDtypeStruct((bsz, t, D_WIDTH), F32)],
        compiler_params=_cparams(("parallel", "arbitrary")),
        name="in_proj_odd",
    )(x, moe, modp, mod, g1, w_in, cos, sin, qg, kvg, wuq, wukv)


def _dense_attn_kernel(tk, q_ref, k_ref, v_ref, o_ref):
    tq = q_ref.shape[1]
    nk = k_ref.shape[1] // tk
    for hd in range(q_ref.shape[2] // HEAD_PAD):
        hs = slice(hd * HEAD_PAD, (hd + 1) * HEAD_PAD)
        q = q_ref[0, :, hs]
        m = jnp.full((tq, 1), NEG_INF, F32)
        acc = jnp.zeros((tq, HEAD_PAD), F32)
        for j in range(nk):
            s = _nt_dot(q, k_ref[0, j * tk:(j + 1) * tk, hs])
            m_new = jnp.maximum(m, jnp.max(s, axis=-1, keepdims=True))
            alpha = jnp.exp2(m - m_new)
            p = jnp.exp2(s - m_new)
            acc = alpha * acc + jnp.dot(p.astype(BF16), v_ref[0, j * tk:(j + 1) * tk, hs],
                                        preferred_element_type=F32)
            m = m_new
        o_ref[0, :, hs] = (acc / acc[:, C_V:C_V + 1]).astype(BF16)


def _dense_attn(q, k, v, q_start, q_len, k_start, k_len, tq, heads_per_step):
    bsz, _, hw = q.shape
    tk = ROW_TILE
    q0 = q_start // tq
    kb = k_start // k_len
    hblk = heads_per_step * HEAD_PAD
    return pl.pallas_call(
        functools.partial(_dense_attn_kernel, tk),
        grid=(bsz, C_HEADS // heads_per_step, q_len // tq),
        in_specs=[pl.BlockSpec((1, tq, hblk), lambda b, h, i: (b, q0 + i, h)),
                  pl.BlockSpec((1, k_len, hblk), lambda b, h, i: (b, kb, h)),
                  pl.BlockSpec((1, k_len, hblk), lambda b, h, i: (b, kb, h))],
        out_specs=pl.BlockSpec((1, tq, hblk), lambda b, h, i: (b, i, h)),
        out_shape=jax.ShapeDtypeStruct((bsz, q_len, hw), BF16),
        compiler_params=_cparams(("parallel", "parallel", "arbitrary")),
        name="dense_attention",
    )(q, k, v)


def _out_odd_kernel(tm, n_lat_tiles, n_tiles, ol_ref, oc_ref, gb_ref, u_ref, up_ref, un_ref, cw_ref,
                    wo_ref, x_ref, mod_ref, g2_ref, rw_ref, xo_ref, hp_ref, aff_ref):
    i = pl.program_id(1)
    dot = functools.partial(jnp.dot, preferred_element_type=F32)
    has_prev = jnp.logical_and(i != 0, i != n_lat_tiles)
    has_next = jnp.logical_and(i != n_lat_tiles - 1, i != n_tiles - 1)
    hw = C_HEADS * HEAD_PAD
    spb = x_ref.shape[0]
    o_att = []
    cs = []
    for s in range(spb):
        o_att.append(jnp.where(i < n_lat_tiles, ol_ref[s], oc_ref[s]))
        u = u_ref[s]
        prev_row = jnp.where(has_prev, up_ref[s, SUBLANES - 1:SUBLANES, :], 0.0)
        next_row = jnp.where(has_next, un_ref[s, 0:1, :], 0.0)
        ridx = lax.broadcasted_iota(jnp.int32, u.shape, 0)
        u_m1 = jnp.where(ridx == 0, prev_row, pltpu.roll(u, 1, 0))
        u_p1 = jnp.where(ridx == tm - 1, next_row, pltpu.roll(u, tm - 1, 0))
        conv = u_m1 * cw_ref[0:1, :] + u * cw_ref[1:2, :] + u_p1 * cw_ref[2:3, :]
        cs.append((gb_ref[s] * conv).astype(BF16))
    y = (dot(jnp.concatenate(o_att, axis=0), wo_ref[:hw, :])
         + dot(jnp.concatenate(cs, axis=0), wo_ref[hw:, :]))
    for s in range(spb):
        _out_epilogue(tm, s, y[s * tm:(s + 1) * tm], x_ref, mod_ref, g2_ref, rw_ref, xo_ref, hp_ref,
                      aff_ref)


def _out_odd(o_lat, o_ctx, gb, u, cw, wo, x, mod, g2, rw_t, n_lat):
    bsz, t, _ = x.shape
    tm = ROW_TILE
    nt = t // tm
    nlt = n_lat // tm
    is_ctx = _tile_is_ctx(nlt)
    spb = _samples_per_step(bsz, 2 * SAMPLES_PER_STEP)
    row = lambda w: pl.BlockSpec((spb, tm, w), lambda b, i: (b, i, 0))
    full = lambda a: pl.BlockSpec(a.shape, lambda b, i: (0,) * a.ndim)
    mod_spec = pl.BlockSpec((spb, 1, MOD_ROWS, D_MODEL), lambda b, i: (b, is_ctx(i), 0, 0))
    hw = C_HEADS * HEAD_PAD
    per = tm // SUBLANES
    last = t // SUBLANES - 1
    halo_prev = pl.BlockSpec((spb, SUBLANES, D_WIDTH), lambda b, i: (b, jnp.maximum(i * per - 1, 0), 0))
    halo_next = pl.BlockSpec((spb, SUBLANES, D_WIDTH), lambda b, i: (b, jnp.minimum((i + 1) * per, last), 0))
    out_specs, out_shape = _out_specs_common(bsz, t, tm, spb)
    return pl.pallas_call(
        functools.partial(_out_odd_kernel, tm, nlt, nt),
        grid=(bsz // spb, nt),
        in_specs=[pl.BlockSpec((spb, tm, hw), lambda b, i: (b, jnp.minimum(i, nlt - 1), 0)),
                  pl.BlockSpec((spb, tm, hw), lambda b, i: (b, jnp.maximum(i - nlt, 0), 0)),
                  row(D_WIDTH), row(D_WIDTH), halo_prev, halo_next,
                  full(cw), full(wo), row(D_MODEL), mod_spec, full(g2), full(rw_t)],
        out_specs=out_specs, out_shape=out_shape,
        compiler_params=_cparams(("parallel", "arbitrary")),
        name="out_proj_odd",
    )(o_lat, o_ctx, gb, u, u, u, cw, wo, x, mod, g2, rw_t)


def _route_kernel(cap, nblk, width, a_ref, idx_ref, gate_ref):
    n_e = N_EXPERTS
    a = a_ref[0]
    bits = lax.bitcast_convert_type(a, jnp.int32)

    def count(mask):
        c = jnp.sum(jnp.where(mask, 1.0, 0.0), axis=2, keepdims=True)
        return jnp.sum(c, axis=1, keepdims=True)

    def search(it, thr):
        cand = thr | jnp.left_shift(jnp.int32(1), 30 - it)
        return jnp.where(count(bits >= cand) >= cap, cand, thr)

    thr = lax.fori_loop(0, 31, search, jnp.zeros((n_e, 1, 1), jnp.int32))
    gt = bits > thr
    eq = bits == thr
    need = cap - count(gt)

    rows = n_e * nblk
    li = lax.broadcasted_iota(jnp.int32, (width, width), 0)
    lj = lax.broadcasted_iota(jnp.int32, (width, width), 1)
    upper = jnp.where(li <= lj, 1.0, 0.0).astype(BF16)
    ones_sq = jnp.ones((width, width), BF16)
    if nblk > 1:
        ri = lax.broadcasted_iota(jnp.int32, (rows, rows), 0)
        rj = lax.broadcasted_iota(jnp.int32, (rows, rows), 1)
        lower = jnp.where(jnp.logical_and(ri // nblk == rj // nblk, rj < ri), 1.0, 0.0).astype(BF16)

    def prefix(mask3):
        m2 = jnp.where(mask3, 1.0, 0.0).astype(BF16).reshape(rows, width)
        local = jnp.dot(m2, upper, preferred_element_type=F32)
        total = jnp.dot(m2, ones_sq, preferred_element_type=F32)
        if nblk > 1:
            excl = jnp.dot(lower, total.astype(BF16), preferred_element_type=F32)
        else:
            excl = jnp.zeros_like(total)
        return local, total, excl

    l_eq, _, x_eq = prefix(eq)
    tie_rank = (l_eq + x_eq).reshape(n_e, nblk, width)
    sel = jnp.logical_or(gt, jnp.logical_and(eq, tie_rank <= need))
    local, total, excl = prefix(sel)
    pin = local + excl

    slot_l = lax.broadcasted_iota(jnp.int32, (nblk, cap), 1).astype(F32)
    blk_s = lax.broadcasted_iota(jnp.int32, (nblk, cap), 0).astype(F32)
    slot_s = lax.broadcasted_iota(jnp.int32, (cap, width), 0).astype(F32)
    lane_w = lax.broadcasted_iota(jnp.int32, (cap, width), 1).astype(F32)
    ones_r = jnp.ones((SUBLANES, width), BF16)
    reps = cap // width if cap >= width else 1
    a2 = a.reshape(rows, width)
    if nblk > 1:
        pin_hi = jnp.floor(pin * (1.0 / PIN_SPLIT))
        a_1 = a2.astype(BF16)
        rem = a2 - a_1.astype(F32)
        a_2 = rem.astype(BF16)
        a_3 = (rem - a_2.astype(F32)).astype(BF16)
        table = jnp.concatenate([pin_hi.astype(BF16), (pin - PIN_SPLIT * pin_hi).astype(BF16),
                                 a_1, a_2, a_3], axis=1)
    for e in range(n_e):
        rs = slice(e * nblk, (e + 1) * nblk)
        if nblk > 1:
            lo = jnp.concatenate([excl[rs]] * reps, axis=1)[:, :cap]
            hi = lo + jnp.concatenate([total[rs]] * reps, axis=1)[:, :cap]
            oh_t = jnp.where(jnp.logical_and(lo <= slot_l, slot_l < hi), 1.0, 0.0)
            got = _tn_dot(oh_t.astype(BF16), table[rs])
            pin_g = PIN_SPLIT * got[:, :width] + got[:, width:2 * width]
            aff_g = (got[:, 2 * width:3 * width] + got[:, 3 * width:4 * width]) + got[:, 4 * width:]
            blk_row = jnp.sum(oh_t * blk_s, axis=0, keepdims=True)
        else:
            pin_g = jnp.broadcast_to(pin[rs], (cap, width))
            aff_g = jnp.broadcast_to(a2[rs], (cap, width))
            blk_row = jnp.zeros((1, cap), F32)
        ind = jnp.where(pin_g <= slot_s, 1.0, 0.0).astype(BF16)
        cnt_row = _nt_dot(ones_r, ind)[0:1]
        cnt_b = jnp.dot(ind, ones_sq, preferred_element_type=F32)
        picked = jnp.where(lane_w == cnt_b, aff_g, 0.0)
        idx_ref[0, e] = (blk_row * width + cnt_row).astype(jnp.int32)
        gate_ref[0, e] = jnp.sum(picked, axis=1, keepdims=True)


def _route(aff, cap):
    bsz, n_e, n = aff.shape
    width = LANES if n % (LANES * SUBLANES) == 0 else n
    nblk = n // width
    a4 = aff.reshape(bsz, n_e, nblk, width)
    idx, gate = pl.pallas_call(
        functools.partial(_route_kernel, cap, nblk, width),
        grid=(bsz,),
        in_specs=[pl.BlockSpec((1, n_e, nblk, width), lambda b: (b, 0, 0, 0))],
        out_specs=[pl.BlockSpec((1, n_e, 1, cap), lambda b: (b, 0, 0, 0)),
                   pl.BlockSpec((1, n_e, cap, 1), lambda b: (b, 0, 0, 0))],
        out_shape=[jax.ShapeDtypeStruct((bsz, n_e, 1, cap), jnp.int32),
                   jax.ShapeDtypeStruct((bsz, n_e, cap, 1), F32)],
        compiler_params=_cparams(("parallel",)),
        name="expert_choice_routing",
    )(a4)
    return idx.reshape(bsz, n_e, cap), gate.reshape(bsz, n_e, cap)


def _ffn_copies(hp_hbm, out_hbm, hsrc, acc, sem_in, sem_out, grp, nsub, n_per, row0):
    ins = []
    outs = []
    for sb in range(nsub):
        b = grp * nsub + sb
        ins.append(pltpu.make_async_copy(
            hp_hbm.at[b, pl.ds(row0 * HP_CHUNKS, n_per * HP_CHUNKS), :],
            hsrc.at[pl.ds(sb * n_per * HP_CHUNKS, n_per * HP_CHUNKS), :], sem_in.at[sb]))
        outs.append(pltpu.make_async_copy(
            acc.at[pl.ds(sb * n_per * ACC_CHUNKS, n_per * ACC_CHUNKS), :],
            out_hbm.at[b, pl.ds(row0 * ACC_CHUNKS, n_per * ACC_CHUNKS), :], sem_out.at[sb]))
    return ins, outs


def _ffn_gather(idx_ref, hsrc, xt, stride, mi):
    src = pl.multiple_of(idx_ref[0, 0, 0, mi] * HP_CHUNKS, HP_CHUNKS)
    xt[pl.ds(mi, HP_CHUNKS, stride=stride), :] = hsrc[pl.ds(src, HP_CHUNKS), :]


def _ffn_scatter(idx_ref, gate_ref, live, spare_row, acc, yt, stride, mis):
    dst = []
    val = []
    for mi in mis:
        d = idx_ref[0, 0, 0, mi] * ACC_CHUNKS
        if live is not None:
            d = jnp.where(live, d, spare_row)
        d = pl.multiple_of(d, ACC_CHUNKS)
        gate = gate_ref[0, 0, 0, mi]
        slab = yt[pl.ds(mi, ACC_CHUNKS, stride=stride), :] * gate
        dst.append(d)
        val.append(acc[pl.ds(d, ACC_CHUNKS), :] + slab)
    for d, v in zip(dst, val):
        acc[pl.ds(d, ACC_CHUNKS), :] = v


def _ffn_kernel(m_slots, nsub, n_per, row0, unroll, idx_ref, idxn_ref, idxp_ref, gate_ref, gatep_ref,
                hp_hbm, w1a_ref, w1b_ref, w3a_ref, w3b_ref, w2a_ref, w2b_ref, _moe_in, out_hbm,
                hsrc, acc, xt, xs_buf, act_buf, yt, sem_in, sem_out):
    grp = pl.program_id(0)
    e = pl.program_id(1)
    stride = m_slots + SUBLANES
    spare_row = nsub * n_per * ACC_CHUNKS
    ins, outs = _ffn_copies(hp_hbm, out_hbm, hsrc, acc, sem_in, sem_out, grp, nsub, n_per, row0)
    _, prev_outs = _ffn_copies(hp_hbm, out_hbm, hsrc, acc, sem_in, sem_out, jnp.maximum(grp - 1, 0),
                               nsub, n_per, row0)

    @pl.when(jnp.logical_and(grp == 0, e == 0))
    def _():
        yt[...] = jnp.zeros(yt.shape, F32)
        acc[pl.ds(spare_row, SUBLANES), :] = jnp.zeros((SUBLANES, LANES), F32)

    @pl.when(e == 0)
    def _():
        for cp in ins:
            cp.start()
        for cp in ins:
            cp.wait()

        def first_gather(c, carry):
            for k in range(unroll):
                _ffn_gather(idx_ref, hsrc, xt, stride, c * unroll + k)
            return carry

        lax.fori_loop(0, m_slots // unroll, first_gather, 0)

    half = D_MODEL // 2
    for j in range(HP_CHUNKS):
        w = xt[pl.ds(j * stride, m_slots), :]
        xs_buf[:, j * LANES:(j + 1) * LANES] = lax.bitcast_convert_type(w << 16, F32).astype(BF16)
        xs_buf[:, half + j * LANES:half + (j + 1) * LANES] = lax.bitcast_convert_type(
            w & jnp.uint32(0xFFFF0000), F32).astype(BF16)

    for c in range(m_slots // unroll):
        _ffn_scatter(idxp_ref, gatep_ref, e > 0, spare_row, acc, yt, stride,
                     range(c * unroll, (c + 1) * unroll))
    for mi in range(m_slots):
        _ffn_gather(idxn_ref, hsrc, xt, stride, mi)

    dot = functools.partial(jnp.dot, preferred_element_type=F32)
    xs = xs_buf[...]
    fh = EXPERT_FF // 2
    for j, (wa_ref, wb_ref) in enumerate(((w1a_ref, w3a_ref), (w1b_ref, w3b_ref))):
        a = dot(xs, wa_ref[0, 0])
        b = dot(xs, wb_ref[0, 0])
        act_buf[:, j * fh:(j + 1) * fh] = (a * jax.nn.sigmoid(a) * b).astype(BF16)
    seen = xt[pl.ds(0, SUBLANES), :]
    for r in range(SUBLANES, m_slots, SUBLANES):
        seen = seen | xt[pl.ds(r, SUBLANES), :]
    seen = seen | lax.bitcast_convert_type(acc[pl.ds(0, SUBLANES), :], jnp.uint32)
    zero = lax.bitcast_convert_type(seen >> 32, F32)
    tile = (2 * SUBLANES, LANES)
    act_buf[:tile[0], :tile[1]] = (act_buf[:tile[0], :tile[1]].astype(F32)
                                   + jnp.concatenate([zero, zero], axis=0)).astype(BF16)
    y = dot(act_buf[:, :fh], w2a_ref[0, 0]) + dot(act_buf[:, fh:], w2b_ref[0, 0])
    for j in range(ACC_CHUNKS):
        yt[pl.ds(j * stride, m_slots), :] = y[:, j * LANES:(j + 1) * LANES]

    @pl.when(e == 0)
    def _():
        @pl.when(grp > 0)
        def _():
            for cp in prev_outs:
                cp.wait()

        acc[...] = jnp.zeros(acc.shape, F32)

    @pl.when(e == N_EXPERTS - 1)
    def _():
        def last_scatter(c, carry):
            _ffn_scatter(idx_ref, gate_ref, None, spare_row, acc, yt, stride,
                         [c * unroll + k for k in range(unroll)])
            return carry

        lax.fori_loop(0, m_slots // unroll, last_scatter, 0)
        for cp in outs:
            cp.start()

        @pl.when(grp == pl.num_programs(0) - 1)
        def _():
            for cp in outs:
                cp.wait()


def _expert_ffn(idx, gate, hp, w1, w3, w2, layer, moe_buf, nsub, n_per, row0):
    groups, n_e, m_slots = idx.shape
    n_tok = nsub * n_per
    stride = m_slots + SUBLANES
    unroll = 8
    cur = lambda g, e: (g, e, 0, 0)
    nxt = lambda g, e: (g, jnp.minimum(e + 1, n_e - 1), 0, 0)
    prv = lambda g, e: (g, jnp.maximum(e - 1, 0), 0, 0)
    smem = lambda imap: pl.BlockSpec((1, 1, 1, m_slots), imap, memory_space=pltpu.SMEM)
    fh = EXPERT_FF // 2
    w_in = lambda j: pl.BlockSpec((1, 1, D_MODEL, fh), lambda g, e: (layer, e, 0, j))
    w_out = lambda j: pl.BlockSpec((1, 1, fh, D_MODEL), lambda g, e: (layer, e, j, 0))
    idx4 = idx.reshape(groups, n_e, 1, m_slots)
    gate4 = gate.reshape(groups, n_e, 1, m_slots)
    return pl.pallas_call(
        functools.partial(_ffn_kernel, m_slots, nsub, n_per, row0, unroll),
        grid=(groups, n_e),
        in_specs=[smem(cur), smem(nxt), smem(prv), smem(cur), smem(prv),
                  pl.BlockSpec(memory_space=pl.ANY), w_in(0), w_in(1), w_in(0), w_in(1),
                  w_out(0), w_out(1), pl.BlockSpec(memory_space=pl.ANY)],
        out_specs=pl.BlockSpec(memory_space=pl.ANY),
        out_shape=jax.ShapeDtypeStruct(moe_buf.shape, F32),
        scratch_shapes=[pltpu.VMEM((n_tok * HP_CHUNKS, LANES), jnp.uint32),
                        pltpu.VMEM((n_tok * ACC_CHUNKS + SUBLANES, LANES), F32),
                        pltpu.VMEM((HP_CHUNKS * stride, LANES), jnp.uint32),
                        pltpu.VMEM((m_slots, D_MODEL), BF16),
                        pltpu.VMEM((m_slots, EXPERT_FF), BF16),
                        pltpu.VMEM((ACC_CHUNKS * stride, LANES), F32),
                        pltpu.SemaphoreType.DMA((nsub,)),
                        pltpu.SemaphoreType.DMA((nsub,))],
        input_output_aliases={12: 0},
        compiler_params=_cparams(("arbitrary", "arbitrary")),
        name="expert_ffn",
    )(idx4, idx4, idx4, gate4, gate4, hp, w1, w1, w3, w3, w2, w2, moe_buf)


def _final_kernel(tm, x_ref, moe_ref, modp_ref, g_ref, o_ref):
    x = _residual_from_moe(x_ref[0], moe_ref, 0, modp_ref[0, 0, 5:6, :], tm)
    o_ref[0] = _rms(x, g_ref[...])


def _final(x, moe, modp, g, n_lat):
    bsz = x.shape[0]
    tm = 4 * ROW_TILE
    return pl.pallas_call(
        functools.partial(_final_kernel, tm),
        grid=(bsz, n_lat // tm),
        in_specs=[pl.BlockSpec((1, tm, D_MODEL), lambda b, i: (b, i, 0)),
                  pl.BlockSpec((1, tm * ACC_CHUNKS, LANES), lambda b, i: (b, i, 0)),
                  pl.BlockSpec((1, 1, MOD_ROWS, D_MODEL), lambda b, i: (b, 0, 0, 0)),
                  pl.BlockSpec(g.shape, lambda b, i: (0, 0))],
        out_specs=pl.BlockSpec((1, tm, D_MODEL), lambda b, i: (b, i, 0)),
        out_shape=jax.ShapeDtypeStruct((bsz, n_lat, D_MODEL), F32),
        compiler_params=_cparams(("parallel", "arbitrary")),
        name="final_norm",
    )(x, moe, modp, g)


def _rope_angles(rows, rot_dim):
    row = jnp.repeat(jnp.arange(rows, dtype=F32), GRID_W)
    col = jnp.tile(jnp.arange(GRID_W, dtype=F32), rows)
    n_freq = rot_dim // 4
    inv_freq = ROPE_BASE ** (-jnp.arange(n_freq, dtype=F32) / n_freq)
    ang = jnp.concatenate([row[:, None] * inv_freq[None, :], col[:, None] * inv_freq[None, :]], axis=-1)
    return jnp.cos(ang), jnp.sin(ang)


def _rope_tables_even(n_lat, n_ctx):
    cos, sin = _rope_angles(n_lat // GRID_W, A_HEAD_DIM)
    cos_h = jnp.concatenate([cos, cos], axis=1)
    sin_h = jnp.concatenate([-sin, sin], axis=1)
    reps = LANES // A_HEAD_DIM
    cos_l = jnp.tile(cos_h, (1, reps))
    sin_l = jnp.tile(sin_h, (1, reps))
    return (jnp.concatenate([cos_l, jnp.ones((n_ctx, LANES), F32)], axis=0),
            jnp.concatenate([sin_l, jnp.zeros((n_ctx, LANES), F32)], axis=0))


def _rope_tables_odd(n_lat, n_ctx):
    cos, sin = _rope_angles(n_lat // GRID_W, C_ROPE)
    n = cos.shape[0]
    pad = LANES - C_NOPE - C_ROPE
    cos_l = jnp.concatenate([jnp.ones((n, C_NOPE), F32), cos, cos, jnp.ones((n, pad), F32)], axis=1)
    sin_l = jnp.concatenate([jnp.zeros((n, C_NOPE), F32), -sin, sin, jnp.zeros((n, pad), F32)], axis=1)
    return (jnp.concatenate([cos_l, jnp.ones((n_ctx, LANES), F32)], axis=0),
            jnp.concatenate([sin_l, jnp.zeros((n_ctx, LANES), F32)], axis=0))


def _even_w_in(w):
    k = w[:, :A_KV_W]
    v = w[:, A_KV_W:2 * A_KV_W]
    dup = lambda a: jnp.concatenate(
        [a[:, h * A_HEAD_DIM:(h + 1) * A_HEAD_DIM] for h in range(A_KV_HEADS) for _ in range(2)], axis=1)
    return jnp.concatenate([dup(k), dup(v), w[:, 2 * A_KV_W:]], axis=1).astype(BF16)


def _odd_w_in(w):
    d = w.shape[0]
    kr = w[:, C_KV_LORA:C_KV_LORA + C_ROPE]
    kr_group = jnp.concatenate(
        [jnp.zeros((d, C_NOPE), F32), kr, jnp.zeros((d, LANES - C_NOPE - C_ROPE), F32)], axis=1)
    return jnp.concatenate([w[:, :C_KV_LORA], kr_group, w[:, C_KV_LORA + C_ROPE:]], axis=1).astype(BF16)


def _odd_w_uq(w):
    r = w.shape[0]
    w3 = w.reshape(r, C_HEADS, C_NOPE + C_ROPE)
    pad = jnp.zeros((r, C_HEADS, HEAD_PAD - C_NOPE - C_ROPE), F32)
    return jnp.concatenate([w3, pad], axis=2).reshape(r, C_HEADS * HEAD_PAD).astype(BF16)


def _odd_w_ukv(w):
    r = w.shape[0]
    w3 = w.reshape(r, C_HEADS, C_NOPE + C_V)
    kpad = jnp.zeros((r, C_HEADS, HEAD_PAD - C_NOPE), F32)
    vpad = jnp.zeros((r, C_HEADS, HEAD_PAD - C_V), F32)
    kpart = jnp.concatenate([w3[:, :, :C_NOPE], kpad], axis=2).reshape(r, C_HEADS * HEAD_PAD)
    vpart = jnp.concatenate([w3[:, :, C_NOPE:], vpad], axis=2).reshape(r, C_HEADS * HEAD_PAD)
    return jnp.concatenate([kpart, vpart], axis=1).astype(BF16)


def _odd_w_out(w):
    d = w.shape[1]
    att = w[:C_HEADS * C_V].reshape(C_HEADS, C_V, d)
    att = jnp.concatenate([att, jnp.zeros((C_HEADS, HEAD_PAD - C_V, d), F32)], axis=1)
    return jnp.concatenate([att.reshape(C_HEADS * HEAD_PAD, d), w[C_HEADS * C_V:]], axis=0).astype(BF16)


def _mod_table(mods_l, bsz):
    lat = mods_l[:bsz].reshape(bsz, 1, 6, D_MODEL)
    ctx = jnp.broadcast_to(mods_l[bsz].reshape(1, 1, 6, D_MODEL), (bsz, 1, 6, D_MODEL))
    tab = jnp.concatenate([lat, ctx], axis=1)
    return jnp.pad(tab, ((0, 0), (0, 0), (0, MOD_ROWS - 6), (0, 0)))


def kernel(x, c, ctx, c_ctx, mod_w, mod_b, norm1_g, norm2_g, ev_w_in, ev_sink, ev_sgu_norm_g, ev_sgu_w, ev_sgu_b, ev_w_out, od_w_in, od_q_norm_g, od_w_uq, od_kv_norm_g, od_w_ukv, od_conv_w, od_w_out, router_w, exp_w1, exp_w3, exp_w2, final_g):
    bsz, n_lat, _ = x.shape
    n_ctx = ctx.shape[1]
    t = n_lat + n_ctx
    cap_lat = EC_FACTOR * n_lat // N_EXPERTS
    cap_ctx = EC_FACTOR * n_ctx // N_EXPERTS

    mod_rows = -(-(bsz + 1) // SUBLANES) * SUBLANES
    cc = jnp.concatenate([c, c_ctx[None, :], jnp.zeros((mod_rows - bsz - 1, D_MODEL), F32)], axis=0)
    mods = _modulation(cc, mod_w, mod_b)
    tabs = [_mod_table(mods[l], bsz) for l in range(DEPTH)]

    cos_e, sin_e = _rope_tables_even(n_lat, n_ctx)
    cos_o, sin_o = _rope_tables_odd(n_lat, n_ctx)

    xs = jnp.concatenate([x, ctx], axis=1)
    w1 = exp_w1.astype(BF16)
    w3 = exp_w3.astype(BF16)
    w2 = exp_w2.astype(BF16)
    moe = None
    for layer in range(DEPTH):
        i = layer // 2
        need_ctx = layer < DEPTH - 1
        g1 = norm1_g[layer][None, :]
        g2 = norm2_g[layer][None, :]
        rw_f = router_w[layer].T
        rw_hi = rw_f.astype(BF16)
        rw_t = jnp.stack([rw_hi, (rw_f - rw_hi.astype(F32)).astype(BF16)])
        modp = tabs[layer - 1] if layer > 0 else None
        if layer % 2 == 0:
            xs, q, kd, vd, u, vn = _in_even(xs, moe, modp, tabs[layer], g1, _even_w_in(ev_w_in[i]),
                                            cos_e, sin_e, ev_sgu_norm_g[i][None, :], n_lat)
            o = _win_attn(ev_sink[i], q, kd, vd, n_lat, n_ctx)
            bs = jnp.repeat(ev_sgu_b[i].T, B_WIDTH // B_GROUPS, axis=1)
            xs, hp, aff = _out_even(o, u, vn, ev_sgu_w[i].astype(BF16), bs, ev_w_out[i].astype(BF16),
                                    xs, tabs[layer], g2, rw_t, n_lat)
        else:
            xs, q, k, v, gb, u = _in_odd(xs, moe, modp, tabs[layer], g1, _odd_w_in(od_w_in[i]),
                                         cos_o, sin_o, od_q_norm_g[i][None, :], od_kv_norm_g[i][None, :],
                                         _odd_w_uq(od_w_uq[i]), _odd_w_ukv(od_w_ukv[i]), n_lat)
            o_lat = _dense_attn(q, k, v, 0, n_lat, 0, t, 4 * ROW_TILE, 1)
            o_ctx = _dense_attn(q, k, v, n_lat, n_ctx, n_lat, n_ctx, n_ctx, C_HEADS)
            xs, hp, aff = _out_odd(o_lat, o_ctx, gb, u, od_conv_w[i], _odd_w_out(od_w_out[i]), xs, tabs[layer],
                                   g2, rw_t, n_lat)
        if moe is None:
            moe = jnp.zeros((bsz, t * ACC_CHUNKS, LANES), F32)
        idx, gate = _route(aff[:, :, :n_lat], cap_lat)
        moe = _expert_ffn(idx, gate, hp, w1, w3, w2, layer, moe, 1, n_lat, 0)
        if need_ctx:
            idx_c, gate_c = _route(aff[:, :, n_lat:], cap_ctx)
            idx_c = idx_c + (jnp.arange(bsz, dtype=jnp.int32) * n_ctx)[:, None, None]
            idx_c = jnp.transpose(idx_c, (1, 0, 2)).reshape(1, N_EXPERTS, bsz * cap_ctx)
            gate_c = jnp.transpose(gate_c, (1, 0, 2)).reshape(1, N_EXPERTS, bsz * cap_ctx)
            moe = _expert_ffn(idx_c, gate_c, hp, w1, w3, w2, layer, moe, bsz, n_ctx, n_lat)
    return _final(xs, moe, tabs[DEPTH - 1], final_g[None, :], n_lat)
```

```python
import functools

import jax
import jax.numpy as jnp
from jax import lax
from jax.experimental import pallas as pl
from jax.experimental.pallas import tpu as pltpu

F32 = jnp.float32
BF16 = jnp.bfloat16
HIGHEST = lax.Precision.HIGHEST

D_MODEL = 1024
DEPTH = 4
GRID_W = 64
NORM_EPS = 1e-6
ROPE_BASE = 10000.0
NEG_INF = -1e30
LOG2_E = 1.4426950408889634
PIN_SPLIT = 64.0

A_HEADS = 8
A_KV_HEADS = 2
A_HEAD_DIM = 64
WINDOW = 128
A_BLOCK = 128
B_WIDTH = 512
B_GROUPS = 4
B_CHUNK = 128
C_HEADS = 8
C_Q_LORA = 384
C_KV_LORA = 256
C_NOPE = 64
C_ROPE = 32
C_V = 64
D_WIDTH = 512
D_CONV = 3
N_EXPERTS = 16
EXPERT_FF = 1024
EC_FACTOR = 2

A_Q_W = A_HEADS * A_HEAD_DIM
A_KV_W = A_KV_HEADS * A_HEAD_DIM

LANES = 128
SUBLANES = 8
ROW_TILE = 256
SAMPLES_PER_STEP = 2
MOD_ROWS = 8
HP_CHUNKS = D_MODEL // (2 * LANES)
ACC_CHUNKS = D_MODEL // LANES
VMEM_LIMIT = 56 * 1024 * 1024

EVEN_COLS = 2 * A_KV_W + 2 * A_KV_W + A_Q_W + 2 * B_WIDTH
ODD_COLS = C_KV_LORA + LANES + C_Q_LORA + 3 * D_WIDTH
HEAD_PAD = LANES


def _cparams(sem):
    return pltpu.CompilerParams(dimension_semantics=sem, vmem_limit_bytes=VMEM_LIMIT)


def _nt_dot(a, b, precision=None):
    return lax.dot_general(a, b, (((1,), (1,)), ((), ())), precision=precision,
                           preferred_element_type=F32)


def _tn_dot(a, b, precision=None):
    return lax.dot_general(a, b, (((0,), (0,)), ((), ())), precision=precision,
                           preferred_element_type=F32)


def _mod_kernel(cc_ref, w_ref, b_ref, o_ref):
    cc = cc_ref[...]
    s = cc * jax.nn.sigmoid(cc)
    o_ref[0] = jnp.dot(s, w_ref[0], precision=HIGHEST, preferred_element_type=F32) + b_ref[0]


def _modulation(cc, mod_w, mod_b):
    rows = cc.shape[0]
    ncol = mod_w.shape[2] // D_MODEL
    return pl.pallas_call(
        _mod_kernel,
        grid=(DEPTH, ncol),
        in_specs=[
            pl.BlockSpec((rows, D_MODEL), lambda l, j: (0, 0)),
            pl.BlockSpec((1, D_MODEL, D_MODEL), lambda l, j: (l, 0, j)),
            pl.BlockSpec((1, 1, D_MODEL), lambda l, j: (l, 0, j)),
        ],
        out_specs=pl.BlockSpec((1, rows, D_MODEL), lambda l, j: (l, 0, j)),
        out_shape=jax.ShapeDtypeStruct((DEPTH, rows, mod_w.shape[2]), F32),
        compiler_params=_cparams(("arbitrary", "arbitrary")),
        name="adaln_modulation",
    )(cc, mod_w, mod_b.reshape(DEPTH, 1, -1))


def _residual_from_moe(x, moe_ref, s, gate_row, tm):
    cols = []
    for j in range(ACC_CHUNKS):
        chunk = moe_ref[s, pl.ds(j, tm, stride=ACC_CHUNKS), :]
        sl = slice(j * LANES, (j + 1) * LANES)
        cols.append(x[:, sl] + gate_row[:, sl] * chunk)
    return jnp.concatenate(cols, axis=1)


def _rms_mod(x, g, shift, scale):
    gain = g * (1.0 + scale)
    return x * lax.rsqrt(jnp.mean(x * x, axis=-1, keepdims=True) + NORM_EPS) * gain + shift


def _rms(x, g):
    return x * lax.rsqrt(jnp.mean(x * x, axis=-1, keepdims=True) + NORM_EPS) * g


def _rope_pairs(a, cos, sin, half, first):
    rot = jnp.where(first, pltpu.roll(a, LANES - half, 1), pltpu.roll(a, half, 1))
    return a * cos + rot * sin


def _in_even_kernel(has_moe, tm, *refs):
    if has_moe:
        (x_ref, moe_ref, modp_ref, mod_ref, g_ref, w_ref, cos_ref, sin_ref, lng_ref,
         xo_ref, q_ref, kd_ref, vd_ref, u_ref, vn_ref) = refs
    else:
        (x_ref, mod_ref, g_ref, w_ref, cos_ref, sin_ref, lng_ref,
         q_ref, kd_ref, vd_ref, u_ref, vn_ref) = refs
    cos = cos_ref[...]
    sin = sin_ref[...]
    lane = lax.broadcasted_iota(jnp.int32, (tm, LANES), 1)
    first = (lane % A_HEAD_DIM) < (A_HEAD_DIM // 2)
    half = A_HEAD_DIM // 2
    qoff = 4 * LANES
    uoff = qoff + A_Q_W
    scale = A_HEAD_DIM ** -0.5 * LOG2_E
    for s in range(x_ref.shape[0]):
        x = x_ref[s]
        if has_moe:
            x = _residual_from_moe(x, moe_ref, s, modp_ref[s, 0, 5:6, :], tm)
            xo_ref[s] = x
        h = _rms_mod(x, g_ref[...], mod_ref[s, 0, 0:1, :], mod_ref[s, 0, 1:2, :])
        z = jnp.dot(h.astype(BF16), w_ref[...], preferred_element_type=F32)
        for j in range(2):
            sl = slice(j * LANES, (j + 1) * LANES)
            kd_ref[s, :, sl] = _rope_pairs(z[:, sl], cos, sin, half, first).astype(BF16)
        vd_ref[s] = z[:, 2 * LANES:4 * LANES].astype(BF16)
        for j in range(A_Q_W // LANES):
            sl = slice(qoff + j * LANES, qoff + (j + 1) * LANES)
            q_ref[s, :, j * LANES:(j + 1) * LANES] = (
                _rope_pairs(z[:, sl], cos, sin, half, first) * scale).astype(BF16)
        u_ref[s] = jax.nn.gelu(z[:, uoff:uoff + B_WIDTH])
        gv = jax.nn.gelu(z[:, uoff + B_WIDTH:uoff + 2 * B_WIDTH])
        mu = jnp.mean(gv, axis=-1, keepdims=True)
        var = jnp.mean(jnp.square(gv - mu), axis=-1, keepdims=True)
        vn_ref[s] = ((gv - mu) * lax.rsqrt(var + NORM_EPS) * lng_ref[...]).astype(BF16)


def _samples_per_step(bsz, want):
    return want if bsz % want == 0 else SAMPLES_PER_STEP


def _tile_is_ctx(n_lat_tiles):
    return lambda i: jnp.where(i >= n_lat_tiles, 1, 0)


def _in_even(x, moe, modp, mod, g1, w_in, cos, sin, lng, n_lat):
    bsz, t, _ = x.shape
    tm = ROW_TILE
    nt = t // tm
    is_ctx = _tile_is_ctx(n_lat // tm)
    has_moe = moe is not None
    spb = _samples_per_step(bsz, 2 * SAMPLES_PER_STEP)
    row = lambda w: pl.BlockSpec((spb, tm, w), lambda b, i: (b, i, 0))
    mod_spec = pl.BlockSpec((spb, 1, MOD_ROWS, D_MODEL), lambda b, i: (b, is_ctx(i), 0, 0))
    full2 = lambda a: pl.BlockSpec(a.shape, lambda b, i: (0, 0))
    in_specs = [row(D_MODEL)]
    args = [x]
    if has_moe:
        in_specs += [pl.BlockSpec((spb, tm * ACC_CHUNKS, LANES), lambda b, i: (b, i, 0)), mod_spec]
        args += [moe, modp]
    in_specs += [mod_spec, full2(g1), full2(w_in),
                 pl.BlockSpec((tm, LANES), lambda b, i: (i, 0)),
                 pl.BlockSpec((tm, LANES), lambda b, i: (i, 0)), full2(lng)]
    args += [mod, g1, w_in, cos, sin, lng]
    out_specs = []
    out_shape = []
    if has_moe:
        out_specs.append(row(D_MODEL))
        out_shape.append(jax.ShapeDtypeStruct((bsz, t, D_MODEL), F32))
    out_specs += [row(A_Q_W), row(2 * LANES), row(2 * LANES), row(B_WIDTH), row(B_WIDTH)]
    out_shape += [jax.ShapeDtypeStruct((bsz, t, A_Q_W), BF16),
                  jax.ShapeDtypeStruct((bsz, t, 2 * LANES), BF16),
                  jax.ShapeDtypeStruct((bsz, t, 2 * LANES), BF16),
                  jax.ShapeDtypeStruct((bsz, t, B_WIDTH), F32),
                  jax.ShapeDtypeStruct((bsz, t, B_WIDTH), BF16)]
    outs = pl.pallas_call(
        functools.partial(_in_even_kernel, has_moe, tm),
        grid=(bsz // spb, nt), in_specs=in_specs, out_specs=out_specs, out_shape=out_shape,
        compiler_params=_cparams(("parallel", "arbitrary")),
        name="in_proj_even",
    )(*args)
    if not has_moe:
        outs = [x] + list(outs)
    return outs


def _win_attn_kernel(sink_ref, bias_ref, q_ref, kp_ref, kc_ref, kn_ref, kx_ref,
                     vp_ref, vc_ref, vn_ref, vx_ref, o_ref):
    blk = A_BLOCK
    g = A_HEADS // A_KV_HEADS
    nwin = 3 * blk
    nkeys = nwin + kx_ref.shape[1]
    bias = jnp.concatenate([bias_ref[0]] * g, axis=0)
    ones_col = jnp.where(lax.broadcasted_iota(jnp.int32, (nkeys, LANES), 1) == 0, 1.0, 0.0).astype(BF16)
    lane = lax.broadcasted_iota(jnp.int32, (blk, LANES), 1)
    low = lane < A_HEAD_DIM
    zero = jnp.zeros((), BF16)
    for smp, h in [(a, b) for a in range(q_ref.shape[0]) for b in range(A_KV_HEADS)]:
        hs = slice(h * LANES, (h + 1) * LANES)
        qs = []
        for cidx in range(g // 2):
            qc = q_ref[smp, :, (h * (g // 2) + cidx) * LANES:(h * (g // 2) + cidx + 1) * LANES]
            qs.append(jnp.where(low, qc, zero))
            qs.append(jnp.where(low, zero, qc))
        qh = jnp.concatenate(qs, axis=0)
        kcat = jnp.concatenate([kp_ref[smp, :, hs], kc_ref[smp, :, hs], kn_ref[smp, :, hs],
                                kx_ref[smp, :, hs]], axis=0)
        vcat = jnp.concatenate([vp_ref[smp, :, hs], vc_ref[smp, :, hs], vn_ref[smp, :, hs],
                                vx_ref[smp, :, hs]], axis=0)
        s = _nt_dot(qh, kcat)
        s = jnp.concatenate([s[:, :nwin] + bias, s[:, nwin:]], axis=1)
        sink = jnp.concatenate(
            [jnp.full((blk, 1), sink_ref[h * g + j] * LOG2_E, F32) for j in range(g)], axis=0)
        m = jnp.maximum(jnp.max(s, axis=-1, keepdims=True), sink)
        p = jnp.exp2(s - m)
        pv = jnp.dot(p.astype(BF16), jnp.concatenate([vcat, ones_col], axis=1),
                     preferred_element_type=F32)
        den = pv[:, LANES:LANES + 1] + jnp.exp2(sink - m)
        o = pv[:, :LANES] * (1.0 / den)
        for cidx in range(g // 2):
            o_even = o[(2 * cidx) * blk:(2 * cidx + 1) * blk]
            o_odd = o[(2 * cidx + 1) * blk:(2 * cidx + 2) * blk]
            col = (h * (g // 2) + cidx) * LANES
            o_ref[smp, :, col:col + LANES] = jnp.where(low, o_even, o_odd).astype(BF16)


def _win_attn(sink, q, kd, vd, n_lat, n_ctx):
    bsz, t, _ = q.shape
    blk = A_BLOCK
    nb = t // blk
    nlb = n_lat // blk
    kvw = 2 * LANES
    spb = _samples_per_step(bsz, 2 * SAMPLES_PER_STEP)
    prev = pl.BlockSpec((spb, blk, kvw), lambda b, i: (b, jnp.maximum(i - 1, 0), 0))
    cur = pl.BlockSpec((spb, blk, kvw), lambda b, i: (b, i, 0))
    nxt = pl.BlockSpec((spb, blk, kvw), lambda b, i: (b, jnp.minimum(i + 1, nb - 1), 0))
    ctx = pl.BlockSpec((spb, n_ctx, kvw), lambda b, i: (b, t // n_ctx - 1, 0))
    r = jnp.arange(blk)[:, None]
    c = jnp.arange(blk)[None, :]
    yes = jnp.ones((blk, blk), jnp.bool_)
    no = jnp.zeros((blk, blk), jnp.bool_)
    kinds = [(c >= r, yes, c <= r), (no, yes, c <= r), (c >= r, yes, no), (no, no, no)]
    bias = jnp.stack([jnp.where(jnp.concatenate([p, m, n], axis=1), 0.0, NEG_INF)
                      for p, m, n in kinds]).astype(F32)
    kind = lambda i: jnp.where(i >= nlb, 3, jnp.where(i == 0, 1, jnp.where(i == nlb - 1, 2, 0)))
    return pl.pallas_call(
        _win_attn_kernel,
        grid=(bsz // spb, nb),
        in_specs=[pl.BlockSpec(memory_space=pltpu.SMEM),
                  pl.BlockSpec((1, blk, 3 * blk), lambda b, i: (kind(i), 0, 0)),
                  pl.BlockSpec((spb, blk, A_Q_W), lambda b, i: (b, i, 0)),
                  prev, cur, nxt, ctx, prev, cur, nxt, ctx],
        out_specs=pl.BlockSpec((spb, blk, A_Q_W), lambda b, i: (b, i, 0)),
        out_shape=jax.ShapeDtypeStruct((bsz, t, A_Q_W), BF16),
        compiler_params=_cparams(("parallel", "arbitrary")),
        name="window_attention",
    )(sink, bias, q, kd, kd, kd, kd, vd, vd, vd, vd)


def _out_epilogue(tm, s, y, x_ref, mod_ref, g2_ref, rw_ref, xo_ref, hp_ref, aff_ref):
    x = x_ref[s] + mod_ref[s, 0, 2:3, :] * y
    xo_ref[s] = x
    h2 = _rms_mod(x, g2_ref[...], mod_ref[s, 0, 3:4, :], mod_ref[s, 0, 4:5, :])
    h_hi = h2.astype(BF16)
    h_hi32 = h_hi.astype(F32)
    bits = lax.bitcast_convert_type(h_hi32, jnp.uint32)
    half = D_MODEL // 2
    packed = (bits[:, :half] >> 16) | (bits[:, half:] & jnp.uint32(0xFFFF0000))
    for j in range(HP_CHUNKS):
        hp_ref[s, pl.ds(j, tm, stride=HP_CHUNKS), :] = packed[:, j * LANES:(j + 1) * LANES]
    h_lo = (h2 - h_hi32).astype(BF16)
    logits = (_nt_dot(rw_ref[0], h_hi) + _nt_dot(rw_ref[0], h_lo)) + _nt_dot(rw_ref[1], h_hi)
    mx = jnp.max(logits, axis=0, keepdims=True)
    ex = jnp.exp(logits - mx)
    aff_ref[s] = ex / jnp.sum(ex, axis=0, keepdims=True)


def _out_even_kernel(tm, o_ref, u_ref, vn_ref, ws_ref, bs_ref, wo_ref, x_ref, mod_ref, g2_ref,
                     rw_ref, xo_ref, hp_ref, aff_ref):
    dot = functools.partial(jnp.dot, preferred_element_type=F32)
    gw = B_WIDTH // B_GROUPS
    spb = x_ref.shape[0]
    rows = []
    for s in range(spb):
        for cidx in range(tm // B_CHUNK):
            rs = slice(cidx * B_CHUNK, (cidx + 1) * B_CHUNK)
            cols = []
            for g in range(B_GROUPS):
                cs = slice(g * gw, (g + 1) * gw)
                mixed = dot(ws_ref[g], vn_ref[s, rs, cs]) + bs_ref[:, cs]
                cols.append(u_ref[s, rs, cs] * mixed)
            rows.append(jnp.concatenate(cols, axis=1))
    sg = jnp.concatenate(rows, axis=0).astype(BF16)
    o_all = jnp.concatenate([o_ref[s] for s in range(spb)], axis=0)
    y = dot(o_all, wo_ref[:A_Q_W, :]) + dot(sg, wo_ref[A_Q_W:, :])
    for s in range(spb):
        _out_epilogue(tm, s, y[s * tm:(s + 1) * tm], x_ref, mod_ref, g2_ref, rw_ref, xo_ref, hp_ref,
                      aff_ref)


def _out_specs_common(bsz, t, tm, spb):
    row = lambda w: pl.BlockSpec((spb, tm, w), lambda b, i: (b, i, 0))
    out_specs = [row(D_MODEL),
                 pl.BlockSpec((spb, tm * HP_CHUNKS, LANES), lambda b, i: (b, i, 0)),
                 pl.BlockSpec((spb, N_EXPERTS, tm), lambda b, i: (b, 0, i))]
    out_shape = [jax.ShapeDtypeStruct((bsz, t, D_MODEL), F32),
                 jax.ShapeDtypeStruct((bsz, t * HP_CHUNKS, LANES), jnp.uint32),
                 jax.ShapeDtypeStruct((bsz, N_EXPERTS, t), F32)]
    return out_specs, out_shape


def _out_even(o, u, vn, ws, bs, wo, x, mod, g2, rw_t, n_lat):
    bsz, t, _ = x.shape
    tm = ROW_TILE
    is_ctx = _tile_is_ctx(n_lat // tm)
    spb = _samples_per_step(bsz, 2 * SAMPLES_PER_STEP)
    row = lambda w: pl.BlockSpec((spb, tm, w), lambda b, i: (b, i, 0))
    full = lambda a: pl.BlockSpec(a.shape, lambda b, i: (0,) * a.ndim)
    mod_spec = pl.BlockSpec((spb, 1, MOD_ROWS, D_MODEL), lambda b, i: (b, is_ctx(i), 0, 0))
    out_specs, out_shape = _out_specs_common(bsz, t, tm, spb)
    return pl.pallas_call(
        functools.partial(_out_even_kernel, tm),
        grid=(bsz // spb, t // tm),
        in_specs=[row(A_Q_W), row(B_WIDTH), row(B_WIDTH), full(ws), full(bs), full(wo),
                  row(D_MODEL), mod_spec, full(g2), full(rw_t)],
        out_specs=out_specs, out_shape=out_shape,
        compiler_params=_cparams(("parallel", "arbitrary")),
        name="out_proj_even",
    )(o, u, vn, ws, bs, wo, x, mod, g2, rw_t)


def _in_odd_kernel(has_moe, tm, *refs):
    (x_ref, moe_ref, modp_ref, mod_ref, g_ref, w_ref, cos_ref, sin_ref, qg_ref, kvg_ref,
     wuq_ref, wukv_ref, xo_ref, q_ref, k_ref, v_ref, gb_ref, u_ref) = refs
    cos = cos_ref[...]
    sin = sin_ref[...]
    lane = lax.broadcasted_iota(jnp.int32, (tm, LANES), 1)
    half = C_ROPE // 2
    first = lane < C_NOPE + half
    cqo = C_KV_LORA + LANES
    co = cqo + C_Q_LORA
    scale = (C_NOPE + C_ROPE) ** -0.5 * LOG2_E
    ones_col = jnp.where(lane == C_V, 1.0, 0.0)
    voff = C_HEADS * HEAD_PAD
    for s in range(x_ref.shape[0]):
        x = _residual_from_moe(x_ref[s], moe_ref, s, modp_ref[s, 0, 5:6, :], tm)
        xo_ref[s] = x
        h = _rms_mod(x, g_ref[...], mod_ref[s, 0, 0:1, :], mod_ref[s, 0, 1:2, :])
        z = jnp.dot(h.astype(BF16), w_ref[...], preferred_element_type=F32)
        ckv = _rms(z[:, :C_KV_LORA], kvg_ref[...])
        kr = _rope_pairs(z[:, C_KV_LORA:C_KV_LORA + LANES], cos, sin, half, first)
        cq = _rms(z[:, cqo:cqo + C_Q_LORA], qg_ref[...])
        kv = jnp.dot(ckv.astype(BF16), wukv_ref[...], preferred_element_type=F32)
        qq = jnp.dot(cq.astype(BF16), wuq_ref[...], preferred_element_type=F32)
        for hd in range(C_HEADS):
            sl = slice(hd * HEAD_PAD, (hd + 1) * HEAD_PAD)
            q_ref[s, :, sl] = (_rope_pairs(qq[:, sl], cos, sin, half, first) * scale).astype(BF16)
            k_ref[s, :, sl] = (kv[:, sl] + kr).astype(BF16)
            v_ref[s, :, sl] = (kv[:, voff + hd * HEAD_PAD:voff + (hd + 1) * HEAD_PAD]
                               + ones_col).astype(BF16)
        gb_ref[s] = z[:, co:co + D_WIDTH]
        u_ref[s] = z[:, co + D_WIDTH:co + 2 * D_WIDTH] * z[:, co + 2 * D_WIDTH:co + 3 * D_WIDTH]


def _in_odd(x, moe, modp, mod, g1, w_in, cos, sin, qg, kvg, wuq, wukv, n_lat):
    bsz, t, _ = x.shape
    tm = ROW_TILE
    is_ctx = _tile_is_ctx(n_lat // tm)
    spb = SAMPLES_PER_STEP
    row = lambda w: pl.BlockSpec((spb, tm, w), lambda b, i: (b, i, 0))
    mod_spec = pl.BlockSpec((spb, 1, MOD_ROWS, D_MODEL), lambda b, i: (b, is_ctx(i), 0, 0))
    full2 = lambda a: pl.BlockSpec(a.shape, lambda b, i: (0, 0))
    hw = C_HEADS * HEAD_PAD
    return pl.pallas_call(
        functools.partial(_in_odd_kernel, True, tm),
        grid=(bsz // spb, t // tm),
        in_specs=[row(D_MODEL),
                  pl.BlockSpec((spb, tm * ACC_CHUNKS, LANES), lambda b, i: (b, i, 0)),
                  mod_spec, mod_spec, full2(g1), full2(w_in),
                  pl.BlockSpec((tm, LANES), lambda b, i: (i, 0)),
                  pl.BlockSpec((tm, LANES), lambda b, i: (i, 0)),
                  full2(qg), full2(kvg), full2(wuq), full2(wukv)],
        out_specs=[row(D_MODEL), row(hw), row(hw), row(hw), row(D_WIDTH), row(D_WIDTH)],
        out_shape=[jax.ShapeDtypeStruct((bsz, t, D_MODEL), F32),
                   jax.ShapeDtypeStruct((bsz, t, hw), BF16),
                   jax.ShapeDtypeStruct((bsz, t, hw), BF16),
                   jax.ShapeDtypeStruct((bsz, t, hw), BF16),
                   jax.ShapeDtypeStruct((bsz, t, D_WIDTH), F32),
                   jax.ShapeDtypeStruct((bsz, t, D_WIDTH), F32)],
        compiler_params=_cparams(("parallel", "arbitrary")),
        name="in_proj_odd",
    )(x, moe, modp, mod, g1, w_in, cos, sin, qg, kvg, wuq, wukv)


def _dense_attn_kernel(tk, q_ref, k_ref, v_ref, o_ref):
    tq = q_ref.shape[1]
    nk = k_ref.shape[1] // tk
    for hd in range(q_ref.shape[2] // HEAD_PAD):
        hs = slice(hd * HEAD_PAD, (hd + 1) * HEAD_PAD)
        q = q_ref[0, :, hs]
        m = jnp.full((tq, 1), NEG_INF, F32)
        acc = jnp.zeros((tq, HEAD_PAD), F32)
        for j in range(nk):
            s = _nt_dot(q, k_ref[0, j * tk:(j + 1) * tk, hs])
            m_new = jnp.maximum(m, jnp.max(s, axis=-1, keepdims=True))
            alpha = jnp.exp2(m - m_new)
            p = jnp.exp2(s - m_new)
            acc = alpha * acc + jnp.dot(p.astype(BF16), v_ref[0, j * tk:(j + 1) * tk, hs],
                                        preferred_element_type=F32)
            m = m_new
        o_ref[0, :, hs] = (acc / acc[:, C_V:C_V + 1]).astype(BF16)


def _dense_attn(q, k, v, q_start, q_len, k_start, k_len, tq, heads_per_step):
    bsz, _, hw = q.shape
    tk = ROW_TILE
    q0 = q_start // tq
    kb = k_start // k_len
    hblk = heads_per_step * HEAD_PAD
    return pl.pallas_call(
        functools.partial(_dense_attn_kernel, tk),
        grid=(bsz, C_HEADS // heads_per_step, q_len // tq),
        in_specs=[pl.BlockSpec((1, tq, hblk), lambda b, h, i: (b, q0 + i, h)),
                  pl.BlockSpec((1, k_len, hblk), lambda b, h, i: (b, kb, h)),
                  pl.BlockSpec((1, k_len, hblk), lambda b, h, i: (b, kb, h))],
        out_specs=pl.BlockSpec((1, tq, hblk), lambda b, h, i: (b, i, h)),
        out_shape=jax.ShapeDtypeStruct((bsz, q_len, hw), BF16),
        compiler_params=_cparams(("parallel", "parallel", "arbitrary")),
        name="dense_attention",
    )(q, k, v)


def _out_odd_kernel(tm, n_lat_tiles, n_tiles, ol_ref, oc_ref, gb_ref, u_ref, up_ref, un_ref, cw_ref,
                    wo_ref, x_ref, mod_ref, g2_ref, rw_ref, xo_ref, hp_ref, aff_ref):
    i = pl.program_id(1)
    dot = functools.partial(jnp.dot, preferred_element_type=F32)
    has_prev = jnp.logical_and(i != 0, i != n_lat_tiles)
    has_next = jnp.logical_and(i != n_lat_tiles - 1, i != n_tiles - 1)
    hw = C_HEADS * HEAD_PAD
    spb = x_ref.shape[0]
    o_att = []
    cs = []
    for s in range(spb):
        o_att.append(jnp.where(i < n_lat_tiles, ol_ref[s], oc_ref[s]))
        u = u_ref[s]
        prev_row = jnp.where(has_prev, up_ref[s, SUBLANES - 1:SUBLANES, :], 0.0)
        next_row = jnp.where(has_next, un_ref[s, 0:1, :], 0.0)
        ridx = lax.broadcasted_iota(jnp.int32, u.shape, 0)
        u_m1 = jnp.where(ridx == 0, prev_row, pltpu.roll(u, 1, 0))
        u_p1 = jnp.where(ridx == tm - 1, next_row, pltpu.roll(u, tm - 1, 0))
        conv = u_m1 * cw_ref[0:1, :] + u * cw_ref[1:2, :] + u_p1 * cw_ref[2:3, :]
        cs.append((gb_ref[s] * conv).astype(BF16))
    y = (dot(jnp.concatenate(o_att, axis=0), wo_ref[:hw, :])
         + dot(jnp.concatenate(cs, axis=0), wo_ref[hw:, :]))
    for s in range(spb):
        _out_epilogue(tm, s, y[s * tm:(s + 1) * tm], x_ref, mod_ref, g2_ref, rw_ref, xo_ref, hp_ref,
                      aff_ref)


def _out_odd(o_lat, o_ctx, gb, u, cw, wo, x, mod, g2, rw_t, n_lat):
    bsz, t, _ = x.shape
    tm = ROW_TILE
    nt = t // tm
    nlt = n_lat // tm
    is_ctx = _tile_is_ctx(nlt)
    spb = _samples_per_step(bsz, 2 * SAMPLES_PER_STEP)
    row = lambda w: pl.BlockSpec((spb, tm, w), lambda b, i: (b, i, 0))
    full = lambda a: pl.BlockSpec(a.shape, lambda b, i: (0,) * a.ndim)
    mod_spec = pl.BlockSpec((spb, 1, MOD_ROWS, D_MODEL), lambda b, i: (b, is_ctx(i), 0, 0))
    hw = C_HEADS * HEAD_PAD
    per = tm // SUBLANES
    last = t // SUBLANES - 1
    halo_prev = pl.BlockSpec((spb, SUBLANES, D_WIDTH), lambda b, i: (b, jnp.maximum(i * per - 1, 0), 0))
    halo_next = pl.BlockSpec((spb, SUBLANES, D_WIDTH), lambda b, i: (b, jnp.minimum((i + 1) * per, last), 0))
    out_specs, out_shape = _out_specs_common(bsz, t, tm, spb)
    return pl.pallas_call(
        functools.partial(_out_odd_kernel, tm, nlt, nt),
        grid=(bsz // spb, nt),
        in_specs=[pl.BlockSpec((spb, tm, hw), lambda b, i: (b, jnp.minimum(i, nlt - 1), 0)),
                  pl.BlockSpec((spb, tm, hw), lambda b, i: (b, jnp.maximum(i - nlt, 0), 0)),
                  row(D_WIDTH), row(D_WIDTH), halo_prev, halo_next,
                  full(cw), full(wo), row(D_MODEL), mod_spec, full(g2), full(rw_t)],
        out_specs=out_specs, out_shape=out_shape,
        compiler_params=_cparams(("parallel", "arbitrary")),
        name="out_proj_odd",
    )(o_lat, o_ctx, gb, u, u, u, cw, wo, x, mod, g2, rw_t)


def _route_kernel(cap, nblk, width, a_ref, idx_ref, gate_ref):
    n_e = N_EXPERTS
    a = a_ref[0]
    bits = lax.bitcast_convert_type(a, jnp.int32)

    def count(mask):
        c = jnp.sum(jnp.where(mask, 1.0, 0.0), axis=2, keepdims=True)
        return jnp.sum(c, axis=1, keepdims=True)

    def search(it, thr):
        cand = thr | jnp.left_shift(jnp.int32(1), 30 - it)
        return jnp.where(count(bits >= cand) >= cap, cand, thr)

    thr = lax.fori_loop(0, 31, search, jnp.zeros((n_e, 1, 1), jnp.int32))
    gt = bits > thr
    eq = bits == thr
    need = cap - count(gt)

    rows = n_e * nblk
    li = lax.broadcasted_iota(jnp.int32, (width, width), 0)
    lj = lax.broadcasted_iota(jnp.int32, (width, width), 1)
    upper = jnp.where(li <= lj, 1.0, 0.0).astype(BF16)
    ones_sq = jnp.ones((width, width), BF16)
    if nblk > 1:
        ri = lax.broadcasted_iota(jnp.int32, (rows, rows), 0)
        rj = lax.broadcasted_iota(jnp.int32, (rows, rows), 1)
        lower = jnp.where(jnp.logical_and(ri // nblk == rj // nblk, rj < ri), 1.0, 0.0).astype(BF16)

    def prefix(mask3):
        m2 = jnp.where(mask3, 1.0, 0.0).astype(BF16).reshape(rows, width)
        local = jnp.dot(m2, upper, preferred_element_type=F32)
        total = jnp.dot(m2, ones_sq, preferred_element_type=F32)
        if nblk > 1:
            excl = jnp.dot(lower, total.astype(BF16), preferred_element_type=F32)
        else:
            excl = jnp.zeros_like(total)
        return local, total, excl

    l_eq, _, x_eq = prefix(eq)
    tie_rank = (l_eq + x_eq).reshape(n_e, nblk, width)
    sel = jnp.logical_or(gt, jnp.logical_and(eq, tie_rank <= need))
    local, total, excl = prefix(sel)
    pin = local + excl

    slot_l = lax.broadcasted_iota(jnp.int32, (nblk, cap), 1).astype(F32)
    blk_s = lax.broadcasted_iota(jnp.int32, (nblk, cap), 0).astype(F32)
    slot_s = lax.broadcasted_iota(jnp.int32, (cap, width), 0).astype(F32)
    lane_w = lax.broadcasted_iota(jnp.int32, (cap, width), 1).astype(F32)
    ones_r = jnp.ones((SUBLANES, width), BF16)
    reps = cap // width if cap >= width else 1
    a2 = a.reshape(rows, width)
    if nblk > 1:
        pin_hi = jnp.floor(pin * (1.0 / PIN_SPLIT))
        a_1 = a2.astype(BF16)
        rem = a2 - a_1.astype(F32)
        a_2 = rem.astype(BF16)
        a_3 = (rem - a_2.astype(F32)).astype(BF16)
        table = jnp.concatenate([pin_hi.astype(BF16), (pin - PIN_SPLIT * pin_hi).astype(BF16),
                                 a_1, a_2, a_3], axis=1)
    for e in range(n_e):
        rs = slice(e * nblk, (e + 1) * nblk)
        if nblk > 1:
            lo = jnp.concatenate([excl[rs]] * reps, axis=1)[:, :cap]
            hi = lo + jnp.concatenate([total[rs]] * reps, axis=1)[:, :cap]
            oh_t = jnp.where(jnp.logical_and(lo <= slot_l, slot_l < hi), 1.0, 0.0)
            got = _tn_dot(oh_t.astype(BF16), table[rs])
            pin_g = PIN_SPLIT * got[:, :width] + got[:, width:2 * width]
            aff_g = (got[:, 2 * width:3 * width] + got[:, 3 * width:4 * width]) + got[:, 4 * width:]
            blk_row = jnp.sum(oh_t * blk_s, axis=0, keepdims=True)
        else:
            pin_g = jnp.broadcast_to(pin[rs], (cap, width))
            aff_g = jnp.broadcast_to(a2[rs], (cap, width))
            blk_row = jnp.zeros((1, cap), F32)
        ind = jnp.where(pin_g <= slot_s, 1.0, 0.0).astype(BF16)
        cnt_row = _nt_dot(ones_r, ind)[0:1]
        cnt_b = jnp.dot(ind, ones_sq, preferred_element_type=F32)
        picked = jnp.where(lane_w == cnt_b, aff_g, 0.0)
        idx_ref[0, e] = (blk_row * width + cnt_row).astype(jnp.int32)
        gate_ref[0, e] = jnp.sum(picked, axis=1, keepdims=True)


def _route(aff, cap):
    bsz, n_e, n = aff.shape
    width = LANES if n % (LANES * SUBLANES) == 0 else n
    nblk = n // width
    a4 = aff.reshape(bsz, n_e, nblk, width)
    idx, gate = pl.pallas_call(
        functools.partial(_route_kernel, cap, nblk, width),
        grid=(bsz,),
        in_specs=[pl.BlockSpec((1, n_e, nblk, width), lambda b: (b, 0, 0, 0))],
        out_specs=[pl.BlockSpec((1, n_e, 1, cap), lambda b: (b, 0, 0, 0)),
                   pl.BlockSpec((1, n_e, cap, 1), lambda b: (b, 0, 0, 0))],
        out_shape=[jax.ShapeDtypeStruct((bsz, n_e, 1, cap), jnp.int32),
                   jax.ShapeDtypeStruct((bsz, n_e, cap, 1), F32)],
        compiler_params=_cparams(("parallel",)),
        name="expert_choice_routing",
    )(a4)
    return idx.reshape(bsz, n_e, cap), gate.reshape(bsz, n_e, cap)


def _ffn_copies(hp_hbm, out_hbm, hsrc, acc, sem_in, sem_out, grp, nsub, n_per, row0):
    ins = []
    outs = []
    for sb in range(nsub):
        b = grp * nsub + sb
        ins.append(pltpu.make_async_copy(
            hp_hbm.at[b, pl.ds(row0 * HP_CHUNKS, n_per * HP_CHUNKS), :],
            hsrc.at[pl.ds(sb * n_per * HP_CHUNKS, n_per * HP_CHUNKS), :], sem_in.at[sb]))
        outs.append(pltpu.make_async_copy(
            acc.at[pl.ds(sb * n_per * ACC_CHUNKS, n_per * ACC_CHUNKS), :],
            out_hbm.at[b, pl.ds(row0 * ACC_CHUNKS, n_per * ACC_CHUNKS), :], sem_out.at[sb]))
    return ins, outs


def _ffn_gather(idx_ref, hsrc, xt, stride, mi):
    src = pl.multiple_of(idx_ref[0, 0, 0, mi] * HP_CHUNKS, HP_CHUNKS)
    xt[pl.ds(mi, HP_CHUNKS, stride=stride), :] = hsrc[pl.ds(src, HP_CHUNKS), :]


def _ffn_scatter(idx_ref, gate_ref, gate_on, acc, yt, stride, mis):
    dst = []
    val = []
    for mi in mis:
        d = pl.multiple_of(idx_ref[0, 0, 0, mi] * ACC_CHUNKS, ACC_CHUNKS)
        gate = gate_ref[0, 0, 0, mi]
        if gate_on is not None:
            gate = jnp.where(gate_on, gate, 0.0)
        slab = yt[pl.ds(mi, ACC_CHUNKS, stride=stride), :] * gate
        dst.append(d)
        val.append(acc[pl.ds(d, ACC_CHUNKS), :] + slab)
    for d, v in zip(dst, val):
        acc[pl.ds(d, ACC_CHUNKS), :] = v


def _ffn_kernel(m_slots, nsub, n_per, row0, unroll, idx_ref, idxn_ref, idxp_ref, gate_ref, gatep_ref,
                hp_hbm, w1a_ref, w1b_ref, w3a_ref, w3b_ref, w2a_ref, w2b_ref, _moe_in, out_hbm,
                hsrc, acc, xt, xs_buf, act_buf, yt, sem_in, sem_out):
    grp = pl.program_id(0)
    e = pl.program_id(1)
    stride = m_slots + SUBLANES
    ins, outs = _ffn_copies(hp_hbm, out_hbm, hsrc, acc, sem_in, sem_out, grp, nsub, n_per, row0)

    @pl.when(jnp.logical_and(grp == 0, e == 0))
    def _():
        yt[...] = jnp.zeros(yt.shape, F32)

    @pl.when(e == 0)
    def _():
        for cp in ins:
            cp.start()
        acc[...] = jnp.zeros(acc.shape, F32)
        for cp in ins:
            cp.wait()

        def first_gather(c, carry):
            for k in range(unroll):
                _ffn_gather(idx_ref, hsrc, xt, stride, c * unroll + k)
            return carry

        lax.fori_loop(0, m_slots // unroll, first_gather, 0)

    half = D_MODEL // 2
    for j in range(HP_CHUNKS):
        w = xt[pl.ds(j * stride, m_slots), :]
        xs_buf[:, j * LANES:(j + 1) * LANES] = lax.bitcast_convert_type(w << 16, F32).astype(BF16)
        xs_buf[:, half + j * LANES:half + (j + 1) * LANES] = lax.bitcast_convert_type(
            w & jnp.uint32(0xFFFF0000), F32).astype(BF16)

    for c in range(m_slots // unroll):
        _ffn_scatter(idxp_ref, gatep_ref, e > 0, acc, yt, stride, range(c * unroll, (c + 1) * unroll))
    for mi in range(m_slots):
        _ffn_gather(idxn_ref, hsrc, xt, stride, mi)

    dot = functools.partial(jnp.dot, preferred_element_type=F32)
    xs = xs_buf[...]
    fh = EXPERT_FF // 2
    for j, (wa_ref, wb_ref) in enumerate(((w1a_ref, w3a_ref), (w1b_ref, w3b_ref))):
        a = dot(xs, wa_ref[0, 0])
        b = dot(xs, wb_ref[0, 0])
        act_buf[:, j * fh:(j + 1) * fh] = (a * jax.nn.sigmoid(a) * b).astype(BF16)
    seen = xt[pl.ds(0, SUBLANES), :]
    for r in range(SUBLANES, m_slots, SUBLANES):
        seen = seen | xt[pl.ds(r, SUBLANES), :]
    seen = seen | lax.bitcast_convert_type(acc[pl.ds(0, SUBLANES), :], jnp.uint32)
    zero = lax.bitcast_convert_type(seen >> 32, F32)
    tile = (slice(0, 2 * SUBLANES), slice(fh, fh + LANES))
    act_buf[tile] = (act_buf[tile].astype(F32) + jnp.concatenate([zero, zero], axis=0)).astype(BF16)
    y = dot(act_buf[:, :fh], w2a_ref[0, 0]) + dot(act_buf[:, fh:], w2b_ref[0, 0])
    for j in range(ACC_CHUNKS):
        yt[pl.ds(j * stride, m_slots), :] = y[:, j * LANES:(j + 1) * LANES]

    @pl.when(e == N_EXPERTS - 1)
    def _():
        def last_scatter(c, carry):
            _ffn_scatter(idx_ref, gate_ref, None, acc, yt, stride,
                         [c * unroll + k for k in range(unroll)])
            return carry

        lax.fori_loop(0, m_slots // unroll, last_scatter, 0)
        for cp in outs:
            cp.start()
        for cp in outs:
            cp.wait()


def _expert_ffn(idx, gate, hp, w1, w3, w2, layer, moe_buf, nsub, n_per, row0):
    groups, n_e, m_slots = idx.shape
    n_tok = nsub * n_per
    stride = m_slots + SUBLANES
    unroll = 8
    cur = lambda g, e: (g, e, 0, 0)
    nxt = lambda g, e: (g, jnp.minimum(e + 1, n_e - 1), 0, 0)
    prv = lambda g, e: (g, jnp.maximum(e - 1, 0), 0, 0)
    smem = lambda imap: pl.BlockSpec((1, 1, 1, m_slots), imap, memory_space=pltpu.SMEM)
    fh = EXPERT_FF // 2
    w_in = lambda j: pl.BlockSpec((1, 1, D_MODEL, fh), lambda g, e: (layer, e, 0, j))
    w_out = lambda j: pl.BlockSpec((1, 1, fh, D_MODEL), lambda g, e: (layer, e, j, 0))
    idx4 = idx.reshape(groups, n_e, 1, m_slots)
    gate4 = gate.reshape(groups, n_e, 1, m_slots)
    return pl.pallas_call(
        functools.partial(_ffn_kernel, m_slots, nsub, n_per, row0, unroll),
        grid=(groups, n_e),
        in_specs=[smem(cur), smem(nxt), smem(prv), smem(cur), smem(prv),
                  pl.BlockSpec(memory_space=pl.ANY), w_in(0), w_in(1), w_in(0), w_in(1),
                  w_out(0), w_out(1), pl.BlockSpec(memory_space=pl.ANY)],
        out_specs=pl.BlockSpec(memory_space=pl.ANY),
        out_shape=jax.ShapeDtypeStruct(moe_buf.shape, F32),
        scratch_shapes=[pltpu.VMEM((n_tok * HP_CHUNKS, LANES), jnp.uint32),
                        pltpu.VMEM((n_tok * ACC_CHUNKS, LANES), F32),
                        pltpu.VMEM((HP_CHUNKS * stride, LANES), jnp.uint32),
                        pltpu.VMEM((m_slots, D_MODEL), BF16),
                        pltpu.VMEM((m_slots, EXPERT_FF), BF16),
                        pltpu.VMEM((ACC_CHUNKS * stride, LANES), F32),
                        pltpu.SemaphoreType.DMA((nsub,)),
                        pltpu.SemaphoreType.DMA((nsub,))],
        input_output_aliases={12: 0},
        compiler_params=_cparams(("arbitrary", "arbitrary")),
        name="expert_ffn",
    )(idx4, idx4, idx4, gate4, gate4, hp, w1, w1, w3, w3, w2, w2, moe_buf)


def _final_kernel(tm, x_ref, moe_ref, modp_ref, g_ref, o_ref):
    x = _residual_from_moe(x_ref[0], moe_ref, 0, modp_ref[0, 0, 5:6, :], tm)
    o_ref[0] = _rms(x, g_ref[...])


def _final(x, moe, modp, g, n_lat):
    bsz = x.shape[0]
    tm = 4 * ROW_TILE
    return pl.pallas_call(
        functools.partial(_final_kernel, tm),
        grid=(bsz, n_lat // tm),
        in_specs=[pl.BlockSpec((1, tm, D_MODEL), lambda b, i: (b, i, 0)),
                  pl.BlockSpec((1, tm * ACC_CHUNKS, LANES), lambda b, i: (b, i, 0)),
                  pl.BlockSpec((1, 1, MOD_ROWS, D_MODEL), lambda b, i: (b, 0, 0, 0)),
                  pl.BlockSpec(g.shape, lambda b, i: (0, 0))],
        out_specs=pl.BlockSpec((1, tm, D_MODEL), lambda b, i: (b, i, 0)),
        out_shape=jax.ShapeDtypeStruct((bsz, n_lat, D_MODEL), F32),
        compiler_params=_cparams(("parallel", "arbitrary")),
        name="final_norm",
    )(x, moe, modp, g)


def _rope_angles(rows, rot_dim):
    row = jnp.repeat(jnp.arange(rows, dtype=F32), GRID_W)
    col = jnp.tile(jnp.arange(GRID_W, dtype=F32), rows)
    n_freq = rot_dim // 4
    inv_freq = ROPE_BASE ** (-jnp.arange(n_freq, dtype=F32) / n_freq)
    ang = jnp.concatenate([row[:, None] * inv_freq[None, :], col[:, None] * inv_freq[None, :]], axis=-1)
    return jnp.cos(ang), jnp.sin(ang)


def _rope_tables_even(n_lat, n_ctx):
    cos, sin = _rope_angles(n_lat // GRID_W, A_HEAD_DIM)
    cos_h = jnp.concatenate([cos, cos], axis=1)
    sin_h = jnp.concatenate([-sin, sin], axis=1)
    reps = LANES // A_HEAD_DIM
    cos_l = jnp.tile(cos_h, (1, reps))
    sin_l = jnp.tile(sin_h, (1, reps))
    return (jnp.concatenate([cos_l, jnp.ones((n_ctx, LANES), F32)], axis=0),
            jnp.concatenate([sin_l, jnp.zeros((n_ctx, LANES), F32)], axis=0))


def _rope_tables_odd(n_lat, n_ctx):
    cos, sin = _rope_angles(n_lat // GRID_W, C_ROPE)
    n = cos.shape[0]
    pad = LANES - C_NOPE - C_ROPE
    cos_l = jnp.concatenate([jnp.ones((n, C_NOPE), F32), cos, cos, jnp.ones((n, pad), F32)], axis=1)
    sin_l = jnp.concatenate([jnp.zeros((n, C_NOPE), F32), -sin, sin, jnp.zeros((n, pad), F32)], axis=1)
    return (jnp.concatenate([cos_l, jnp.ones((n_ctx, LANES), F32)], axis=0),
            jnp.concatenate([sin_l, jnp.zeros((n_ctx, LANES), F32)], axis=0))


def _even_w_in(w):
    k = w[:, :A_KV_W]
    v = w[:, A_KV_W:2 * A_KV_W]
    dup = lambda a: jnp.concatenate(
        [a[:, h * A_HEAD_DIM:(h + 1) * A_HEAD_DIM] for h in range(A_KV_HEADS) for _ in range(2)], axis=1)
    return jnp.concatenate([dup(k), dup(v), w[:, 2 * A_KV_W:]], axis=1).astype(BF16)


def _odd_w_in(w):
    d = w.shape[0]
    kr = w[:, C_KV_LORA:C_KV_LORA + C_ROPE]
    kr_group = jnp.concatenate(
        [jnp.zeros((d, C_NOPE), F32), kr, jnp.zeros((d, LANES - C_NOPE - C_ROPE), F32)], axis=1)
    return jnp.concatenate([w[:, :C_KV_LORA], kr_group, w[:, C_KV_LORA + C_ROPE:]], axis=1).astype(BF16)


def _odd_w_uq(w):
    r = w.shape[0]
    w3 = w.reshape(r, C_HEADS, C_NOPE + C_ROPE)
    pad = jnp.zeros((r, C_HEADS, HEAD_PAD - C_NOPE - C_ROPE), F32)
    return jnp.concatenate([w3, pad], axis=2).reshape(r, C_HEADS * HEAD_PAD).astype(BF16)


def _odd_w_ukv(w):
    r = w.shape[0]
    w3 = w.reshape(r, C_HEADS, C_NOPE + C_V)
    kpad = jnp.zeros((r, C_HEADS, HEAD_PAD - C_NOPE), F32)
    vpad = jnp.zeros((r, C_HEADS, HEAD_PAD - C_V), F32)
    kpart = jnp.concatenate([w3[:, :, :C_NOPE], kpad], axis=2).reshape(r, C_HEADS * HEAD_PAD)
    vpart = jnp.concatenate([w3[:, :, C_NOPE:], vpad], axis=2).reshape(r, C_HEADS * HEAD_PAD)
    return jnp.concatenate([kpart, vpart], axis=1).astype(BF16)


def _odd_w_out(w):
    d = w.shape[1]
    att = w[:C_HEADS * C_V].reshape(C_HEADS, C_V, d)
    att = jnp.concatenate([att, jnp.zeros((C_HEADS, HEAD_PAD - C_V, d), F32)], axis=1)
    return jnp.concatenate([att.reshape(C_HEADS * HEAD_PAD, d), w[C_HEADS * C_V:]], axis=0).astype(BF16)


def _mod_table(mods_l, bsz):
    lat = mods_l[:bsz].reshape(bsz, 1, 6, D_MODEL)
    ctx = jnp.broadcast_to(mods_l[bsz].reshape(1, 1, 6, D_MODEL), (bsz, 1, 6, D_MODEL))
    tab = jnp.concatenate([lat, ctx], axis=1)
    return jnp.pad(tab, ((0, 0), (0, 0), (0, MOD_ROWS - 6), (0, 0)))


def kernel(x, c, ctx, c_ctx, mod_w, mod_b, norm1_g, norm2_g, ev_w_in, ev_sink, ev_sgu_norm_g, ev_sgu_w, ev_sgu_b, ev_w_out, od_w_in, od_q_norm_g, od_w_uq, od_kv_norm_g, od_w_ukv, od_conv_w, od_w_out, router_w, exp_w1, exp_w3, exp_w2, final_g):
    bsz, n_lat, _ = x.shape
    n_ctx = ctx.shape[1]
    t = n_lat + n_ctx
    cap_lat = EC_FACTOR * n_lat // N_EXPERTS
    cap_ctx = EC_FACTOR * n_ctx // N_EXPERTS

    mod_rows = -(-(bsz + 1) // SUBLANES) * SUBLANES
    cc = jnp.concatenate([c, c_ctx[None, :], jnp.zeros((mod_rows - bsz - 1, D_MODEL), F32)], axis=0)
    mods = _modulation(cc, mod_w, mod_b)
    tabs = [_mod_table(mods[l], bsz) for l in range(DEPTH)]

    cos_e, sin_e = _rope_tables_even(n_lat, n_ctx)
    cos_o, sin_o = _rope_tables_odd(n_lat, n_ctx)

    xs = jnp.concatenate([x, ctx], axis=1)
    w1 = exp_w1.astype(BF16)
    w3 = exp_w3.astype(BF16)
    w2 = exp_w2.astype(BF16)
    moe = None
    for layer in range(DEPTH):
        i = layer // 2
        need_ctx = layer < DEPTH - 1
        g1 = norm1_g[layer][None, :]
        g2 = norm2_g[layer][None, :]
        rw_f = router_w[layer].T
        rw_hi = rw_f.astype(BF16)
        rw_t = jnp.stack([rw_hi, (rw_f - rw_hi.astype(F32)).astype(BF16)])
        modp = tabs[layer - 1] if layer > 0 else None
        if layer % 2 == 0:
            xs, q, kd, vd, u, vn = _in_even(xs, moe, modp, tabs[layer], g1, _even_w_in(ev_w_in[i]),
                                            cos_e, sin_e, ev_sgu_norm_g[i][None, :], n_lat)
            o = _win_attn(ev_sink[i], q, kd, vd, n_lat, n_ctx)
            bs = jnp.repeat(ev_sgu_b[i].T, B_WIDTH // B_GROUPS, axis=1)
            xs, hp, aff = _out_even(o, u, vn, ev_sgu_w[i].astype(BF16), bs, ev_w_out[i].astype(BF16),
                                    xs, tabs[layer], g2, rw_t, n_lat)
        else:
            xs, q, k, v, gb, u = _in_odd(xs, moe, modp, tabs[layer], g1, _odd_w_in(od_w_in[i]),
                                         cos_o, sin_o, od_q_norm_g[i][None, :], od_kv_norm_g[i][None, :],
                                         _odd_w_uq(od_w_uq[i]), _odd_w_ukv(od_w_ukv[i]), n_lat)
            o_lat = _dense_attn(q, k, v, 0, n_lat, 0, t, 4 * ROW_TILE, 1)
            o_ctx = _dense_attn(q, k, v, n_lat, n_ctx, n_lat, n_ctx, n_ctx, C_HEADS)
            xs, hp, aff = _out_odd(o_lat, o_ctx, gb, u, od_conv_w[i], _odd_w_out(od_w_out[i]), xs, tabs[layer],
                                   g2, rw_t, n_lat)
        if moe is None:
            moe = jnp.zeros((bsz, t * ACC_CHUNKS, LANES), F32)
        idx, gate = _route(aff[:, :, :n_lat], cap_lat)
        moe = _expert_ffn(idx, gate, hp, w1, w3, w2, layer, moe, 1, n_lat, 0)
        if need_ctx:
            idx_c, gate_c = _route(aff[:, :, n_lat:], cap_ctx)
            idx_c = idx_c + (jnp.arange(bsz, dtype=jnp.int32) * n_ctx)[:, None, None]
            idx_c = jnp.transpose(idx_c, (1, 0, 2)).reshape(1, N_EXPERTS, bsz * cap_ctx)
            gate_c = jnp.transpose(gate_c, (1, 0, 2)).reshape(1, N_EXPERTS, bsz * cap_ctx)
            moe = _expert_ffn(idx_c, gate_c, hp, w1, w3, w2, layer, moe, bsz, n_ctx, n_lat)
    return _final(xs, moe, tabs[DEPTH - 1], final_g[None, :], n_lat)
```
